```python
import math
import jax, jax.numpy as jnp
from jax import lax
import numpy as np

D_MODEL = 2048
BATCH = 32
SEQ = 256
DEPTH = 4
DEC_BATCH = 4
DEC_SEQ = 2048
PAST_LEN = 512

GRID_W = 64
N_BRANCH = 4
BRANCH_W = D_MODEL // 2
GQA_HEAD_DIM = 128
GQA_Q_HEADS = BRANCH_W // GQA_HEAD_DIM
GQA_KV_HEADS = 2
GQA_GROUP = GQA_Q_HEADS // GQA_KV_HEADS
SSD_INNER = BRANCH_W
SSD_HEAD_DIM = 64
SSD_HEADS = SSD_INNER // SSD_HEAD_DIM
SSD_GROUPS = 2
SSD_STATE = 128
SSD_CONV = 5
SSD_CHUNK = 128
SSD_CONV_CH = SSD_INNER + 2 * SSD_GROUPS * SSD_STATE
MLA_HEADS = 8
MLA_NOPE = 128
MLA_ROPE = 64
MLA_V = BRANCH_W // MLA_HEADS
MLA_Q_LORA = 512
MLA_KV_LORA = 256
S5_WIDTH = BRANCH_W
S5_GROUP_CH = 16
S5_GROUPS = S5_WIDTH // S5_GROUP_CH
S5_STATE = 64
FFN_HIDDEN = -(-8 * D_MODEL // (3 * 256)) * 256
Q_BLOCK = 128
ROPE_THETA = 10000.0
NORM_EPS = 1e-6
IN_SPLITS = (N_BRANCH * D_MODEL, GQA_Q_HEADS * GQA_HEAD_DIM, GQA_KV_HEADS * GQA_HEAD_DIM,
             GQA_KV_HEADS * GQA_HEAD_DIM, SSD_INNER, SSD_CONV_CH, 2 * SSD_HEADS,
             MLA_Q_LORA, MLA_KV_LORA + MLA_ROPE, S5_WIDTH)
IN_COLS = sum(IN_SPLITS)

kernel_name = 'hybrid_gqa_ssd_mla_s5_diffusion_step'


def rms_norm(x, g):
    xf = x.astype(jnp.float32)
    y = xf * lax.rsqrt(jnp.mean(xf * xf, axis=-1, keepdims=True) + NORM_EPS)
    return (y * g.astype(jnp.float32)).astype(x.dtype)


def axial_rope_tables(seq_len, dim):
    rows = seq_len // GRID_W
    t = jnp.arange(rows * GRID_W)
    row = (t // GRID_W).astype(jnp.float32)
    col = (t % GRID_W).astype(jnp.float32)
    nf = dim // 4
    inv = ROPE_THETA ** (-jnp.arange(nf, dtype=jnp.float32) / nf)
    ang = jnp.stack([row[:, None] * inv, col[:, None] * inv], axis=1)
    return jnp.cos(ang), jnp.sin(ang)


def apply_axial_rope(x, cos, sin):
    shp = x.shape
    xr = x.astype(jnp.float32).reshape(shp[:-1] + (2, 2, shp[-1] // 4))
    x0, x1 = xr[..., 0, :], xr[..., 1, :]
    cs, sn = cos[:, None], sin[:, None]
    out = jnp.stack([x0 * cs - x1 * sn, x0 * sn + x1 * cs], axis=-2)
    return out.reshape(shp).astype(x.dtype)


def block_attention(q, k, v, scale):
    bsz, lq, g, r, dk = q.shape
    dv = v.shape[-1]
    nb = lq // Q_BLOCK
    qb = jnp.moveaxis(q.reshape(bsz, nb, Q_BLOCK, g, r, dk), 1, 0)

    def one_block(qblk):
        s = jnp.einsum('bqgrd,bkgd->bgrqk', qblk, k).astype(jnp.float32) * scale
        p = jax.nn.softmax(s, axis=-1).astype(v.dtype)
        return jnp.einsum('bgrqk,bkge->bqgre', p, v)

    o = lax.map(one_block, qb)
    return jnp.moveaxis(o, 0, 1).reshape(bsz, lq, g, r, dv)


def dwconv_centred(x, w, bias):
    ch = x.shape[-1]
    pad = w.shape[0] // 2
    out = lax.conv_general_dilated(x, w[:, None, :], window_strides=(1,), padding=((pad, pad),),
                                   dimension_numbers=('NWC', 'WIO', 'NWC'), feature_group_count=ch)
    return out + bias


def segsum(x):
    t = x.shape[-1]
    xr = jnp.broadcast_to(x[..., :, None], x.shape + (t,))
    low = jnp.tril(jnp.ones((t, t), dtype=bool), -1)
    cs = jnp.cumsum(jnp.where(low, xr, 0.0), axis=-2)
    return jnp.where(jnp.tril(jnp.ones((t, t), dtype=bool)), cs, -jnp.inf)


def ssd_scan(x, dt, a_head, bm, cm, h0):
    bsz, l, nh, hp = x.shape
    ng, ns = bm.shape[-2:]
    r = nh // ng
    nc = l // SSD_CHUNK
    xs = (x * dt[..., None]).reshape(bsz, nc, SSD_CHUNK, ng, r, hp)
    da = jnp.moveaxis((dt * a_head).reshape(bsz, nc, SSD_CHUNK, ng, r), 2, -1)
    bc = bm.reshape(bsz, nc, SSD_CHUNK, ng, ns)
    cc = cm.reshape(bsz, nc, SSD_CHUNK, ng, ns)
    a_cum = jnp.cumsum(da, axis=-1)
    cb = jnp.einsum('bclgn,bcsgn->bcgls', cc, bc)
    att = cb[:, :, :, None] * jnp.exp(segsum(da))
    y_diag = jnp.einsum('bcgrls,bcsgrp->bclgrp', att, xs)
    decay_states = jnp.exp(a_cum[..., -1:] - a_cum)
    states = jnp.einsum('bcsgn,bcgrs,bcsgrp->bcgrpn', bc, decay_states, xs)
    states = jnp.concatenate([h0.astype(jnp.float32).reshape(bsz, 1, ng, r, hp, ns), states], axis=1)
    chunk_a = jnp.pad(jnp.moveaxis(a_cum[..., -1], 1, -1), ((0, 0), (0, 0), (0, 0), (1, 0)))
    new_states = jnp.einsum('bgrzc,bcgrpn->bzgrpn', jnp.exp(segsum(chunk_a)), states)
    y_off = jnp.einsum('bclgn,bcgrpn,bcgrl->bclgrp', cc, new_states[:, :-1], jnp.exp(a_cum))
    y = (y_diag + y_off).reshape(bsz, l, nh, hp)
    return y, new_states[:, -1].reshape(bsz, nh, hp, ns)


def complex_linear_combine(e1, e2):
    a1r, a1i, b1r, b1i = e1
    a2r, a2i, b2r, b2i = e2
    return (a2r * a1r - a2i * a1i, a2r * a1i + a2i * a1r,
            a2r * b1r - a2i * b1i + b2r, a2r * b1i + a2i * b1r + b2i)


def gqa_mixer(q, k, v, lp, rope, ctx_kv):
    bsz, l, _ = q.shape
    q = rms_norm(q.reshape(bsz, l, GQA_Q_HEADS, GQA_HEAD_DIM), lp['gqa_qn_g'])
    k = rms_norm(k.reshape(bsz, l, GQA_KV_HEADS, GQA_HEAD_DIM), lp['gqa_kn_g'])
    v = v.reshape(bsz, l, GQA_KV_HEADS, GQA_HEAD_DIM)
    own = (k, v)
    if ctx_kv is not None:
        cos, sin = rope
        q = apply_axial_rope(q, cos, sin)
        k = jnp.concatenate([apply_axial_rope(k, cos, sin), ctx_kv[0]], axis=1)
        v = jnp.concatenate([v, ctx_kv[1]], axis=1)
    o = block_attention(q.reshape(bsz, l, GQA_KV_HEADS, GQA_GROUP, GQA_HEAD_DIM), k, v, GQA_HEAD_DIM ** -0.5)
    return o.reshape(bsz, l, GQA_Q_HEADS * GQA_HEAD_DIM), own


def ssd_mixer(z, xbc, dt_raw, lp, h0):
    bsz, l, _ = z.shape
    xbc = jax.nn.silu(dwconv_centred(xbc, lp['ssd_conv_w'], lp['ssd_conv_b']))
    xs, bm, cm = jnp.split(xbc, [SSD_INNER, SSD_INNER + SSD_GROUPS * SSD_STATE], axis=-1)
    xs = xs.reshape(bsz, l, SSD_HEADS, SSD_HEAD_DIM).astype(jnp.float32)
    bm = bm.reshape(bsz, l, SSD_GROUPS, SSD_STATE).astype(jnp.float32)
    cm = cm.reshape(bsz, l, SSD_GROUPS, SSD_STATE).astype(jnp.float32)
    dt = jax.nn.softplus(dt_raw.astype(jnp.float32).reshape(bsz, l, 2, SSD_HEADS)
                         + lp['ssd_dt_bias'].astype(jnp.float32))
    a = -jnp.exp(lp['ssd_a_log'].astype(jnp.float32))
    rev = lambda t: jnp.flip(t, axis=1)
    y_f, h_f = ssd_scan(xs, dt[:, :, 0], a[0], bm, cm, h0[:, 0])
    y_b, h_b = ssd_scan(rev(xs), rev(dt[:, :, 1]), a[1], rev(bm), rev(cm), h0[:, 1])
    y = y_f + rev(y_b) + lp['ssd_d'].astype(jnp.float32)[:, None] * xs
    y = y.reshape(bsz, l, SSD_INNER) * jax.nn.silu(z.astype(jnp.float32))
    return rms_norm(y, lp['ssd_norm_g']).astype(z.dtype), jnp.stack([h_f, h_b], axis=1)


def mla_mixer(qd, kvd, lp, rope, ctx):
    bsz, l, _ = qd.shape
    q = (rms_norm(qd, lp['mla_qn_g']) @ lp['mla_w_uq']).reshape(bsz, l, MLA_HEADS, MLA_NOPE + MLA_ROPE)
    q_nope, q_pe = jnp.split(q, [MLA_NOPE], axis=-1)
    ckv, kpe = jnp.split(kvd, [MLA_KV_LORA], axis=-1)
    ckv = rms_norm(ckv, lp['mla_kvn_g'])
    own = (ckv, kpe)
    if ctx is not None:
        cos, sin = rope
        q_pe = apply_axial_rope(q_pe, cos, sin)
        kpe = apply_axial_rope(kpe[:, :, None, :], cos, sin)[:, :, 0]
        ckv = jnp.concatenate([ckv, ctx[0]], axis=1)
        kpe = jnp.concatenate([kpe, ctx[1]], axis=1)
    lk = ckv.shape[1]
    kv = (ckv @ lp['mla_w_ukv']).reshape(bsz, lk, MLA_HEADS, MLA_NOPE + MLA_V)
    k_nope, v = jnp.split(kv, [MLA_NOPE], axis=-1)
    k = jnp.concatenate([k_nope, jnp.broadcast_to(kpe[:, :, None, :], (bsz, lk, MLA_HEADS, MLA_ROPE))], axis=-1)
    q = jnp.concatenate([q_nope, q_pe], axis=-1)[:, :, :, None, :]
    o = block_attention(q, k, v, (MLA_NOPE + MLA_ROPE) ** -0.5)
    return o.reshape(bsz, l, MLA_HEADS * MLA_V), own


def s5_mixer(u, lp, h0):
    bsz, l, _ = u.shape
    f32 = jnp.float32
    uf = u.astype(f32)
    ug = uf.reshape(bsz, l, S5_GROUPS, S5_GROUP_CH)
    bu_re = jnp.einsum('blgh,gph->lbgp', ug, lp['s5_b_re'].astype(f32))
    bu_im = jnp.einsum('blgh,gph->lbgp', ug, lp['s5_b_im'].astype(f32))
    lam_re = lp['s5_lam_re'].astype(f32)
    lam_im = lp['s5_lam_im'].astype(f32)
    step = jnp.exp(lp['s5_log_step'].astype(f32))[..., None]
    mag = jnp.exp(lam_re * step)
    ab_re = mag * jnp.cos(lam_im * step)
    ab_im = mag * jnp.sin(lam_im * step)
    den = lam_re * lam_re + lam_im * lam_im
    k_re = ((ab_re - 1.0) * lam_re + ab_im * lam_im) / den
    k_im = (ab_im * lam_re - (ab_re - 1.0) * lam_im) / den
    h0 = h0.astype(f32)
    sums_re, sums_im, finals = [], [], []
    for d in range(2):
        b_re = k_re[d] * bu_re - k_im[d] * bu_im
        b_im = k_re[d] * bu_im + k_im[d] * bu_re
        first = 0 if d == 0 else l - 1
        last = l - 1 if d == 0 else 0
        h0_re, h0_im = h0[:, d, 0], h0[:, d, 1]
        b_re = b_re.at[first].add(ab_re[d] * h0_re - ab_im[d] * h0_im)
        b_im = b_im.at[first].add(ab_re[d] * h0_im + ab_im[d] * h0_re)
        a_re = jnp.broadcast_to(ab_re[d], (l, 1, S5_GROUPS, S5_STATE))
        a_im = jnp.broadcast_to(ab_im[d], (l, 1, S5_GROUPS, S5_STATE))
        _, _, h_re, h_im = lax.associative_scan(complex_linear_combine, (a_re, a_im, b_re, b_im), reverse=(d == 1))
        sums_re.append(h_re)
        sums_im.append(h_im)
        finals.append(jnp.stack([h_re[last], h_im[last]], axis=1))
    h_re = sums_re[0] + sums_re[1]
    h_im = sums_im[0] + sums_im[1]
    y = (jnp.einsum('lbgp,ghp->blgh', h_re, lp['s5_c_re'].astype(f32))
         - jnp.einsum('lbgp,ghp->blgh', h_im, lp['s5_c_im'].astype(f32))).reshape(bsz, l, S5_WIDTH)
    y = jax.nn.gelu(y + lp['s5_d'].astype(f32) * uf)
    val, gate = jnp.split(y @ lp['s5_w_glu'].astype(f32), 2, axis=-1)
    return (val * jax.nn.sigmoid(gate)).astype(u.dtype), jnp.stack(finals, axis=1)


def token_mixers(h, lp, cache, rope_a, rope_c):
    bsz, l, _ = h.shape
    bounds, acc = [], 0
    for w in IN_SPLITS[:-1]:
        acc += w
        bounds.append(acc)
    (gate_pre, gq, gk, gv, sz, sxbc, sdt, mqd, mkvd, s5u) = jnp.split(h @ lp['w_in'], bounds, axis=-1)
    if cache is None:
        ctx_gqa, ctx_mla = None, None
        ssd_h0 = jnp.zeros((bsz, 2, SSD_HEADS, SSD_HEAD_DIM, SSD_STATE), jnp.float32)
        s5_h0 = jnp.zeros((bsz, 2, 2, S5_GROUPS, S5_STATE), jnp.float32)
    else:
        ctx_gqa = (cache[0], cache[1])
        ctx_mla = (cache[2], cache[3])
        ssd_h0, s5_h0 = cache[4], cache[5]
    o_a, (kk, vv) = gqa_mixer(gq, gk, gv, lp, rope_a, ctx_gqa)
    o_b, ssd_state = ssd_mixer(sz, sxbc, sdt, lp, ssd_h0)
    o_c, (ckv, kpe) = mla_mixer(mqd, mkvd, lp, rope_c, ctx_mla)
    o_d, s5_state = s5_mixer(s5u, lp, s5_h0)
    branches = jnp.stack([o_a, o_b, o_c, o_d], axis=2)
    gates = jax.nn.sigmoid(gate_pre.astype(jnp.float32)).reshape(bsz, l, N_BRANCH, D_MODEL).astype(h.dtype)
    proj = jnp.einsum('blnw,nwd->blnd', branches, lp['w_branch'])
    mixed = jnp.sum(gates * proj, axis=2) @ lp['w_out']
    return mixed, (kk, vv, ckv, kpe, ssd_state, s5_state)


def trunk_layer(x, cond_mod, lp, cache, rope_a, rope_c):
    sh1, sc1, g1, sh2, sc2, g2 = jnp.split(cond_mod, 6, axis=-1)
    h = rms_norm(x, lp['norm1_g']) * (1.0 + sc1) + sh1
    mixed, ctx = token_mixers(h, lp, cache, rope_a, rope_c)
    x = x + g1 * mixed
    h = rms_norm(x, lp['norm2_g']) * (1.0 + sc2) + sh2
    gate, up = jnp.split(h @ lp['w_ffn_in'], 2, axis=-1)
    x = x + g2 * ((jax.nn.silu(gate) * up) @ lp['w_ffn_out'])
    return x, ctx


def setup_inputs(seed: int = 0) -> dict:
    key = jax.random.key(seed)
    ks = iter(jax.random.split(key, 64))
    f32 = jnp.float32
    nrm = lambda shape, scale: scale * jax.random.normal(next(ks), shape, f32)
    gain = lambda shape: 1.0 + 0.01 * jax.random.normal(next(ks), shape, f32)
    unif = lambda shape, lo, hi: jax.random.uniform(next(ks), shape, f32, lo, hi)
    dt0 = jnp.exp(unif((DEPTH, 2, SSD_HEADS), math.log(1e-3), math.log(1e-1)))
    return {
        'x_prompt': nrm((BATCH, SEQ, D_MODEL), 1.0),
        'x_sample': nrm((DEC_BATCH, DEC_SEQ, D_MODEL), 1.0),
        'cache_gqa_k': nrm((DEC_BATCH, DEPTH, PAST_LEN, GQA_KV_HEADS, GQA_HEAD_DIM), 1.0),
        'cache_gqa_v': nrm((DEC_BATCH, DEPTH, PAST_LEN, GQA_KV_HEADS, GQA_HEAD_DIM), 1.0),
        'cache_mla_ckv': nrm((DEC_BATCH, DEPTH, PAST_LEN, MLA_KV_LORA), 1.0),
        'cache_mla_kpe': nrm((DEC_BATCH, DEPTH, PAST_LEN, MLA_ROPE), 1.0),
        'state_ssd': nrm((DEC_BATCH, DEPTH, 2, SSD_HEADS, SSD_HEAD_DIM, SSD_STATE), 0.1),
        'state_s5': nrm((DEC_BATCH, DEPTH, 2, 2, S5_GROUPS, S5_STATE), 0.1),
        'c': nrm((DEC_BATCH, D_MODEL), 1.0),
        'c_ctx': nrm((D_MODEL,), 1.0),
        'norm1_g': gain((DEPTH, D_MODEL)),
        'norm2_g': gain((DEPTH, D_MODEL)),
        'w_mod': nrm((DEPTH, D_MODEL, 6 * D_MODEL), 0.5 * D_MODEL ** -0.5),
        'b_mod': nrm((DEPTH, 6 * D_MODEL), 0.02),
        'w_in': nrm((DEPTH, D_MODEL, IN_COLS), D_MODEL ** -0.5),
        'gqa_qn_g': gain((DEPTH, GQA_HEAD_DIM)),
        'gqa_kn_g': gain((DEPTH, GQA_HEAD_DIM)),
        'ssd_conv_w': nrm((DEPTH, SSD_CONV, SSD_CONV_CH), SSD_CONV ** -0.5),
        'ssd_conv_b': nrm((DEPTH, SSD_CONV_CH), 0.02),
        'ssd_a_log': jnp.log(unif((DEPTH, 2, SSD_HEADS), 1.0, 16.0)),
        'ssd_dt_bias': dt0 + jnp.log(-jnp.expm1(-dt0)),
        'ssd_d': gain((DEPTH, SSD_HEADS)),
        'ssd_norm_g': gain((DEPTH, SSD_INNER)),
        'mla_qn_g': gain((DEPTH, MLA_Q_LORA)),
        'mla_w_uq': nrm((DEPTH, MLA_Q_LORA, MLA_HEADS * (MLA_NOPE + MLA_ROPE)), MLA_Q_LORA ** -0.5),
        'mla_kvn_g': gain((DEPTH, MLA_KV_LORA)),
        'mla_w_ukv': nrm((DEPTH, MLA_KV_LORA, MLA_HEADS * (MLA_NOPE + MLA_V)), MLA_KV_LORA ** -0.5),
        's5_lam_re': -0.5 + nrm((DEPTH, 2, S5_GROUPS, S5_STATE), 0.01),
        's5_lam_im': math.pi * jnp.arange(S5_STATE, dtype=f32) + nrm((DEPTH, 2, S5_GROUPS, S5_STATE), 0.01),
        's5_log_step': unif((DEPTH, 2, S5_GROUPS), math.log(1e-3), math.log(1e-1)),
        's5_b_re': nrm((DEPTH, S5_GROUPS, S5_STATE, S5_GROUP_CH), (2 * S5_GROUP_CH) ** -0.5),
        's5_b_im': nrm((DEPTH, S5_GROUPS, S5_STATE, S5_GROUP_CH), (2 * S5_GROUP_CH) ** -0.5),
        's5_c_re': nrm((DEPTH, S5_GROUPS, S5_GROUP_CH, S5_STATE), S5_STATE ** -0.5),
        's5_c_im': nrm((DEPTH, S5_GROUPS, S5_GROUP_CH, S5_STATE), S5_STATE ** -0.5),
        's5_d': nrm((DEPTH, S5_WIDTH), 0.5),
        's5_w_glu': nrm((DEPTH, S5_WIDTH, 2 * S5_WIDTH), S5_WIDTH ** -0.5),
        'w_branch': nrm((DEPTH, N_BRANCH, BRANCH_W, D_MODEL), BRANCH_W ** -0.5),
        'w_out': nrm((DEPTH, D_MODEL, D_MODEL), D_MODEL ** -0.5),
        'w_ffn_in': nrm((DEPTH, D_MODEL, 2 * FFN_HIDDEN), D_MODEL ** -0.5),
        'w_ffn_out': nrm((DEPTH, FFN_HIDDEN, D_MODEL), FFN_HIDDEN ** -0.5),
        'final_g': gain((D_MODEL,)),
    }


def reference(x_prompt, x_sample, cache_gqa_k, cache_gqa_v, cache_mla_ckv, cache_mla_kpe, state_ssd, state_s5,
              c, c_ctx, norm1_g, norm2_g, w_mod, b_mod, w_in, gqa_qn_g, gqa_kn_g, ssd_conv_w, ssd_conv_b,
              ssd_a_log, ssd_dt_bias, ssd_d, ssd_norm_g, mla_qn_g, mla_w_uq, mla_kvn_g, mla_w_ukv,
              s5_lam_re, s5_lam_im, s5_log_step, s5_b_re, s5_b_im, s5_c_re, s5_c_im, s5_d, s5_w_glu,
              w_branch, w_out, w_ffn_in, w_ffn_out, final_g):
    layer_w = dict(norm1_g=norm1_g, norm2_g=norm2_g, w_mod=w_mod, b_mod=b_mod, w_in=w_in,
                   gqa_qn_g=gqa_qn_g, gqa_kn_g=gqa_kn_g, ssd_conv_w=ssd_conv_w, ssd_conv_b=ssd_conv_b,
                   ssd_a_log=ssd_a_log, ssd_dt_bias=ssd_dt_bias, ssd_d=ssd_d, ssd_norm_g=ssd_norm_g,
                   mla_qn_g=mla_qn_g, mla_w_uq=mla_w_uq, mla_kvn_g=mla_kvn_g, mla_w_ukv=mla_w_ukv,
                   s5_lam_re=s5_lam_re, s5_lam_im=s5_lam_im, s5_log_step=s5_log_step, s5_b_re=s5_b_re,
                   s5_b_im=s5_b_im, s5_c_re=s5_c_re, s5_c_im=s5_c_im, s5_d=s5_d, s5_w_glu=s5_w_glu,
                   w_branch=w_branch, w_out=w_out, w_ffn_in=w_ffn_in, w_ffn_out=w_ffn_out)
    lat_len = x_sample.shape[1]
    rope_a = axial_rope_tables(lat_len, GQA_HEAD_DIM)
    rope_c = axial_rope_tables(lat_len, MLA_ROPE)
    xc, xl = x_prompt, x_sample
    outs = ([], [], [], [], [], [])
    for i in range(DEPTH):
        lp = {name: w[i] for name, w in layer_w.items()}
        mod_ctx = (jax.nn.silu(c_ctx) @ lp['w_mod'] + lp['b_mod'])[None, None, :]
        mod_lat = (jax.nn.silu(c) @ lp['w_mod'] + lp['b_mod'])[:, None, :]
        xc, ctx = trunk_layer(xc, mod_ctx, lp, None, None, None)
        for lst, t in zip(outs, ctx):
            lst.append(t)
        cache_l = (cache_gqa_k[:, i], cache_gqa_v[:, i], cache_mla_ckv[:, i], cache_mla_kpe[:, i],
                   state_ssd[:, i], state_s5[:, i])
        xl, _ = trunk_layer(xl, mod_lat, lp, cache_l, rope_a, rope_c)
    y_prompt = rms_norm(xc, final_g)
    y_sample = rms_norm(xl, final_g)
    new_gqa_k = jnp.stack(outs[0], axis=1)
    new_gqa_v = jnp.stack(outs[1], axis=1)
    new_mla_ckv = jnp.stack(outs[2], axis=1)
    new_mla_kpe = jnp.stack(outs[3], axis=1)
    new_ssd = jnp.stack(outs[4], axis=1)
    new_s5 = jnp.stack(outs[5], axis=1)
    return (y_prompt, y_sample, new_gqa_k, new_gqa_v, new_mla_ckv, new_mla_kpe, new_ssd, new_s5)
```

```python
import functools
import math

import jax
import jax.numpy as jnp
from jax import lax
from jax.experimental import pallas as pl
from jax.experimental.pallas import tpu as pltpu

F32 = jnp.float32
BF16 = jnp.bfloat16
HIGHEST = lax.Precision.HIGHEST

GRID_W = 64
N_BRANCH = 4
GQA_HEAD_DIM = 128
GQA_KV_HEADS = 2
SSD_HEAD_DIM = 64
SSD_GROUPS = 2
SSD_STATE = 128
SSD_CONV = 5
SSD_CHUNK = 128
MLA_HEADS = 8
MLA_NOPE = 128
MLA_ROPE = 64
MLA_Q_LORA = 512
MLA_KV_LORA = 256
S5_GROUP_CH = 16
S5_STATE = 64
S5_CHUNK = 16
ROPE_THETA = 10000.0
NORM_EPS = 1e-6
LANE = 128
VMEM_LIMIT = 56 * 1024 * 1024

TM = 1024
TN_IN = 1280
TN_FFN = 512
TM_FFN_OUT = 512
TN_OUT = 512
TR_PREP = 512
TQ_LAT = 512
S5_GB = 8


def _cparams(sem):
    return pltpu.CompilerParams(dimension_semantics=sem, vmem_limit_bytes=VMEM_LIMIT)


def _row_group(i, tm, ctx_rows, lat_len):
    nct = ctx_rows // tm
    per = lat_len // tm
    return jnp.where(i < nct, 0, 1 + (i - nct) // per)


def _silu(x):
    return x * jax.nn.sigmoid(x)


def _softplus(x):
    return jnp.maximum(x, 0.0) + jnp.log(1.0 + jnp.exp(-jnp.abs(x)))


def _rms(x, g):
    ms = jnp.mean(x * x, axis=-1, keepdims=True)
    return x * lax.rsqrt(ms + NORM_EPS) * g


def _mod_kernel(c_ref, w_ref, b_ref, o_ref):
    c = c_ref[...]
    s = _silu(c).astype(BF16)
    o_ref[0] = jnp.dot(s, w_ref[0].astype(BF16), preferred_element_type=F32) + b_ref[0]


def _modulation(cvec, w_mod, b_mod):
    depth, d, n = w_mod.shape
    tn = 1024
    return pl.pallas_call(
        _mod_kernel,
        out_shape=jax.ShapeDtypeStruct((depth, 8, n), F32),
        grid=(depth, n // tn),
        in_specs=[pl.BlockSpec((8, d), lambda l, j: (0, 0)),
                  pl.BlockSpec((1, d, tn), lambda l, j: (l, 0, j)),
                  pl.BlockSpec((1, 1, tn), lambda l, j: (l, 0, j))],
        out_specs=pl.BlockSpec((1, 8, tn), lambda l, j: (l, 0, j)),
        compiler_params=_cparams(("parallel", "parallel")),
    )(cvec, w_mod, b_mod.reshape(depth, 1, n))


def _norm_mod(x_ref, g_ref, sc_ref, sh_ref):
    y = _rms(x_ref[...], g_ref[...])
    return (y * (1.0 + sc_ref[...]) + sh_ref[...]).astype(BF16)


def _in_proj_kernel(x_ref, g_ref, sc_ref, sh_ref, w_ref, ws_ref, o_ref, os_ref, hs_ref):
    @pl.when(pl.program_id(1) == 0)
    def _():
        h = _norm_mod(x_ref, g_ref, sc_ref, sh_ref)
        hs_ref[...] = h
        os_ref[...] = jnp.dot(h, ws_ref[...], preferred_element_type=F32)

    o_ref[...] = jnp.dot(hs_ref[...], w_ref[...], preferred_element_type=F32).astype(o_ref.dtype)


def _in_proj(x, norm_g, mod, w_main, w_small, dims):
    m, d = x.shape
    npad = w_main.shape[1]
    tm, tn = min(TM, dims["lat_len"]), TN_IN
    grp = lambda i: _row_group(i, tm, dims["ctx_rows"], dims["lat_len"])
    return pl.pallas_call(
        _in_proj_kernel,
        out_shape=(jax.ShapeDtypeStruct((m, npad), BF16), jax.ShapeDtypeStruct((m, LANE), F32)),
        grid=(m // tm, npad // tn),
        in_specs=[pl.BlockSpec((tm, d), lambda i, j: (i, 0)),
                  pl.BlockSpec((1, d), lambda i, j: (0, 0)),
                  pl.BlockSpec((None, None, 1, d), lambda i, j: (grp(i), 1, 0, 0)),
                  pl.BlockSpec((None, None, 1, d), lambda i, j: (grp(i), 0, 0, 0)),
                  pl.BlockSpec((d, tn), lambda i, j: (0, j)),
                  pl.BlockSpec((d, LANE), lambda i, j: (0, 0))],
        out_specs=(pl.BlockSpec((tm, tn), lambda i, j: (i, j)),
                   pl.BlockSpec((tm, LANE), lambda i, j: (i, 0))),
        scratch_shapes=[pltpu.VMEM((tm, d), BF16)],
        compiler_params=_cparams(("parallel", "arbitrary")),
    )(x, norm_g.reshape(1, d), mod, mod, w_main, w_small)


def _ffn_in_kernel(x_ref, g_ref, sc_ref, sh_ref, wg_ref, wu_ref, o_ref, hs_ref):
    @pl.when(pl.program_id(1) == 0)
    def _():
        hs_ref[...] = _norm_mod(x_ref, g_ref, sc_ref, sh_ref)

    h = hs_ref[...]
    a = jnp.dot(h, wg_ref[...], preferred_element_type=F32)
    b = jnp.dot(h, wu_ref[...], preferred_element_type=F32)
    o_ref[...] = (_silu(a) * b).astype(o_ref.dtype)


def _ffn_in(x, norm_g, mod, w, dims):
    m, d = x.shape
    hid = w.shape[1] // 2
    tm, tn = min(TM, dims["lat_len"]), TN_FFN
    nj = hid // tn
    grp = lambda i: _row_group(i, tm, dims["ctx_rows"], dims["lat_len"])
    return pl.pallas_call(
        _ffn_in_kernel,
        out_shape=jax.ShapeDtypeStruct((m, hid), BF16),
        grid=(m // tm, nj),
        in_specs=[pl.BlockSpec((tm, d), lambda i, j: (i, 0)),
                  pl.BlockSpec((1, d), lambda i, j: (0, 0)),
                  pl.BlockSpec((None, None, 1, d), lambda i, j: (grp(i), 4, 0, 0)),
                  pl.BlockSpec((None, None, 1, d), lambda i, j: (grp(i), 3, 0, 0)),
                  pl.BlockSpec((d, tn), lambda i, j: (0, j)),
                  pl.BlockSpec((d, tn), lambda i, j: (0, j + nj))],
        out_specs=pl.BlockSpec((tm, tn), lambda i, j: (i, j)),
        scratch_shapes=[pltpu.VMEM((tm, d), BF16)],
        compiler_params=_cparams(("parallel", "arbitrary")),
    )(x, norm_g.reshape(1, d), mod, mod, w, w)


def _resid_kernel(x_ref, w_ref, r_ref, g_ref, o_ref):
    o_ref[...] = r_ref[...] + g_ref[...] * jnp.dot(x_ref[...], w_ref[...], preferred_element_type=F32)


def _resid_proj(xin, w, resid, mod, mod_idx, tm, dims):
    m, k = xin.shape
    n = w.shape[1]
    tm, tn = min(tm, dims["lat_len"]), TN_OUT
    grp = lambda i: _row_group(i, tm, dims["ctx_rows"], dims["lat_len"])
    return pl.pallas_call(
        _resid_kernel,
        out_shape=jax.ShapeDtypeStruct((m, n), F32),
        grid=(m // tm, n // tn),
        in_specs=[pl.BlockSpec((tm, k), lambda i, j: (i, 0)),
                  pl.BlockSpec((k, tn), lambda i, j: (0, j)),
                  pl.BlockSpec((tm, tn), lambda i, j: (i, j)),
                  pl.BlockSpec((None, None, 1, tn), lambda i, j: (grp(i), mod_idx, 0, j))],
        out_specs=pl.BlockSpec((tm, tn), lambda i, j: (i, j)),
        compiler_params=_cparams(("parallel", "arbitrary")),
    )(xin, w, resid, mod)


def _mix_kernel(oa_ref, ob_ref, oc_ref, od_ref, ga_ref, gb_ref, gc_ref, gd_ref, w_ref, o_ref):
    acc = None
    for n, (o_n, g_n) in enumerate(((oa_ref, ga_ref), (ob_ref, gb_ref), (oc_ref, gc_ref), (od_ref, gd_ref))):
        proj = jnp.dot(o_n[...], w_ref[n], preferred_element_type=F32)
        term = jax.nn.sigmoid(g_n[...].astype(F32)) * proj
        acc = term if acc is None else acc + term
    o_ref[...] = acc.astype(o_ref.dtype)


def _branch_mix(branches, proj_all, w_branch, col_gate, dims):
    m, bw = branches[0].shape
    d = w_branch.shape[2]
    tm, tn = min(TM, dims["lat_len"]), TN_OUT
    gate_specs = [pl.BlockSpec((tm, tn), functools.partial(lambda i, j, n: (i, (col_gate + n * d) // tn + j), n=n))
                  for n in range(N_BRANCH)]
    return pl.pallas_call(
        _mix_kernel,
        out_shape=jax.ShapeDtypeStruct((m, d), BF16),
        grid=(m // tm, d // tn),
        in_specs=[pl.BlockSpec((tm, bw), lambda i, j: (i, 0))] * N_BRANCH + gate_specs
                 + [pl.BlockSpec((N_BRANCH, bw, tn), lambda i, j: (0, 0, j))],
        out_specs=pl.BlockSpec((tm, tn), lambda i, j: (i, j)),
        compiler_params=_cparams(("parallel", "arbitrary")),
    )(*branches, proj_all, proj_all, proj_all, proj_all, w_branch)


def _final_norm_kernel(x_ref, g_ref, o_ref):
    o_ref[...] = _rms(x_ref[...], g_ref[...])


def _final_norm(x, g):
    m, d = x.shape
    tm = 512
    return pl.pallas_call(
        _final_norm_kernel,
        out_shape=jax.ShapeDtypeStruct((m, d), F32),
        grid=(m // tm,),
        in_specs=[pl.BlockSpec((tm, d), lambda i: (i, 0)), pl.BlockSpec((1, d), lambda i: (0, 0))],
        out_specs=pl.BlockSpec((tm, d), lambda i: (i, 0)),
        compiler_params=_cparams(("parallel",)),
    )(x, g.reshape(1, d))


def _rope_tables(seq_len, dim):
    nf = dim // 4
    t = jnp.arange(seq_len)
    row = (t // GRID_W).astype(F32)
    col = (t % GRID_W).astype(F32)
    inv = ROPE_THETA ** (-jnp.arange(nf, dtype=F32) / nf)
    ang_r, ang_c = row[:, None] * inv, col[:, None] * inv
    cos = jnp.concatenate([jnp.cos(ang_r), jnp.cos(ang_r), jnp.cos(ang_c), jnp.cos(ang_c)], axis=1)
    sin = jnp.concatenate([-jnp.sin(ang_r), jnp.sin(ang_r), -jnp.sin(ang_c), jnp.sin(ang_c)], axis=1)
    pad = LANE - dim
    if pad:
        cos = jnp.concatenate([cos, jnp.ones((seq_len, pad), F32)], axis=1)
        sin = jnp.concatenate([sin, jnp.zeros((seq_len, pad), F32)], axis=1)
    return cos, sin


def _rope(x, cos, sin, nf):
    lane = lax.broadcasted_iota(jnp.int32, x.shape, 1)
    swapped = jnp.where((lane % (2 * nf)) < nf, pltpu.roll(x, LANE - nf, 1), pltpu.roll(x, nf, 1))
    return x * cos + swapped * sin


def _attn_kernel(q_ref, k_ref, v_ref, o_ref, *, n_rep, dk, dv):
    k = k_ref[0]
    v = v_ref[0]
    for r in range(n_rep):
        q = q_ref[:, r * dk:(r + 1) * dk]
        s = lax.dot_general(q, k, (((1,), (1,)), ((), ())), preferred_element_type=F32)
        p = jnp.exp(s - jnp.max(s, axis=-1, keepdims=True))
        l = jnp.sum(p, axis=-1, keepdims=True)
        o = jnp.dot(p.astype(BF16), v, preferred_element_type=F32)
        o_ref[:, r * dv:(r + 1) * dv] = (o / l).astype(o_ref.dtype)


def _skip_input(kern, pos, *refs):
    return kern(*refs[:pos], *refs[pos + 1:])


def _attention(q, k, v, out_buf, *, nseq, seq_len, tq, n_kv, n_rep, dk, dv, k_col0, v_col0, row0):
    lk = k.shape[1]
    nq = seq_len // tq
    rb0 = row0 // tq
    kern = functools.partial(_skip_input, functools.partial(_attn_kernel, n_rep=n_rep, dk=dk, dv=dv), 3)
    return pl.pallas_call(
        kern,
        out_shape=jax.ShapeDtypeStruct(out_buf.shape, out_buf.dtype),
        grid=(nseq, n_kv, nq),
        in_specs=[pl.BlockSpec((tq, n_rep * dk), lambda b, g, i: (b * nq + i, g)),
                  pl.BlockSpec((1, lk, dk), lambda b, g, i: (b, 0, k_col0 + g)),
                  pl.BlockSpec((1, lk, dv), lambda b, g, i: (b, 0, v_col0 + g)),
                  pl.BlockSpec(memory_space=pl.ANY)],
        out_specs=pl.BlockSpec((tq, n_rep * dv), lambda b, g, i: (rb0 + b * nq + i, g)),
        input_output_aliases={3: 0},
        compiler_params=_cparams(("parallel", "parallel", "arbitrary")),
    )(q, k, v, out_buf)


def _gqa_prep_kernel(*refs, rope, scale, n_q, n_kv):
    if rope:
        q_ref, k_ref, qg_ref, kg_ref, cos_ref, sin_ref, qo_ref, ko_ref = refs
    else:
        q_ref, k_ref, qg_ref, kg_ref, qo_ref, ko_ref, k32_ref = refs
    hd = GQA_HEAD_DIM
    for h in range(n_q):
        y = _rms(q_ref[:, h * hd:(h + 1) * hd].astype(F32), qg_ref[...])
        if rope:
            y = _rope(y, cos_ref[...], sin_ref[...], hd // 4)
        qo_ref[:, h * hd:(h + 1) * hd] = (y * scale).astype(qo_ref.dtype)
    for h in range(n_kv):
        y = _rms(k_ref[:, h * hd:(h + 1) * hd].astype(F32), kg_ref[...])
        if rope:
            y = _rope(y, cos_ref[...], sin_ref[...], hd // 4)
        else:
            k32_ref[:, h * hd:(h + 1) * hd] = y
        ko_ref[:, h * hd:(h + 1) * hd] = y.astype(ko_ref.dtype)


def _gqa_prep(proj, qg, kg, tabs, *, row0, nrows, seq_len, cols):
    tr = min(TR_PREP, seq_len)
    rb0 = row0 // tr
    qw, kw = cols["gq"][1], cols["gk"][1]
    n_q, n_kv = qw // GQA_HEAD_DIM, kw // GQA_HEAD_DIM
    rope = tabs is not None
    in_specs = [pl.BlockSpec((tr, qw), lambda i: (rb0 + i, cols["gq"][0] // qw)),
                pl.BlockSpec((tr, kw), lambda i: (rb0 + i, cols["gk"][0] // kw)),
                pl.BlockSpec((1, GQA_HEAD_DIM), lambda i: (0, 0)),
                pl.BlockSpec((1, GQA_HEAD_DIM), lambda i: (0, 0))]
    args = [proj, proj, qg.reshape(1, -1), kg.reshape(1, -1)]
    out_shape = [jax.ShapeDtypeStruct((nrows, qw), BF16), jax.ShapeDtypeStruct((nrows, kw), BF16)]
    out_specs = [pl.BlockSpec((tr, qw), lambda i: (i, 0)), pl.BlockSpec((tr, kw), lambda i: (i, 0))]
    if rope:
        per = seq_len // tr
        in_specs += [pl.BlockSpec((tr, LANE), lambda i: (i % per, 0))] * 2
        args += list(tabs)
    else:
        out_shape.append(jax.ShapeDtypeStruct((nrows, kw), F32))
        out_specs.append(pl.BlockSpec((tr, kw), lambda i: (i, 0)))
    return pl.pallas_call(
        functools.partial(_gqa_prep_kernel, rope=rope, scale=GQA_HEAD_DIM ** -0.5, n_q=n_q, n_kv=n_kv),
        out_shape=tuple(out_shape),
        grid=(nrows // tr,),
        in_specs=in_specs,
        out_specs=tuple(out_specs),
        compiler_params=_cparams(("parallel",)),
    )(*args)


MLA_KV_IN = MLA_KV_LORA + LANE
MLA_QK = 2 * LANE


def _mla_prep_kernel(*refs, rope):
    if rope:
        qd_ref, ckv_ref, sm_ref, qg_ref, kvg_ref, cos_ref, sin_ref, qo_ref, kvo_ref = refs
    else:
        qd_ref, ckv_ref, sm_ref, qg_ref, kvg_ref, qo_ref, kvo_ref, ckv32_ref = refs
    qo_ref[...] = _rms(qd_ref[...].astype(F32), qg_ref[...]).astype(qo_ref.dtype)
    ckv = _rms(ckv_ref[...].astype(F32), kvg_ref[...])
    kvo_ref[:, :MLA_KV_LORA] = ckv.astype(kvo_ref.dtype)
    sm = sm_ref[...]
    if rope:
        sm = _rope(sm, cos_ref[...], sin_ref[...], MLA_ROPE // 4)
    else:
        ckv32_ref[...] = ckv
    lane = lax.broadcasted_iota(jnp.int32, sm.shape, 1)
    kvo_ref[:, MLA_KV_LORA:] = jnp.where(lane < MLA_ROPE, sm, 0.0).astype(kvo_ref.dtype)


def _mla_prep(proj, small, qg, kvg, tabs, *, row0, nrows, seq_len, cols):
    tr = min(TR_PREP, seq_len)
    rb0 = row0 // tr
    qw, cw = cols["mqd"][1], cols["ckv"][1]
    rope = tabs is not None
    in_specs = [pl.BlockSpec((tr, qw), lambda i: (rb0 + i, cols["mqd"][0] // qw)),
                pl.BlockSpec((tr, cw), lambda i: (rb0 + i, cols["ckv"][0] // cw)),
                pl.BlockSpec((tr, LANE), lambda i: (rb0 + i, 0)),
                pl.BlockSpec((1, qw), lambda i: (0, 0)),
                pl.BlockSpec((1, cw), lambda i: (0, 0))]
    args = [proj, proj, small, qg.reshape(1, -1), kvg.reshape(1, -1)]
    out_shape = [jax.ShapeDtypeStruct((nrows, qw), BF16), jax.ShapeDtypeStruct((nrows, MLA_KV_IN), BF16)]
    out_specs = [pl.BlockSpec((tr, qw), lambda i: (i, 0)), pl.BlockSpec((tr, MLA_KV_IN), lambda i: (i, 0))]
    if rope:
        per = seq_len // tr
        in_specs += [pl.BlockSpec((tr, LANE), lambda i: (i % per, 0))] * 2
        args += list(tabs)
    else:
        out_shape.append(jax.ShapeDtypeStruct((nrows, cw), F32))
        out_specs.append(pl.BlockSpec((tr, cw), lambda i: (i, 0)))
    return pl.pallas_call(
        functools.partial(_mla_prep_kernel, rope=rope),
        out_shape=tuple(out_shape),
        grid=(nrows // tr,),
        in_specs=in_specs,
        out_specs=tuple(out_specs),
        compiler_params=_cparams(("parallel",)),
    )(*args)


def _mla_q_kernel(*refs, rope, scale):
    if rope:
        x_ref, w_ref, cos_ref, sin_ref, o_ref = refs
    else:
        x_ref, w_ref, o_ref = refs
    acc = jnp.dot(x_ref[...], w_ref[...], preferred_element_type=F32) * scale
    if rope:
        o_ref[:, :LANE] = acc[:, :LANE].astype(o_ref.dtype)
        o_ref[:, LANE:] = _rope(acc[:, LANE:], cos_ref[...], sin_ref[...], MLA_ROPE // 4).astype(o_ref.dtype)
    else:
        o_ref[...] = acc.astype(o_ref.dtype)


def _mla_q(qdn, w_q, tabs, *, seq_len):
    nrows, k = qdn.shape
    tm = min(TR_PREP, seq_len)
    rope = tabs is not None
    in_specs = [pl.BlockSpec((tm, k), lambda i, h: (i, 0)), pl.BlockSpec((k, MLA_QK), lambda i, h: (0, h))]
    args = [qdn, w_q]
    if rope:
        per = seq_len // tm
        in_specs += [pl.BlockSpec((tm, LANE), lambda i, h: (i % per, 0))] * 2
        args += list(tabs)
    return pl.pallas_call(
        functools.partial(_mla_q_kernel, rope=rope, scale=(MLA_NOPE + MLA_ROPE) ** -0.5),
        out_shape=jax.ShapeDtypeStruct((nrows, MLA_HEADS * MLA_QK), BF16),
        grid=(nrows // tm, MLA_HEADS),
        in_specs=in_specs,
        out_specs=pl.BlockSpec((tm, MLA_QK), lambda i, h: (i, h)),
        compiler_params=_cparams(("parallel", "arbitrary")),
    )(*args)


def _matmul_kernel(x_ref, w_ref, o_ref):
    o_ref[...] = jnp.dot(x_ref[...], w_ref[...], preferred_element_type=F32).astype(o_ref.dtype)


def _mla_kv(kv_in, w_kv):
    nrows, k = kv_in.shape
    n = w_kv.shape[1]
    tm = 512 if nrows % 512 == 0 else 256
    return pl.pallas_call(
        _matmul_kernel,
        out_shape=jax.ShapeDtypeStruct((nrows, n), BF16),
        grid=(nrows // tm,),
        in_specs=[pl.BlockSpec((tm, k), lambda i: (i, 0)), pl.BlockSpec((k, n), lambda i: (0, 0))],
        out_specs=pl.BlockSpec((tm, n), lambda i: (i, 0)),
        compiler_params=_cparams(("parallel",)),
    )(kv_in, w_kv)


def _conv_kernel(x_ref, w_ref, b_ref, o_ref):
    x = x_ref[0].astype(F32)
    seq = x.shape[0]
    row = lax.broadcasted_iota(jnp.int32, x.shape, 0)
    pad = SSD_CONV // 2
    acc = x * w_ref[pad:pad + 1, :] + b_ref[...]
    for k in range(SSD_CONV):
        d = k - pad
        if d == 0:
            continue
        shifted = pltpu.roll(x, (-d) % seq, 0)
        valid = (row + d >= 0) & (row + d < seq)
        acc = acc + jnp.where(valid, shifted, 0.0) * w_ref[k:k + 1, :]
    o_ref[0] = _silu(acc).astype(o_ref.dtype)


def _ssd_conv(proj, conv_w, conv_b, *, seq0, nseq, seq_len, cols):
    npad = proj.shape[1]
    c0, cw = cols["sxbc"]
    tc = 256
    view = proj.reshape(-1, seq_len, npad)
    return pl.pallas_call(
        _conv_kernel,
        out_shape=jax.ShapeDtypeStruct((nseq, seq_len, cw), BF16),
        grid=(nseq, cw // tc),
        in_specs=[pl.BlockSpec((1, seq_len, tc), lambda s, c: (seq0 + s, 0, c0 // tc + c)),
                  pl.BlockSpec((SSD_CONV, tc), lambda s, c: (0, c)),
                  pl.BlockSpec((1, tc), lambda s, c: (0, c))],
        out_specs=pl.BlockSpec((1, seq_len, tc), lambda s, c: (s, 0, c)),
        compiler_params=_cparams(("parallel", "parallel")),
    )(view, conv_w, conv_b.reshape(1, -1)).reshape(nseq * seq_len, cw)


def _pair_cols(vals, h0):
    q = vals.shape[0]
    lane = lax.broadcasted_iota(jnp.int32, (q, LANE), 1)
    return jnp.where(lane < SSD_HEAD_DIM, vals[:, h0:h0 + 1], vals[:, h0 + 1:h0 + 2])


def _ssd_kernel(*refs, reverse, zero_init, final, d):
    if final:
        (x_ref, b_ref, c_ref, dtc_ref, dtr_ref, biasc_ref, biasr_ref, alogc_ref, alogr_ref, h0_ref,
         yf_ref, z_ref, dskip_ref, ng_ref, y_ref, hout_ref, st_ref) = refs
    else:
        (x_ref, b_ref, c_ref, dtc_ref, dtr_ref, biasc_ref, biasr_ref, alogc_ref, alogr_ref, h0_ref,
         y_ref, hout_ref, st_ref) = refs
    q = SSD_CHUNK
    nh = dtc_ref.shape[1] // 2
    hpg = nh // SSD_GROUPS
    gw = hpg * SSD_HEAD_DIM
    c = pl.program_id(1)

    @pl.when(c == 0)
    def _():
        for g in range(SSD_GROUPS):
            if zero_init:
                st_ref[g] = jnp.zeros(st_ref.shape[1:], F32)
            else:
                st_ref[g] = h0_ref[0, g].T

    lo = d * nh
    dt_c = _softplus(dtc_ref[:, lo:lo + nh] + biasc_ref[:, lo:lo + nh])
    dt_r = _softplus(dtr_ref[lo:lo + nh, :] + biasr_ref[lo:lo + nh, :])
    a_c = dt_c * (-jnp.exp(alogc_ref[:, lo:lo + nh]))
    a_r = dt_r * (-jnp.exp(alogr_ref[lo:lo + nh, :]))
    ii = lax.broadcasted_iota(jnp.int32, (q, q), 0)
    jj = lax.broadcasted_iota(jnp.int32, (q, q), 1)
    causal = (jj >= ii) if reverse else (jj <= ii)
    tri = causal.astype(F32)
    tri_t = ((ii >= jj) if reverse else (ii <= jj)).astype(F32)
    cum_c = jnp.dot(tri, a_c, precision=HIGHEST, preferred_element_type=F32)
    cum_r = jnp.dot(a_r, tri_t, precision=HIGHEST, preferred_element_type=F32)
    tot_c = jnp.sum(a_c, axis=0, keepdims=True)
    e_in = jnp.exp(cum_c)
    w_out = dt_c * jnp.exp(tot_c - cum_c)
    e_tot = jnp.exp(tot_c)

    x = x_ref[...]
    lane = lax.broadcasted_iota(jnp.int32, (q, LANE), 1)
    lane_row = lax.broadcasted_iota(jnp.int32, (1, LANE), 1)
    y_groups = []
    for g in range(SSD_GROUPS):
        bg = b_ref[:, g * SSD_STATE:(g + 1) * SSD_STATE]
        cg = c_ref[:, g * SSD_STATE:(g + 1) * SSD_STATE]
        cb = lax.dot_general(cg, bg, (((1,), (1,)), ((), ())), preferred_element_type=F32)
        st = st_ref[g]
        y_in = jnp.dot(cg, st.astype(BF16), preferred_element_type=F32)
        y_pairs, xs_pairs, dec_pairs = [], [], []
        for pr in range(hpg // 2):
            h0 = g * hpg + 2 * pr
            x_pair = x[:, h0 * SSD_HEAD_DIM:(h0 + 2) * SSD_HEAD_DIM]
            yd = []
            for h in (h0, h0 + 1):
                seg = cum_c[:, h:h + 1] - cum_r[h:h + 1, :]
                decay = jnp.where(causal, jnp.exp(jnp.minimum(seg, 0.0)), 0.0)
                att = (cb * decay * dt_r[h:h + 1, :]).astype(BF16)
                yd.append(jnp.dot(att, x_pair, preferred_element_type=F32))
            y_pair = jnp.where(lane < SSD_HEAD_DIM, yd[0], yd[1])
            y_pair = y_pair + _pair_cols(e_in, h0) * y_in[:, 2 * pr * SSD_HEAD_DIM:(2 * pr + 2) * SSD_HEAD_DIM]
            y_pairs.append(y_pair)
            xs_pairs.append((x_pair.astype(F32) * _pair_cols(w_out, h0)).astype(BF16))
            dec_pairs.append(jnp.where(lane_row < SSD_HEAD_DIM, e_tot[:, h0:h0 + 1], e_tot[:, h0 + 1:h0 + 2]))
        xs_dec = jnp.concatenate(xs_pairs, axis=1)
        upd = lax.dot_general(bg, xs_dec, (((0,), (0,)), ((), ())), preferred_element_type=F32)
        st_ref[g] = st * jnp.concatenate(dec_pairs, axis=1) + upd
        y_groups.append(jnp.concatenate(y_pairs, axis=1))
    y = jnp.concatenate(y_groups, axis=1)

    if final:
        y = y + yf_ref[...] + dskip_ref[...] * x.astype(F32)
        y = y * _silu(z_ref[...].astype(F32))
        y_ref[...] = _rms(y, ng_ref[...]).astype(y_ref.dtype)
    else:
        y_ref[...] = y

    @pl.when(c == pl.num_programs(1) - 1)
    def _():
        for g in range(SSD_GROUPS):
            hout_ref[0, g] = st_ref[g].T


def _ssd_pass(xbc, dt_col, dt_row, dt_bias, a_log, h0, extra, out_buf, *, d, nseq, seq_len, row0, cols):
    q = SSD_CHUNK
    nc = seq_len // q
    nrows = nseq * seq_len
    nh2 = dt_col.shape[1]
    inner = cols["sz"][1]
    gw = inner // SSD_GROUPS
    bw = SSD_GROUPS * SSD_STATE
    rb0 = row0 // q
    reverse = d == 1
    final = extra is not None
    cidx = (lambda c: nc - 1 - c) if reverse else (lambda c: c)
    loc = lambda s, c: s * nc + cidx(c)
    zero_init = h0 is None
    if zero_init:
        h0 = jnp.zeros((1, SSD_GROUPS, gw, SSD_STATE), F32)
    in_specs = [pl.BlockSpec((q, inner), lambda s, c: (loc(s, c), 0)),
                pl.BlockSpec((q, bw), lambda s, c: (loc(s, c), inner // bw)),
                pl.BlockSpec((q, bw), lambda s, c: (loc(s, c), inner // bw + 1)),
                pl.BlockSpec((q, nh2), lambda s, c: (rb0 + loc(s, c), 0)),
                pl.BlockSpec((nh2, q), lambda s, c: (0, rb0 + loc(s, c))),
                pl.BlockSpec((1, nh2), lambda s, c: (0, 0)),
                pl.BlockSpec((nh2, 1), lambda s, c: (0, 0)),
                pl.BlockSpec((1, nh2), lambda s, c: (0, 0)),
                pl.BlockSpec((nh2, 1), lambda s, c: (0, 0)),
                pl.BlockSpec((1, SSD_GROUPS, gw, SSD_STATE), (lambda s, c: (0, 0, 0, 0)) if zero_init
                             else (lambda s, c: (s, 0, 0, 0)))]
    args = [xbc, xbc, xbc, dt_col, dt_row, dt_bias.reshape(1, -1), dt_bias.reshape(-1, 1),
            a_log.reshape(1, -1), a_log.reshape(-1, 1), h0]
    kern = functools.partial(_ssd_kernel, reverse=reverse, zero_init=zero_init, final=final, d=d)
    aliases = {}
    if final:
        y_fwd, proj, d_skip, norm_g = extra
        in_specs += [pl.BlockSpec((q, inner), lambda s, c: (loc(s, c), 0)),
                     pl.BlockSpec((q, inner), lambda s, c: (rb0 + loc(s, c), cols["sz"][0] // inner)),
                     pl.BlockSpec((1, inner), lambda s, c: (0, 0)),
                     pl.BlockSpec((1, inner), lambda s, c: (0, 0)),
                     pl.BlockSpec(memory_space=pl.ANY)]
        args += [y_fwd, proj, d_skip.reshape(1, -1), norm_g.reshape(1, -1), out_buf]
        aliases = {len(args) - 1: 0}
        kern = functools.partial(_skip_input, kern, len(args) - 1)
        y_shape = jax.ShapeDtypeStruct(out_buf.shape, out_buf.dtype)
        y_spec = pl.BlockSpec((q, inner), lambda s, c: (rb0 + loc(s, c), 0))
    else:
        y_shape = jax.ShapeDtypeStruct((nrows, inner), F32)
        y_spec = pl.BlockSpec((q, inner), lambda s, c: (loc(s, c), 0))
    return pl.pallas_call(
        kern,
        out_shape=(y_shape, jax.ShapeDtypeStruct((nseq, SSD_GROUPS, gw, SSD_STATE), F32)),
        grid=(nseq, nc),
        in_specs=in_specs,
        out_specs=(y_spec, pl.BlockSpec((1, SSD_GROUPS, gw, SSD_STATE), lambda s, c: (s, 0, 0, 0))),
        scratch_shapes=[pltpu.VMEM((SSD_GROUPS, SSD_STATE, gw), F32)],
        input_output_aliases=aliases,
        compiler_params=_cparams(("parallel", "arbitrary")),
    )(*args)


def _s5_kernel(u_ref, wt_ref, ws_ref, wc_ref, a_ref, h0_ref, y_ref, hout_ref, s_scr, hin_scr, *, nseq, nc):
    gb = u_ref.shape[0]
    npair = gb // 2
    w = npair * LANE
    for p in range(npair):
        s = (jnp.dot(u_ref[2 * p], ws_ref[2 * p], preferred_element_type=F32)
             + jnp.dot(u_ref[2 * p + 1], ws_ref[2 * p + 1], preferred_element_type=F32))
        for comp in range(4):
            s_scr[:, comp * w + p * LANE:comp * w + (p + 1) * LANE] = s[:, comp * LANE:(comp + 1) * LANE]

    af_re, af_im, ab_re, ab_im = a_ref[0], a_ref[1], a_ref[2], a_ref[3]

    def one_sequence(sq, _):
        def step(i, carry):
            hf_re, hf_im, hb_re, hb_im = carry
            rf = sq * nc + i
            rb = sq * nc + nc - 1 - i
            hin_scr[pl.ds(rf, 1), 0:w] = hf_re
            hin_scr[pl.ds(rf, 1), w:2 * w] = hf_im
            hin_scr[pl.ds(rb, 1), 2 * w:3 * w] = hb_re
            hin_scr[pl.ds(rb, 1), 3 * w:4 * w] = hb_im
            sf_re = s_scr[pl.ds(rf, 1), 0:w]
            sf_im = s_scr[pl.ds(rf, 1), w:2 * w]
            sb_re = s_scr[pl.ds(rb, 1), 2 * w:3 * w]
            sb_im = s_scr[pl.ds(rb, 1), 3 * w:4 * w]
            return (af_re * hf_re - af_im * hf_im + sf_re,
                    af_re * hf_im + af_im * hf_re + sf_im,
                    ab_re * hb_re - ab_im * hb_im + sb_re,
                    ab_re * hb_im + ab_im * hb_re + sb_im)

        last = lax.fori_loop(0, nc, step, tuple(h0_ref[sq, comp] for comp in range(4)))
        for comp in range(4):
            hout_ref[sq, comp] = last[comp]
        return 0

    lax.fori_loop(0, nseq, one_sequence, 0)

    for p in range(npair):
        hin = jnp.concatenate([hin_scr[:, comp * w + p * LANE:comp * w + (p + 1) * LANE] for comp in range(4)],
                              axis=1).astype(BF16)
        for e in range(2):
            g = 2 * p + e
            y = (jnp.dot(u_ref[g], wt_ref[g], preferred_element_type=F32)
                 + jnp.dot(hin, wc_ref[g], preferred_element_type=F32))
            y_ref[g] = y.astype(y_ref.dtype)


def _s5_scan(u_chunks, w_toep, w_state, w_carry, a_pow, h0, *, nseq, seq_len, row0):
    ng, _, kw = u_chunks.shape
    nc = seq_len // S5_CHUNK
    rows = nseq * nc
    rblk = row0 // rows
    gb = S5_GB
    w = (gb // 2) * LANE
    return pl.pallas_call(
        functools.partial(_s5_kernel, nseq=nseq, nc=nc),
        out_shape=(jax.ShapeDtypeStruct((ng, rows, kw), F32),
                   jax.ShapeDtypeStruct(h0.shape, F32)),
        grid=(ng // gb,),
        in_specs=[pl.BlockSpec((gb, rows, kw), lambda j: (j, rblk, 0)),
                  pl.BlockSpec((gb, kw, kw), lambda j: (j, 0, 0)),
                  pl.BlockSpec((gb, kw, 4 * LANE), lambda j: (j, 0, 0)),
                  pl.BlockSpec((gb, 4 * LANE, kw), lambda j: (j, 0, 0)),
                  pl.BlockSpec((4, 1, w), lambda j: (0, 0, j)),
                  pl.BlockSpec((nseq, 4, 1, w), lambda j: (0, 0, 0, j))],
        out_specs=(pl.BlockSpec((gb, rows, kw), lambda j: (j, 0, 0)),
                   pl.BlockSpec((nseq, 4, 1, w), lambda j: (0, 0, 0, j))),
        scratch_shapes=[pltpu.VMEM((rows, 4 * w), F32), pltpu.VMEM((rows, 4 * w), F32)],
        compiler_params=_cparams(("parallel",)),
    )(u_chunks, w_toep, w_state, w_carry, a_pow, h0)


def _s5_weights(lam_re, lam_im, log_step, b_re, b_im, c_re, c_im):
    t = S5_CHUNK
    ng, ns, nh = b_re.shape
    step = jnp.exp(log_step)[..., None]
    lr, li = lam_re * step, lam_im * step

    def a_pow_fn(n):
        mag = jnp.exp(lr * n)
        return mag * jnp.cos(li * n), mag * jnp.sin(li * n)

    a_re, a_im = a_pow_fn(1.0)
    den = lam_re * lam_re + lam_im * lam_im
    k_re = ((a_re - 1.0) * lam_re + a_im * lam_im) / den
    k_im = (a_im * lam_re - (a_re - 1.0) * lam_im) / den
    w_re = k_re[..., None] * b_re - k_im[..., None] * b_im
    w_im = k_re[..., None] * b_im + k_im[..., None] * b_re
    taus = jnp.arange(t + 1, dtype=F32)
    pw_re = jnp.stack([a_pow_fn(n)[0] for n in taus], axis=0)
    pw_im = jnp.stack([a_pow_fn(n)[1] for n in taus], axis=0)
    aw_re = pw_re[:t, :, :, :, None] * w_re - pw_im[:t, :, :, :, None] * w_im
    aw_im = pw_re[:t, :, :, :, None] * w_im + pw_im[:t, :, :, :, None] * w_re
    hi_prec = dict(precision=HIGHEST)
    kern = (jnp.einsum("gop,tdgpi->tdgoi", c_re, aw_re, **hi_prec)
            - jnp.einsum("gop,tdgpi->tdgoi", c_im, aw_im, **hi_prec))
    ti = jnp.arange(t)
    lag = ti[None, :] - ti[:, None]
    kf = jnp.where((lag >= 0)[..., None, None, None], kern[jnp.clip(lag, 0, t - 1), 0], 0.0)
    kb = jnp.where((lag <= 0)[..., None, None, None], kern[jnp.clip(-lag, 0, t - 1), 1], 0.0)
    toep = (kf + kb).transpose(2, 0, 4, 1, 3).reshape(ng, t * nh, t * nh)
    idx_f = (t - 1 - ti)
    sf_re = pw_re[idx_f, 0][..., None] * w_re[0] - pw_im[idx_f, 0][..., None] * w_im[0]
    sf_im = pw_re[idx_f, 0][..., None] * w_im[0] + pw_im[idx_f, 0][..., None] * w_re[0]
    sb_re = pw_re[ti, 1][..., None] * w_re[1] - pw_im[ti, 1][..., None] * w_im[1]
    sb_im = pw_re[ti, 1][..., None] * w_im[1] + pw_im[ti, 1][..., None] * w_re[1]
    comps = jnp.stack([sf_re, sf_im, sb_re, sb_im], axis=0)
    comps = comps.transpose(2, 1, 4, 0, 3).reshape(ng, t * nh, 4, ns)
    even = (jnp.arange(ng) % 2 == 0)[:, None, None, None]
    w_state = jnp.concatenate([jnp.where(even, comps, 0.0), jnp.where(even, 0.0, comps)], axis=-1)
    w_state = w_state.reshape(ng, t * nh, 4 * 2 * ns)
    cf_re = c_re[None, :, :, :] * pw_re[ti + 1, 0][:, :, None, :] - c_im[None] * pw_im[ti + 1, 0][:, :, None, :]
    cf_im = c_re[None] * pw_im[ti + 1, 0][:, :, None, :] + c_im[None] * pw_re[ti + 1, 0][:, :, None, :]
    cb_re = c_re[None] * pw_re[t - ti, 1][:, :, None, :] - c_im[None] * pw_im[t - ti, 1][:, :, None, :]
    cb_im = c_re[None] * pw_im[t - ti, 1][:, :, None, :] + c_im[None] * pw_re[t - ti, 1][:, :, None, :]
    carry = jnp.stack([cf_re, -cf_im, cb_re, -cb_im], axis=0)
    carry = carry.transpose(2, 0, 4, 1, 3).reshape(ng, 4, ns, t * nh)
    w_carry = jnp.concatenate([jnp.where(even, carry, 0.0), jnp.where(even, 0.0, carry)], axis=2)
    w_carry = w_carry.reshape(ng, 4 * 2 * ns, t * nh)
    a_pow = jnp.stack([pw_re[t, 0], pw_im[t, 0], pw_re[t, 1], pw_im[t, 1]], axis=0).reshape(4, 1, ng * ns)
    return toep.astype(BF16), w_state.astype(BF16), w_carry.astype(BF16), a_pow


def _glu_kernel(y_ref, u_ref, d_ref, wv_ref, wg_ref, o_ref, vs_ref):
    @pl.when(pl.program_id(1) == 0)
    def _():
        v = y_ref[...] + d_ref[...] * u_ref[...].astype(F32)
        v = 0.5 * v * (1.0 + jnp.tanh(math.sqrt(2.0 / math.pi) * (v + 0.044715 * (v * v * v))))
        vs_ref[...] = v.astype(BF16)

    v = vs_ref[...]
    a = jnp.dot(v, wv_ref[...], preferred_element_type=F32)
    b = jnp.dot(v, wg_ref[...], preferred_element_type=F32)
    o_ref[...] = (a * jax.nn.sigmoid(b)).astype(o_ref.dtype)


def _s5_glu(y, proj, d_skip, w_glu, dims, cols):
    m, wd = y.shape
    tm, tn = min(TM, dims["lat_len"]), 512
    nj = wd // tn
    return pl.pallas_call(
        _glu_kernel,
        out_shape=jax.ShapeDtypeStruct((m, wd), BF16),
        grid=(m // tm, nj),
        in_specs=[pl.BlockSpec((tm, wd), lambda i, j: (i, 0)),
                  pl.BlockSpec((tm, wd), lambda i, j: (i, cols["s5u"][0] // wd)),
                  pl.BlockSpec((1, wd), lambda i, j: (0, 0)),
                  pl.BlockSpec((wd, tn), lambda i, j: (0, j)),
                  pl.BlockSpec((wd, tn), lambda i, j: (0, j + nj))],
        out_specs=pl.BlockSpec((tm, tn), lambda i, j: (i, j)),
        scratch_shapes=[pltpu.VMEM((tm, wd), BF16)],
        compiler_params=_cparams(("parallel", "arbitrary")),
    )(y, proj, d_skip.reshape(1, -1), w_glu, w_glu)


def _columns(d_model, branch_w):
    kvw = GQA_KV_HEADS * GQA_HEAD_DIM
    xbc = branch_w + 2 * SSD_GROUPS * SSD_STATE
    order = (("gate", N_BRANCH * d_model), ("gq", branch_w), ("sz", branch_w), ("s5u", branch_w),
             ("sxbc", xbc), ("mqd", MLA_Q_LORA), ("ckv", MLA_KV_LORA), ("gk", kvw), ("gv", kvw))
    cols, off = {}, 0
    for name, width in order:
        cols[name] = (off, width)
        off += width
    return cols, off


def _prep_w_in(w_in, d_model, branch_w, n_dt):
    kvw = GQA_KV_HEADS * GQA_HEAD_DIM
    xbc = branch_w + 2 * SSD_GROUPS * SSD_STATE
    splits = (N_BRANCH * d_model, branch_w, kvw, kvw, branch_w, xbc, n_dt, MLA_Q_LORA, MLA_KV_LORA + MLA_ROPE, branch_w)
    bounds, acc = [], 0
    for wd in splits[:-1]:
        acc += wd
        bounds.append(acc)
    gate, gq, gk, gv, sz, sxbc, sdt, mqd, mkvd, s5u = jnp.split(w_in, bounds, axis=-1)
    ckv, kpe = mkvd[..., :MLA_KV_LORA], mkvd[..., MLA_KV_LORA:]
    main = jnp.concatenate([gate, gq, sz, s5u, sxbc, mqd, ckv, gk, gv], axis=-1).astype(BF16)
    pad = jnp.zeros(w_in.shape[:-1] + (LANE - MLA_ROPE - n_dt,), w_in.dtype)
    small = jnp.concatenate([kpe, sdt, pad], axis=-1).astype(BF16)
    return main, small


def _prep_mla(w_uq, w_ukv):
    depth = w_uq.shape[0]
    qk = MLA_NOPE + MLA_ROPE
    wq = w_uq.reshape(depth, MLA_Q_LORA, MLA_HEADS, qk)
    wq = jnp.pad(wq, ((0, 0), (0, 0), (0, 0), (0, MLA_QK - qk))).reshape(depth, MLA_Q_LORA, MLA_HEADS * MLA_QK)
    wkv = w_ukv.reshape(depth, MLA_KV_LORA, MLA_HEADS, MLA_NOPE + LANE)
    k_nope, v = wkv[..., :MLA_NOPE], wkv[..., MLA_NOPE:]
    k_top = jnp.pad(k_nope, ((0, 0), (0, 0), (0, 0), (0, MLA_QK - MLA_NOPE)))
    eye = jnp.eye(LANE, MLA_QK, k=MLA_NOPE, dtype=w_ukv.dtype) * (jnp.arange(LANE) < MLA_ROPE)[:, None]
    k_bot = jnp.broadcast_to(eye[None, :, None, :], (depth, LANE, MLA_HEADS, MLA_QK))
    k_all = jnp.concatenate([k_top, k_bot], axis=1).reshape(depth, MLA_KV_IN, MLA_HEADS * MLA_QK)
    v_all = jnp.pad(v, ((0, 0), (0, LANE), (0, 0), (0, 0))).reshape(depth, MLA_KV_IN, MLA_HEADS * LANE)
    return wq.astype(BF16), jnp.concatenate([k_all, v_all], axis=-1).astype(BF16)


def kernel(x_prompt, x_sample, cache_gqa_k, cache_gqa_v, cache_mla_ckv, cache_mla_kpe, state_ssd, state_s5, c, c_ctx, norm1_g, norm2_g, w_mod, b_mod, w_in, gqa_qn_g, gqa_kn_g, ssd_conv_w, ssd_conv_b, ssd_a_log, ssd_dt_bias, ssd_d, ssd_norm_g, mla_qn_g, mla_w_uq, mla_kvn_g, mla_w_ukv, s5_lam_re, s5_lam_im, s5_log_step, s5_b_re, s5_b_im, s5_c_re, s5_c_im, s5_d, s5_w_glu, w_branch, w_out, w_ffn_in, w_ffn_out, final_g):
    nb_ctx, len_ctx, d_model = x_prompt.shape
    nb_lat, len_lat, _ = x_sample.shape
    depth = w_in.shape[0]
    past = cache_gqa_k.shape[2]
    branch_w = w_branch.shape[2]
    n_heads = ssd_d.shape[1]
    ctx_rows, lat_rows = nb_ctx * len_ctx, nb_lat * len_lat
    m = ctx_rows + lat_rows
    dims = {"ctx_rows": ctx_rows, "lat_len": len_lat}
    cols, _ = _columns(d_model, branch_w)
    kvw = GQA_KV_HEADS * GQA_HEAD_DIM
    n_s5 = branch_w // S5_GROUP_CH

    w_main, w_small = _prep_w_in(w_in, d_model, branch_w, 2 * n_heads)
    w_q, w_kv = _prep_mla(mla_w_uq, mla_w_ukv)
    s5_w = jax.vmap(_s5_weights)(s5_lam_re, s5_lam_im, s5_log_step, s5_b_re, s5_b_im, s5_c_re, s5_c_im)
    tabs_a = _rope_tables(len_lat, GQA_HEAD_DIM)
    tabs_c = _rope_tables(len_lat, MLA_ROPE)
    n_mod = -(-(nb_lat + 1) // 8) * 8
    cvec = jnp.concatenate([c_ctx[None], c, jnp.zeros((n_mod - nb_lat - 1, d_model), F32)], axis=0)
    mod_all = _modulation(cvec, w_mod, b_mod).reshape(depth, n_mod, 6, 1, d_model)
    zpad = jnp.zeros(cache_mla_kpe.shape[:-1] + (LANE - MLA_ROPE,), F32)
    layer_in = dict(
        mod=mod_all, norm1_g=norm1_g, norm2_g=norm2_g, w_main=w_main, w_small=w_small,
        gqa_qn_g=gqa_qn_g, gqa_kn_g=gqa_kn_g, conv_w=ssd_conv_w, conv_b=ssd_conv_b, a_log=ssd_a_log,
        dt_bias=ssd_dt_bias, d_skip=jnp.repeat(ssd_d, SSD_HEAD_DIM, axis=-1), ssd_norm_g=ssd_norm_g,
        mla_qn_g=mla_qn_g, mla_kvn_g=mla_kvn_g, w_q=w_q, w_kv=w_kv,
        s5_toep=s5_w[0], s5_state=s5_w[1], s5_carry=s5_w[2], s5_apow=s5_w[3], s5_d=s5_d,
        w_glu=s5_w_glu.astype(BF16), w_branch=w_branch.astype(BF16), w_out=w_out.astype(BF16),
        w_ffn_in=w_ffn_in.astype(BF16), w_ffn_out=w_ffn_out.astype(BF16),
        cache_k=jnp.moveaxis(cache_gqa_k, 1, 0).reshape(depth, nb_lat, past, kvw).astype(BF16),
        cache_v=jnp.moveaxis(cache_gqa_v, 1, 0).reshape(depth, nb_lat, past, kvw).astype(BF16),
        cache_kv=jnp.moveaxis(jnp.concatenate([cache_mla_ckv, cache_mla_kpe, zpad], axis=-1), 1, 0).astype(BF16),
        h0_ssd=jnp.moveaxis(state_ssd, (1, 2), (0, 1)).reshape(
            depth, 2, nb_lat, SSD_GROUPS, branch_w // SSD_GROUPS, SSD_STATE),
        h0_s5=jnp.moveaxis(state_s5, 1, 0).reshape(depth, nb_lat, 4, 1, n_s5 * S5_STATE),
    )

    def layer(x, lp):
        mod = lp["mod"]
        proj, small = _in_proj(x, lp["norm1_g"], mod, lp["w_main"], lp["w_small"], dims)
        new_buf = lambda: jnp.zeros((m, branch_w), BF16)

        qn, kn, k_own = _gqa_prep(proj, lp["gqa_qn_g"], lp["gqa_kn_g"], None,
                                  row0=0, nrows=ctx_rows, seq_len=len_ctx, cols=cols)
        gqa = dict(n_kv=GQA_KV_HEADS, n_rep=cols["gq"][1] // kvw, dk=GQA_HEAD_DIM, dv=GQA_HEAD_DIM)
        o_a = _attention(qn, kn.reshape(nb_ctx, len_ctx, kvw), proj.reshape(-1, len_ctx, proj.shape[1]), new_buf(),
                         nseq=nb_ctx, seq_len=len_ctx, tq=len_ctx, k_col0=0, v_col0=cols["gv"][0] // GQA_HEAD_DIM,
                         row0=0, **gqa)
        v_own = proj[:ctx_rows, cols["gv"][0]:cols["gv"][0] + kvw]
        qn, kn = _gqa_prep(proj, lp["gqa_qn_g"], lp["gqa_kn_g"], tabs_a,
                           row0=ctx_rows, nrows=lat_rows, seq_len=len_lat, cols=cols)
        k_lat = jnp.concatenate([kn.reshape(nb_lat, len_lat, kvw), lp["cache_k"]], axis=1)
        v_lat = jnp.concatenate([proj[ctx_rows:, cols["gv"][0]:cols["gv"][0] + kvw].reshape(nb_lat, len_lat, kvw),
                                 lp["cache_v"]], axis=1)
        o_a = _attention(qn, k_lat, v_lat, o_a, nseq=nb_lat, seq_len=len_lat, tq=min(TQ_LAT, len_lat),
                         k_col0=0, v_col0=0, row0=ctx_rows, **gqa)

        mla = dict(n_kv=MLA_HEADS, n_rep=1, dk=MLA_QK, dv=LANE, k_col0=0, v_col0=MLA_HEADS * MLA_QK // LANE)
        qdn, kv_in, ckv_own = _mla_prep(proj, small, lp["mla_qn_g"], lp["mla_kvn_g"], None,
                                        row0=0, nrows=ctx_rows, seq_len=len_ctx, cols=cols)
        kpe_own = small[:ctx_rows, :MLA_ROPE]
        kv = _mla_kv(kv_in, lp["w_kv"]).reshape(nb_ctx, len_ctx, -1)
        o_c = _attention(_mla_q(qdn, lp["w_q"], None, seq_len=len_ctx), kv, kv, new_buf(),
                         nseq=nb_ctx, seq_len=len_ctx, tq=len_ctx, row0=0, **mla)
        qdn, kv_in = _mla_prep(proj, small, lp["mla_qn_g"], lp["mla_kvn_g"], tabs_c,
                               row0=ctx_rows, nrows=lat_rows, seq_len=len_lat, cols=cols)
        kv_in = jnp.concatenate([kv_in.reshape(nb_lat, len_lat, MLA_KV_IN), lp["cache_kv"]], axis=1)
        kv = _mla_kv(kv_in.reshape(-1, MLA_KV_IN), lp["w_kv"]).reshape(nb_lat, len_lat + past, -1)
        o_c = _attention(_mla_q(qdn, lp["w_q"], tabs_c, seq_len=len_lat), kv, kv, o_c,
                         nseq=nb_lat, seq_len=len_lat, tq=min(TQ_LAT, len_lat), row0=ctx_rows, **mla)

        dt_col = small[:, MLA_ROPE:MLA_ROPE + 2 * n_heads]
        dt_row = dt_col.T
        o_b = new_buf()
        ssd_new = None
        for nseq, seq_len, row0, h0 in ((nb_ctx, len_ctx, 0, None), (nb_lat, len_lat, ctx_rows, lp["h0_ssd"])):
            xbc = _ssd_conv(proj, lp["conv_w"], lp["conv_b"], seq0=row0 // seq_len, nseq=nseq, seq_len=seq_len,
                            cols=cols)
            common = dict(nseq=nseq, seq_len=seq_len, row0=row0, cols=cols)
            y_f, h_f = _ssd_pass(xbc, dt_col, dt_row, lp["dt_bias"], lp["a_log"], None if h0 is None else h0[0],
                                 None, None, d=0, **common)
            o_b, h_b = _ssd_pass(xbc, dt_col, dt_row, lp["dt_bias"], lp["a_log"], None if h0 is None else h0[1],
                                 (y_f, proj, lp["d_skip"], lp["ssd_norm_g"]), o_b, d=1, **common)
            if h0 is None:
                ssd_new = jnp.stack([h_f, h_b], axis=1).reshape(nb_ctx, 2, n_heads, SSD_HEAD_DIM, SSD_STATE)

        u = proj[:, cols["s5u"][0]:cols["s5u"][0] + branch_w]
        u_chunks = (u.reshape(m // S5_CHUNK, S5_CHUNK, n_s5, S5_GROUP_CH).transpose(2, 0, 1, 3)
                    .reshape(n_s5, m // S5_CHUNK, S5_CHUNK * S5_GROUP_CH))
        s5_args = (u_chunks, lp["s5_toep"], lp["s5_state"], lp["s5_carry"], lp["s5_apow"])
        y_ctx, s5_new = _s5_scan(*s5_args, jnp.zeros((nb_ctx, 4, 1, n_s5 * S5_STATE), F32),
                                 nseq=nb_ctx, seq_len=len_ctx, row0=0)
        y_lat, _ = _s5_scan(*s5_args, lp["h0_s5"], nseq=nb_lat, seq_len=len_lat, row0=ctx_rows // S5_CHUNK)
        y = jnp.concatenate([y_ctx, y_lat], axis=1)
        y = (y.reshape(n_s5, m // S5_CHUNK, S5_CHUNK, S5_GROUP_CH).transpose(1, 2, 0, 3).reshape(m, branch_w))
        o_d = _s5_glu(y, proj, lp["s5_d"], lp["w_glu"], dims, cols)

        mixed = _branch_mix([o_a, o_b, o_c, o_d], proj, lp["w_branch"], cols["gate"][0], dims)
        x = _resid_proj(mixed, lp["w_out"], x, mod, 2, TM, dims)
        hidden = _ffn_in(x, lp["norm2_g"], mod, lp["w_ffn_in"], dims)
        x = _resid_proj(hidden, lp["w_ffn_out"], x, mod, 5, TM_FFN_OUT, dims)
        new = (k_own.reshape(nb_ctx, len_ctx, GQA_KV_HEADS, GQA_HEAD_DIM),
               v_own.astype(F32).reshape(nb_ctx, len_ctx, GQA_KV_HEADS, GQA_HEAD_DIM),
               ckv_own.reshape(nb_ctx, len_ctx, MLA_KV_LORA),
               kpe_own.reshape(nb_ctx, len_ctx, MLA_ROPE),
               ssd_new,
               s5_new.reshape(nb_ctx, 2, 2, n_s5, S5_STATE))
        return x, new

    x0 = jnp.concatenate([x_prompt.reshape(ctx_rows, d_model), x_sample.reshape(lat_rows, d_model)], axis=0)
    x_fin, new = lax.scan(layer, x0, layer_in)
    y = _final_norm(x_fin, final_g)
    return (y[:ctx_rows].reshape(x_prompt.shape), y[ctx_rows:].reshape(x_sample.shape),
            *(jnp.moveaxis(t, 0, 1) for t in new))
```

```python
import functools
import math

import jax
import jax.numpy as jnp
from jax import lax
from jax.experimental import pallas as pl
from jax.experimental.pallas import tpu as pltpu

F32 = jnp.float32
BF16 = jnp.bfloat16
HIGHEST = lax.Precision.HIGHEST

GRID_W = 64
N_BRANCH = 4
GQA_HEAD_DIM = 128
GQA_KV_HEADS = 2
SSD_HEAD_DIM = 64
SSD_GROUPS = 2
SSD_STATE = 128
SSD_CONV = 5
SSD_CHUNK = 128
MLA_HEADS = 8
MLA_NOPE = 128
MLA_ROPE = 64
MLA_Q_LORA = 512
MLA_KV_LORA = 256
S5_GROUP_CH = 16
S5_STATE = 64
S5_CHUNK = 16
ROPE_THETA = 10000.0
NORM_EPS = 1e-6
LANE = 128
VMEM_LIMIT = 56 * 1024 * 1024

TM = 1024
TN_IN = 1280
TN_FFN = 512
TM_FFN_OUT = 512
TN_OUT = 512
TR_ROWS = 1024
TQ_LAT = 512
MLA_LAT_HEADS_PER_STEP = 4
S5_GB = 8


def _cparams(sem):
    return pltpu.CompilerParams(dimension_semantics=sem, vmem_limit_bytes=VMEM_LIMIT)


def _row_group(i, tm, ctx_rows, lat_len):
    nct = ctx_rows // tm
    per = lat_len // tm
    return jnp.where(i < nct, 0, 1 + (i - nct) // per)


def _silu(x):
    return x * jax.nn.sigmoid(x)


def _softplus(x):
    return jnp.maximum(x, 0.0) + jnp.log(1.0 + jnp.exp(-jnp.abs(x)))


def _rms(x, g):
    ms = jnp.mean(x * x, axis=-1, keepdims=True)
    return x * lax.rsqrt(ms + NORM_EPS) * g


def _mod_kernel(c_ref, w_ref, b_ref, o_ref):
    c = c_ref[...]
    s = _silu(c).astype(BF16)
    o_ref[0] = jnp.dot(s, w_ref[0].astype(BF16), preferred_element_type=F32) + b_ref[0]


def _modulation(cvec, w_mod, b_mod):
    depth, d, n = w_mod.shape
    tn = 1024
    return pl.pallas_call(
        _mod_kernel,
        out_shape=jax.ShapeDtypeStruct((depth, 8, n), F32),
        grid=(depth, n // tn),
        in_specs=[pl.BlockSpec((8, d), lambda l, j: (0, 0)),
                  pl.BlockSpec((1, d, tn), lambda l, j: (l, 0, j)),
                  pl.BlockSpec((1, 1, tn), lambda l, j: (l, 0, j))],
        out_specs=pl.BlockSpec((1, 8, tn), lambda l, j: (l, 0, j)),
        compiler_params=_cparams(("parallel", "parallel")),
    )(cvec, w_mod, b_mod.reshape(depth, 1, n))


def _norm_mod(x_ref, g_ref, sc_ref, sh_ref):
    y = _rms(x_ref[...], g_ref[...])
    return (y * (1.0 + sc_ref[...]) + sh_ref[...]).astype(BF16)


def _in_proj_kernel(x_ref, g_ref, sc_ref, sh_ref, w_ref, ws_ref, o_ref, os_ref, hs_ref):
    @pl.when(pl.program_id(1) == 0)
    def _():
        h = _norm_mod(x_ref, g_ref, sc_ref, sh_ref)
        hs_ref[...] = h
        os_ref[...] = jnp.dot(h, ws_ref[...], preferred_element_type=F32)

    o_ref[...] = jnp.dot(hs_ref[...], w_ref[...], preferred_element_type=F32).astype(o_ref.dtype)


def _in_proj(x, norm_g, mod, w_main, w_small, dims):
    m, d = x.shape
    npad = w_main.shape[1]
    tm, tn = min(TM, dims["lat_len"]), TN_IN
    grp = lambda i: _row_group(i, tm, dims["ctx_rows"], dims["lat_len"])
    return pl.pallas_call(
        _in_proj_kernel,
        out_shape=(jax.ShapeDtypeStruct((m, npad), BF16), jax.ShapeDtypeStruct((m, LANE), F32)),
        grid=(m // tm, npad // tn),
        in_specs=[pl.BlockSpec((tm, d), lambda i, j: (i, 0)),
                  pl.BlockSpec((1, d), lambda i, j: (0, 0)),
                  pl.BlockSpec((None, None, 1, d), lambda i, j: (grp(i), 1, 0, 0)),
                  pl.BlockSpec((None, None, 1, d), lambda i, j: (grp(i), 0, 0, 0)),
                  pl.BlockSpec((d, tn), lambda i, j: (0, j)),
                  pl.BlockSpec((d, LANE), lambda i, j: (0, 0))],
        out_specs=(pl.BlockSpec((tm, tn), lambda i, j: (i, j)),
                   pl.BlockSpec((tm, LANE), lambda i, j: (i, 0))),
        scratch_shapes=[pltpu.VMEM((tm, d), BF16)],
        compiler_params=_cparams(("parallel", "arbitrary")),
    )(x, norm_g.reshape(1, d), mod, mod, w_main, w_small)


def _ffn_in_kernel(x_ref, g_ref, sc_ref, sh_ref, wg_ref, wu_ref, o_ref, hs_ref):
    @pl.when(pl.program_id(1) == 0)
    def _():
        hs_ref[...] = _norm_mod(x_ref, g_ref, sc_ref, sh_ref)

    h = hs_ref[...]
    a = jnp.dot(h, wg_ref[...], preferred_element_type=F32)
    b = jnp.dot(h, wu_ref[...], preferred_element_type=F32)
    o_ref[...] = (_silu(a) * b).astype(o_ref.dtype)


def _ffn_in(x, norm_g, mod, w, dims):
    m, d = x.shape
    hid = w.shape[1] // 2
    tm, tn = min(TM, dims["lat_len"]), TN_FFN
    nj = hid // tn
    grp = lambda i: _row_group(i, tm, dims["ctx_rows"], dims["lat_len"])
    return pl.pallas_call(
        _ffn_in_kernel,
        out_shape=jax.ShapeDtypeStruct((m, hid), BF16),
        grid=(m // tm, nj),
        in_specs=[pl.BlockSpec((tm, d), lambda i, j: (i, 0)),
                  pl.BlockSpec((1, d), lambda i, j: (0, 0)),
                  pl.BlockSpec((None, None, 1, d), lambda i, j: (grp(i), 4, 0, 0)),
                  pl.BlockSpec((None, None, 1, d), lambda i, j: (grp(i), 3, 0, 0)),
                  pl.BlockSpec((d, tn), lambda i, j: (0, j)),
                  pl.BlockSpec((d, tn), lambda i, j: (0, j + nj))],
        out_specs=pl.BlockSpec((tm, tn), lambda i, j: (i, j)),
        scratch_shapes=[pltpu.VMEM((tm, d), BF16)],
        compiler_params=_cparams(("parallel", "arbitrary")),
    )(x, norm_g.reshape(1, d), mod, mod, w, w)


def _resid_kernel(x_ref, w_ref, r_ref, g_ref, o_ref):
    o_ref[...] = r_ref[...] + g_ref[...] * jnp.dot(x_ref[...], w_ref[...], preferred_element_type=F32)


def _resid_proj(xin, w, resid, mod, mod_idx, tm, dims):
    m, k = xin.shape
    n = w.shape[1]
    tm, tn = min(tm, dims["lat_len"]), TN_OUT
    grp = lambda i: _row_group(i, tm, dims["ctx_rows"], dims["lat_len"])
    return pl.pallas_call(
        _resid_kernel,
        out_shape=jax.ShapeDtypeStruct((m, n), F32),
        grid=(m // tm, n // tn),
        in_specs=[pl.BlockSpec((tm, k), lambda i, j: (i, 0)),
                  pl.BlockSpec((k, tn), lambda i, j: (0, j)),
                  pl.BlockSpec((tm, tn), lambda i, j: (i, j)),
                  pl.BlockSpec((None, None, 1, tn), lambda i, j: (grp(i), mod_idx, 0, j))],
        out_specs=pl.BlockSpec((tm, tn), lambda i, j: (i, j)),
        compiler_params=_cparams(("parallel", "arbitrary")),
    )(xin, w, resid, mod)


def _mix_kernel(oa_ref, ob_ref, oc_ref, od_ref, ga_ref, gb_ref, gc_ref, gd_ref, w_ref, o_ref):
    acc = None
    for n, (o_n, g_n) in enumerate(((oa_ref, ga_ref), (ob_ref, gb_ref), (oc_ref, gc_ref), (od_ref, gd_ref))):
        proj = jnp.dot(o_n[...], w_ref[n], preferred_element_type=F32)
        term = jax.nn.sigmoid(g_n[...].astype(F32)) * proj
        acc = term if acc is None else acc + term
    o_ref[...] = acc.astype(o_ref.dtype)


def _branch_mix(branches, proj_all, w_branch, col_gate, dims):
    m, bw = branches[0].shape
    d = w_branch.shape[2]
    tm, tn = min(TM, dims["lat_len"]), TN_OUT
    gate_specs = [pl.BlockSpec((tm, tn), functools.partial(lambda i, j, n: (i, (col_gate + n * d) // tn + j), n=n))
                  for n in range(N_BRANCH)]
    return pl.pallas_call(
        _mix_kernel,
        out_shape=jax.ShapeDtypeStruct((m, d), BF16),
        grid=(m // tm, d // tn),
        in_specs=[pl.BlockSpec((tm, bw), lambda i, j: (i, 0))] * N_BRANCH + gate_specs
                 + [pl.BlockSpec((N_BRANCH, bw, tn), lambda i, j: (0, 0, j))],
        out_specs=pl.BlockSpec((tm, tn), lambda i, j: (i, j)),
        compiler_params=_cparams(("parallel", "arbitrary")),
    )(*branches, proj_all, proj_all, proj_all, proj_all, w_branch)


def _final_norm_kernel(x_ref, g_ref, o_ref):
    o_ref[...] = _rms(x_ref[...], g_ref[...])


def _final_norm(x, g):
    m, d = x.shape
    tm = 512
    return pl.pallas_call(
        _final_norm_kernel,
        out_shape=jax.ShapeDtypeStruct((m, d), F32),
        grid=(m // tm,),
        in_specs=[pl.BlockSpec((tm, d), lambda i: (i, 0)), pl.BlockSpec((1, d), lambda i: (0, 0))],
        out_specs=pl.BlockSpec((tm, d), lambda i: (i, 0)),
        compiler_params=_cparams(("parallel",)),
    )(x, g.reshape(1, d))


def _rope_tables(seq_len, dim):
    nf = dim // 4
    t = jnp.arange(seq_len)
    row = (t // GRID_W).astype(F32)
    col = (t % GRID_W).astype(F32)
    inv = ROPE_THETA ** (-jnp.arange(nf, dtype=F32) / nf)
    ang_r, ang_c = row[:, None] * inv, col[:, None] * inv
    cos = jnp.concatenate([jnp.cos(ang_r), jnp.cos(ang_r), jnp.cos(ang_c), jnp.cos(ang_c)], axis=1)
    sin = jnp.concatenate([-jnp.sin(ang_r), jnp.sin(ang_r), -jnp.sin(ang_c), jnp.sin(ang_c)], axis=1)
    pad = LANE - dim
    if pad:
        cos = jnp.concatenate([cos, jnp.ones((seq_len, pad), F32)], axis=1)
        sin = jnp.concatenate([sin, jnp.zeros((seq_len, pad), F32)], axis=1)
    return cos, sin


def _rope(x, cos, sin, nf):
    lane = lax.broadcasted_iota(jnp.int32, x.shape, 1)
    swapped = jnp.where((lane % (2 * nf)) < nf, pltpu.roll(x, LANE - nf, 1), pltpu.roll(x, nf, 1))
    return x * cos + swapped * sin


def _attn_kernel(q_ref, k_ref, v_ref, o_ref, *, kv_per_step, n_rep, dk, dv):
    for a in range(kv_per_step):
        k = k_ref[0, :, a * dk:(a + 1) * dk]
        v = v_ref[0, :, a * dv:(a + 1) * dv]
        for r in range(n_rep):
            h = a * n_rep + r
            q = q_ref[:, h * dk:(h + 1) * dk]
            s = lax.dot_general(q, k, (((1,), (1,)), ((), ())), preferred_element_type=F32)
            p = jnp.exp(s - jnp.max(s, axis=-1, keepdims=True))
            l = jnp.sum(p, axis=-1, keepdims=True)
            o = jnp.dot(p.astype(BF16), v, preferred_element_type=F32)
            o_ref[:, h * dv:(h + 1) * dv] = (o / l).astype(o_ref.dtype)


def _skip_input(kern, pos, *refs):
    return kern(*refs[:pos], *refs[pos + 1:])


def _attention(q, k, v, out_buf, *, nseq, seq_len, tq, n_kv, kv_per_step, n_rep, dk, dv, k_off, v_off, row0):
    lk = k.shape[1]
    nq = seq_len // tq
    rb0 = row0 // tq
    kw, vw = kv_per_step * dk, kv_per_step * dv
    assert k_off % kw == 0 and v_off % vw == 0 and n_kv % kv_per_step == 0
    kb0, vb0 = k_off // kw, v_off // vw
    kern = functools.partial(_attn_kernel, kv_per_step=kv_per_step, n_rep=n_rep, dk=dk, dv=dv)
    return pl.pallas_call(
        functools.partial(_skip_input, kern, 3),
        out_shape=jax.ShapeDtypeStruct(out_buf.shape, out_buf.dtype),
        grid=(nseq, n_kv // kv_per_step, nq),
        in_specs=[pl.BlockSpec((tq, n_rep * kw), lambda b, g, i: (b * nq + i, g)),
                  pl.BlockSpec((1, lk, kw), lambda b, g, i: (b, 0, kb0 + g)),
                  pl.BlockSpec((1, lk, vw), lambda b, g, i: (b, 0, vb0 + g)),
                  pl.BlockSpec(memory_space=pl.ANY)],
        out_specs=pl.BlockSpec((tq, n_rep * vw), lambda b, g, i: (rb0 + b * nq + i, g)),
        input_output_aliases={3: 0},
        compiler_params=_cparams(("parallel", "parallel", "arbitrary")),
    )(q, k, v, out_buf)


def _gqa_prep_kernel(*refs, rope, scale, n_q, n_kv):
    if rope:
        q_ref, k_ref, qg_ref, kg_ref, cos_ref, sin_ref, qo_ref, ko_ref = refs
    else:
        q_ref, k_ref, qg_ref, kg_ref, qo_ref, ko_ref, k32_ref = refs
    hd = GQA_HEAD_DIM
    for h in range(n_q):
        y = _rms(q_ref[:, h * hd:(h + 1) * hd].astype(F32), qg_ref[...])
        if rope:
            y = _rope(y, cos_ref[...], sin_ref[...], hd // 4)
        qo_ref[:, h * hd:(h + 1) * hd] = (y * scale).astype(qo_ref.dtype)
    for h in range(n_kv):
        y = _rms(k_ref[:, h * hd:(h + 1) * hd].astype(F32), kg_ref[...])
        if rope:
            y = _rope(y, cos_ref[...], sin_ref[...], hd // 4)
        else:
            k32_ref[:, h * hd:(h + 1) * hd] = y
        ko_ref[:, h * hd:(h + 1) * hd] = y.astype(ko_ref.dtype)


def _gqa_prep(proj, qg, kg, tabs, *, row0, nrows, seq_len, cols):
    tr = TR_ROWS if tabs is None else min(TR_ROWS, seq_len)
    rb0 = row0 // tr
    qw, kw = cols["gq"][1], cols["gk"][1]
    n_q, n_kv = qw // GQA_HEAD_DIM, kw // GQA_HEAD_DIM
    rope = tabs is not None
    in_specs = [pl.BlockSpec((tr, qw), lambda i: (rb0 + i, cols["gq"][0] // qw)),
                pl.BlockSpec((tr, kw), lambda i: (rb0 + i, cols["gk"][0] // kw)),
                pl.BlockSpec((1, GQA_HEAD_DIM), lambda i: (0, 0)),
                pl.BlockSpec((1, GQA_HEAD_DIM), lambda i: (0, 0))]
    args = [proj, proj, qg.reshape(1, -1), kg.reshape(1, -1)]
    out_shape = [jax.ShapeDtypeStruct((nrows, qw), BF16), jax.ShapeDtypeStruct((nrows, kw), BF16)]
    out_specs = [pl.BlockSpec((tr, qw), lambda i: (i, 0)), pl.BlockSpec((tr, kw), lambda i: (i, 0))]
    if rope:
        per = seq_len // tr
        in_specs += [pl.BlockSpec((tr, LANE), lambda i: (i % per, 0))] * 2
        args += list(tabs)
    else:
        out_shape.append(jax.ShapeDtypeStruct((nrows, kw), F32))
        out_specs.append(pl.BlockSpec((tr, kw), lambda i: (i, 0)))
    return pl.pallas_call(
        functools.partial(_gqa_prep_kernel, rope=rope, scale=GQA_HEAD_DIM ** -0.5, n_q=n_q, n_kv=n_kv),
        out_shape=tuple(out_shape),
        grid=(nrows // tr,),
        in_specs=in_specs,
        out_specs=tuple(out_specs),
        compiler_params=_cparams(("parallel",)),
    )(*args)


MLA_KV_IN = MLA_KV_LORA + LANE
MLA_QK = 2 * LANE


def _mla_prep_kernel(*refs, rope):
    if rope:
        qd_ref, ckv_ref, sm_ref, qg_ref, kvg_ref, cos_ref, sin_ref, qo_ref, kvo_ref = refs
    else:
        qd_ref, ckv_ref, sm_ref, qg_ref, kvg_ref, qo_ref, kvo_ref, ckv32_ref = refs
    qo_ref[...] = _rms(qd_ref[...].astype(F32), qg_ref[...]).astype(qo_ref.dtype)
    ckv = _rms(ckv_ref[...].astype(F32), kvg_ref[...])
    kvo_ref[:, :MLA_KV_LORA] = ckv.astype(kvo_ref.dtype)
    sm = sm_ref[...]
    if rope:
        sm = _rope(sm, cos_ref[...], sin_ref[...], MLA_ROPE // 4)
    else:
        ckv32_ref[...] = ckv
    lane = lax.broadcasted_iota(jnp.int32, sm.shape, 1)
    kvo_ref[:, MLA_KV_LORA:] = jnp.where(lane < MLA_ROPE, sm, 0.0).astype(kvo_ref.dtype)


def _mla_prep(proj, small, qg, kvg, tabs, *, row0, nrows, seq_len, cols):
    tr = TR_ROWS if tabs is None else min(TR_ROWS, seq_len)
    rb0 = row0 // tr
    qw, cw = cols["mqd"][1], cols["ckv"][1]
    rope = tabs is not None
    in_specs = [pl.BlockSpec((tr, qw), lambda i: (rb0 + i, cols["mqd"][0] // qw)),
                pl.BlockSpec((tr, cw), lambda i: (rb0 + i, cols["ckv"][0] // cw)),
                pl.BlockSpec((tr, LANE), lambda i: (rb0 + i, 0)),
                pl.BlockSpec((1, qw), lambda i: (0, 0)),
                pl.BlockSpec((1, cw), lambda i: (0, 0))]
    args = [proj, proj, small, qg.reshape(1, -1), kvg.reshape(1, -1)]
    out_shape = [jax.ShapeDtypeStruct((nrows, qw), BF16), jax.ShapeDtypeStruct((nrows, MLA_KV_IN), BF16)]
    out_specs = [pl.BlockSpec((tr, qw), lambda i: (i, 0)), pl.BlockSpec((tr, MLA_KV_IN), lambda i: (i, 0))]
    if rope:
        per = seq_len // tr
        in_specs += [pl.BlockSpec((tr, LANE), lambda i: (i % per, 0))] * 2
        args += list(tabs)
    else:
        out_shape.append(jax.ShapeDtypeStruct((nrows, cw), F32))
        out_specs.append(pl.BlockSpec((tr, cw), lambda i: (i, 0)))
    return pl.pallas_call(
        functools.partial(_mla_prep_kernel, rope=rope),
        out_shape=tuple(out_shape),
        grid=(nrows // tr,),
        in_specs=in_specs,
        out_specs=tuple(out_specs),
        compiler_params=_cparams(("parallel",)),
    )(*args)


def _mla_q_kernel(*refs, rope, scale):
    if rope:
        x_ref, w_ref, cos_ref, sin_ref, o_ref = refs
    else:
        x_ref, w_ref, o_ref = refs
    acc = jnp.dot(x_ref[...], w_ref[...], preferred_element_type=F32) * scale
    if rope:
        o_ref[:, :LANE] = acc[:, :LANE].astype(o_ref.dtype)
        o_ref[:, LANE:] = _rope(acc[:, LANE:], cos_ref[...], sin_ref[...], MLA_ROPE // 4).astype(o_ref.dtype)
    else:
        o_ref[...] = acc.astype(o_ref.dtype)


def _mla_q(qdn, w_q, tabs, *, seq_len):
    nrows, k = qdn.shape
    rope = tabs is not None
    tm = min(TR_ROWS, seq_len) if rope else TR_ROWS
    in_specs = [pl.BlockSpec((tm, k), lambda i, h: (i, 0)), pl.BlockSpec((k, MLA_QK), lambda i, h: (0, h))]
    args = [qdn, w_q]
    if rope:
        per = seq_len // tm
        in_specs += [pl.BlockSpec((tm, LANE), lambda i, h: (i % per, 0))] * 2
        args += list(tabs)
    return pl.pallas_call(
        functools.partial(_mla_q_kernel, rope=rope, scale=(MLA_NOPE + MLA_ROPE) ** -0.5),
        out_shape=jax.ShapeDtypeStruct((nrows, MLA_HEADS * MLA_QK), BF16),
        grid=(nrows // tm, MLA_HEADS),
        in_specs=in_specs,
        out_specs=pl.BlockSpec((tm, MLA_QK), lambda i, h: (i, h)),
        compiler_params=_cparams(("parallel", "arbitrary")),
    )(*args)


def _matmul_kernel(x_ref, w_ref, o_ref):
    o_ref[...] = jnp.dot(x_ref[...], w_ref[...], preferred_element_type=F32).astype(o_ref.dtype)


def _mla_kv(kv_in, w_kv):
    nrows, k = kv_in.shape
    n = w_kv.shape[1]
    tm = 512 if nrows % 512 == 0 else 256
    return pl.pallas_call(
        _matmul_kernel,
        out_shape=jax.ShapeDtypeStruct((nrows, n), BF16),
        grid=(nrows // tm,),
        in_specs=[pl.BlockSpec((tm, k), lambda i: (i, 0)), pl.BlockSpec((k, n), lambda i: (0, 0))],
        out_specs=pl.BlockSpec((tm, n), lambda i: (i, 0)),
        compiler_params=_cparams(("parallel",)),
    )(kv_in, w_kv)


def _conv_kernel(x_ref, w_ref, b_ref, o_ref):
    x = x_ref[0].astype(F32)
    seq = x.shape[0]
    row = lax.broadcasted_iota(jnp.int32, x.shape, 0)
    pad = SSD_CONV // 2
    acc = x * w_ref[pad:pad + 1, :] + b_ref[...]
    for k in range(SSD_CONV):
        d = k - pad
        if d == 0:
            continue
        shifted = pltpu.roll(x, (-d) % seq, 0)
        valid = (row + d >= 0) & (row + d < seq)
        acc = acc + jnp.where(valid, shifted, 0.0) * w_ref[k:k + 1, :]
    o_ref[0] = _silu(acc).astype(o_ref.dtype)


def _ssd_conv(proj, conv_w, conv_b, *, seq0, nseq, seq_len, cols):
    npad = proj.shape[1]
    c0, cw = cols["sxbc"]
    tc = 256
    view = proj.reshape(-1, seq_len, npad)
    return pl.pallas_call(
        _conv_kernel,
        out_shape=jax.ShapeDtypeStruct((nseq, seq_len, cw), BF16),
        grid=(nseq, cw // tc),
        in_specs=[pl.BlockSpec((1, seq_len, tc), lambda s, c: (seq0 + s, 0, c0 // tc + c)),
                  pl.BlockSpec((SSD_CONV, tc), lambda s, c: (0, c)),
                  pl.BlockSpec((1, tc), lambda s, c: (0, c))],
        out_specs=pl.BlockSpec((1, seq_len, tc), lambda s, c: (s, 0, c)),
        compiler_params=_cparams(("parallel", "parallel")),
    )(view, conv_w, conv_b.reshape(1, -1)).reshape(nseq * seq_len, cw)


def _pair_cols(vals, h0):
    q = vals.shape[0]
    lane = lax.broadcasted_iota(jnp.int32, (q, LANE), 1)
    return jnp.where(lane < SSD_HEAD_DIM, vals[:, h0:h0 + 1], vals[:, h0 + 1:h0 + 2])


def _ssd_kernel(*refs, reverse, zero_init, final, d):
    if final:
        (x_ref, b_ref, c_ref, dtc_ref, dtr_ref, biasc_ref, biasr_ref, alogc_ref, alogr_ref, h0_ref,
         yf_ref, z_ref, dskip_ref, ng_ref, y_ref, hout_ref, st_ref) = refs
    else:
        (x_ref, b_ref, c_ref, dtc_ref, dtr_ref, biasc_ref, biasr_ref, alogc_ref, alogr_ref, h0_ref,
         y_ref, hout_ref, st_ref) = refs
    q = SSD_CHUNK
    nh = dtc_ref.shape[1] // 2
    hpg = nh // SSD_GROUPS
    gw = hpg * SSD_HEAD_DIM
    c = pl.program_id(1)

    @pl.when(c == 0)
    def _():
        for g in range(SSD_GROUPS):
            if zero_init:
                st_ref[g] = jnp.zeros(st_ref.shape[1:], F32)
            else:
                st_ref[g] = h0_ref[0, g].T

    lo = d * nh
    dt_c = _softplus(dtc_ref[:, lo:lo + nh] + biasc_ref[:, lo:lo + nh])
    dt_r = _softplus(dtr_ref[lo:lo + nh, :] + biasr_ref[lo:lo + nh, :])
    a_c = dt_c * (-jnp.exp(alogc_ref[:, lo:lo + nh]))
    a_r = dt_r * (-jnp.exp(alogr_ref[lo:lo + nh, :]))
    ii = lax.broadcasted_iota(jnp.int32, (q, q), 0)
    jj = lax.broadcasted_iota(jnp.int32, (q, q), 1)
    causal = (jj >= ii) if reverse else (jj <= ii)
    tri = causal.astype(F32)
    tri_t = ((ii >= jj) if reverse else (ii <= jj)).astype(F32)
    cum_c = jnp.dot(tri, a_c, precision=HIGHEST, preferred_element_type=F32)
    cum_r = jnp.dot(a_r, tri_t, precision=HIGHEST, preferred_element_type=F32)
    tot_c = jnp.sum(a_c, axis=0, keepdims=True)
    e_in = jnp.exp(cum_c)
    w_out = dt_c * jnp.exp(tot_c - cum_c)
    e_tot = jnp.exp(tot_c)

    x = x_ref[...]
    lane = lax.broadcasted_iota(jnp.int32, (q, LANE), 1)
    lane_row = lax.broadcasted_iota(jnp.int32, (1, LANE), 1)
    y_groups = []
    for g in range(SSD_GROUPS):
        bg = b_ref[:, g * SSD_STATE:(g + 1) * SSD_STATE]
        cg = c_ref[:, g * SSD_STATE:(g + 1) * SSD_STATE]
        cb = lax.dot_general(cg, bg, (((1,), (1,)), ((), ())), preferred_element_type=F32)
        st = st_ref[g]
        y_in = jnp.dot(cg, st.astype(BF16), preferred_element_type=F32)
        y_pairs, xs_pairs, dec_pairs = [], [], []
        for pr in range(hpg // 2):
            h0 = g * hpg + 2 * pr
            x_pair = x[:, h0 * SSD_HEAD_DIM:(h0 + 2) * SSD_HEAD_DIM]
            yd = []
            for h in (h0, h0 + 1):
                seg = cum_c[:, h:h + 1] - cum_r[h:h + 1, :]
                decay = jnp.where(causal, jnp.exp(jnp.minimum(seg, 0.0)), 0.0)
                att = (cb * decay * dt_r[h:h + 1, :]).astype(BF16)
                yd.append(jnp.dot(att, x_pair, preferred_element_type=F32))
            y_pair = jnp.where(lane < SSD_HEAD_DIM, yd[0], yd[1])
            y_pair = y_pair + _pair_cols(e_in, h0) * y_in[:, 2 * pr * SSD_HEAD_DIM:(2 * pr + 2) * SSD_HEAD_DIM]
            y_pairs.append(y_pair)
            xs_pairs.append((x_pair.astype(F32) * _pair_cols(w_out, h0)).astype(BF16))
            dec_pairs.append(jnp.where(lane_row < SSD_HEAD_DIM, e_tot[:, h0:h0 + 1], e_tot[:, h0 + 1:h0 + 2]))
        xs_dec = jnp.concatenate(xs_pairs, axis=1)
        upd = lax.dot_general(bg, xs_dec, (((0,), (0,)), ((), ())), preferred_element_type=F32)
        st_ref[g] = st * jnp.concatenate(dec_pairs, axis=1) + upd
        y_groups.append(jnp.concatenate(y_pairs, axis=1))
    y = jnp.concatenate(y_groups, axis=1)

    if final:
        y = y + yf_ref[...] + dskip_ref[...] * x.astype(F32)
        y = y * _silu(z_ref[...].astype(F32))
        y_ref[...] = _rms(y, ng_ref[...]).astype(y_ref.dtype)
    else:
        y_ref[...] = y

    @pl.when(c == pl.num_programs(1) - 1)
    def _():
        for g in range(SSD_GROUPS):
            hout_ref[0, g] = st_ref[g].T


def _ssd_pass(xbc, dt_col, dt_row, dt_bias, a_log, h0, extra, out_buf, *, d, nseq, seq_len, row0, cols):
    q = SSD_CHUNK
    nc = seq_len // q
    nrows = nseq * seq_len
    nh2 = dt_col.shape[1]
    inner = cols["sz"][1]
    gw = inner // SSD_GROUPS
    bw = SSD_GROUPS * SSD_STATE
    rb0 = row0 // q
    reverse = d == 1
    final = extra is not None
    cidx = (lambda c: nc - 1 - c) if reverse else (lambda c: c)
    loc = lambda s, c: s * nc + cidx(c)
    zero_init = h0 is None
    if zero_init:
        h0 = jnp.zeros((1, SSD_GROUPS, gw, SSD_STATE), F32)
    in_specs = [pl.BlockSpec((q, inner), lambda s, c: (loc(s, c), 0)),
                pl.BlockSpec((q, bw), lambda s, c: (loc(s, c), inner // bw)),
                pl.BlockSpec((q, bw), lambda s, c: (loc(s, c), inner // bw + 1)),
                pl.BlockSpec((q, nh2), lambda s, c: (rb0 + loc(s, c), 0)),
                pl.BlockSpec((nh2, q), lambda s, c: (0, rb0 + loc(s, c))),
                pl.BlockSpec((1, nh2), lambda s, c: (0, 0)),
                pl.BlockSpec((nh2, 1), lambda s, c: (0, 0)),
                pl.BlockSpec((1, nh2), lambda s, c: (0, 0)),
                pl.BlockSpec((nh2, 1), lambda s, c: (0, 0)),
                pl.BlockSpec((1, SSD_GROUPS, gw, SSD_STATE), (lambda s, c: (0, 0, 0, 0)) if zero_init
                             else (lambda s, c: (s, 0, 0, 0)))]
    args = [xbc, xbc, xbc, dt_col, dt_row, dt_bias.reshape(1, -1), dt_bias.reshape(-1, 1),
            a_log.reshape(1, -1), a_log.reshape(-1, 1), h0]
    kern = functools.partial(_ssd_kernel, reverse=reverse, zero_init=zero_init, final=final, d=d)
    aliases = {}
    if final:
        y_fwd, proj, d_skip, norm_g = extra
        in_specs += [pl.BlockSpec((q, inner), lambda s, c: (loc(s, c), 0)),
                     pl.BlockSpec((q, inner), lambda s, c: (rb0 + loc(s, c), cols["sz"][0] // inner)),
                     pl.BlockSpec((1, inner), lambda s, c: (0, 0)),
                     pl.BlockSpec((1, inner), lambda s, c: (0, 0)),
                     pl.BlockSpec(memory_space=pl.ANY)]
        args += [y_fwd, proj, d_skip.reshape(1, -1), norm_g.reshape(1, -1), out_buf]
        aliases = {len(args) - 1: 0}
        kern = functools.partial(_skip_input, kern, len(args) - 1)
        y_shape = jax.ShapeDtypeStruct(out_buf.shape, out_buf.dtype)
        y_spec = pl.BlockSpec((q, inner), lambda s, c: (rb0 + loc(s, c), 0))
    else:
        y_shape = jax.ShapeDtypeStruct((nrows, inner), F32)
        y_spec = pl.BlockSpec((q, inner), lambda s, c: (loc(s, c), 0))
    return pl.pallas_call(
        kern,
        out_shape=(y_shape, jax.ShapeDtypeStruct((nseq, SSD_GROUPS, gw, SSD_STATE), F32)),
        grid=(nseq, nc),
        in_specs=in_specs,
        out_specs=(y_spec, pl.BlockSpec((1, SSD_GROUPS, gw, SSD_STATE), lambda s, c: (s, 0, 0, 0))),
        scratch_shapes=[pltpu.VMEM((SSD_GROUPS, SSD_STATE, gw), F32)],
        input_output_aliases=aliases,
        compiler_params=_cparams(("parallel", "arbitrary")),
    )(*args)


def _s5_kernel(u_ref, wt_ref, ws_ref, wc_ref, a_ref, h0_ref, y_ref, hout_ref, s_scr, hin_scr, *, nseq, nc):
    gb = u_ref.shape[1]
    rows = u_ref.shape[3]
    depth = S5_CHUNK * S5_GROUP_CH
    npair = gb // 2
    w = npair * LANE
    contract_rows = (((0,), (0,)), ((), ()))
    contract_cols = (((1,), (1,)), ((), ()))

    def u_t(g):
        return u_ref[:, g].reshape(depth, rows)

    for p in range(npair):
        s = (lax.dot_general(u_t(2 * p), ws_ref[2 * p], contract_rows, preferred_element_type=F32)
             + lax.dot_general(u_t(2 * p + 1), ws_ref[2 * p + 1], contract_rows, preferred_element_type=F32))
        for comp in range(4):
            s_scr[:, comp * w + p * LANE:comp * w + (p + 1) * LANE] = s[:, comp * LANE:(comp + 1) * LANE]

    af_re, af_im, ab_re, ab_im = a_ref[0], a_ref[1], a_ref[2], a_ref[3]

    def one_sequence(sq, _):
        def step(i, carry):
            hf_re, hf_im, hb_re, hb_im = carry
            rf = sq * nc + i
            rb = sq * nc + nc - 1 - i
            hin_scr[pl.ds(rf, 1), 0:w] = hf_re
            hin_scr[pl.ds(rf, 1), w:2 * w] = hf_im
            hin_scr[pl.ds(rb, 1), 2 * w:3 * w] = hb_re
            hin_scr[pl.ds(rb, 1), 3 * w:4 * w] = hb_im
            sf_re = s_scr[pl.ds(rf, 1), 0:w]
            sf_im = s_scr[pl.ds(rf, 1), w:2 * w]
            sb_re = s_scr[pl.ds(rb, 1), 2 * w:3 * w]
            sb_im = s_scr[pl.ds(rb, 1), 3 * w:4 * w]
            return (af_re * hf_re - af_im * hf_im + sf_re,
                    af_re * hf_im + af_im * hf_re + sf_im,
                    ab_re * hb_re - ab_im * hb_im + sb_re,
                    ab_re * hb_im + ab_im * hb_re + sb_im)

        last = lax.fori_loop(0, nc, step, tuple(h0_ref[sq, comp] for comp in range(4)))
        for comp in range(4):
            hout_ref[sq, comp] = last[comp]
        return 0

    lax.fori_loop(0, nseq, one_sequence, 0)

    for p in range(npair):
        hin = jnp.concatenate([hin_scr[:, comp * w + p * LANE:comp * w + (p + 1) * LANE] for comp in range(4)],
                              axis=1).astype(BF16)
        for e in range(2):
            g = 2 * p + e
            y = (jnp.dot(wt_ref[g], u_t(g), preferred_element_type=F32)
                 + lax.dot_general(wc_ref[g], hin, contract_cols, preferred_element_type=F32))
            y_ref[:, g] = y.reshape(S5_CHUNK, S5_GROUP_CH, rows)


def _s5_scan(u_t, w_toep, w_state, w_carry, a_pow, h0, y_buf, *, nseq, seq_len, row0):
    nt, ng, nh, _ = u_t.shape
    nc = seq_len // S5_CHUNK
    rows = nseq * nc
    rblk = row0 // rows
    gb = S5_GB
    kw = nt * nh
    w = (gb // 2) * LANE
    kern = functools.partial(_s5_kernel, nseq=nseq, nc=nc)
    return pl.pallas_call(
        functools.partial(_skip_input, kern, 6),
        out_shape=(jax.ShapeDtypeStruct(y_buf.shape, y_buf.dtype),
                   jax.ShapeDtypeStruct(h0.shape, F32)),
        grid=(ng // gb,),
        in_specs=[pl.BlockSpec((nt, gb, nh, rows), lambda j: (0, j, 0, rblk)),
                  pl.BlockSpec((gb, kw, kw), lambda j: (j, 0, 0)),
                  pl.BlockSpec((gb, kw, 4 * LANE), lambda j: (j, 0, 0)),
                  pl.BlockSpec((gb, kw, 4 * LANE), lambda j: (j, 0, 0)),
                  pl.BlockSpec((4, 1, w), lambda j: (0, 0, j)),
                  pl.BlockSpec((nseq, 4, 1, w), lambda j: (0, 0, 0, j)),
                  pl.BlockSpec(memory_space=pl.ANY)],
        out_specs=(pl.BlockSpec((nt, gb, nh, rows), lambda j: (0, j, 0, rblk)),
                   pl.BlockSpec((nseq, 4, 1, w), lambda j: (0, 0, 0, j))),
        scratch_shapes=[pltpu.VMEM((rows, 4 * w), F32), pltpu.VMEM((rows, 4 * w), F32)],
        input_output_aliases={6: 0},
        compiler_params=_cparams(("parallel",)),
    )(u_t, w_toep, w_state, w_carry, a_pow, h0, y_buf)


def _s5_weights(lam_re, lam_im, log_step, b_re, b_im, c_re, c_im):
    t = S5_CHUNK
    ng, ns, nh = b_re.shape
    step = jnp.exp(log_step)[..., None]
    lr, li = lam_re * step, lam_im * step
    n = jnp.arange(t + 1, dtype=F32)[:, None, None, None]
    mag = jnp.exp(lr[None] * n)
    pw_re, pw_im = mag * jnp.cos(li[None] * n), mag * jnp.sin(li[None] * n)
    a_re, a_im = pw_re[1], pw_im[1]
    den = lam_re * lam_re + lam_im * lam_im
    k_re = ((a_re - 1.0) * lam_re + a_im * lam_im) / den
    k_im = (a_im * lam_re - (a_re - 1.0) * lam_im) / den
    bt_re, bt_im = b_re.transpose(0, 2, 1), b_im.transpose(0, 2, 1)
    w_re = k_re[:, :, None, :] * bt_re[None] - k_im[:, :, None, :] * bt_im[None]
    w_im = k_re[:, :, None, :] * bt_im[None] + k_im[:, :, None, :] * bt_re[None]

    def times_pow(idx, d, x_re, x_im):
        p_re, p_im = pw_re[idx, d][:, :, None, :], pw_im[idx, d][:, :, None, :]
        return p_re * x_re[None] - p_im * x_im[None], p_re * x_im[None] + p_im * x_re[None]

    ti = jnp.arange(t)
    kern = []
    for d in range(2):
        aw_re, aw_im = times_pow(ti, d, w_re[d], w_im[d])
        kern.append(jnp.einsum("gop,tgip->tgoi", c_re, aw_re, precision=HIGHEST)
                    - jnp.einsum("gop,tgip->tgoi", c_im, aw_im, precision=HIGHEST))
    zeros = lambda k: jnp.zeros((k,) + kern[0].shape[1:], F32)
    rows_out = [jnp.concatenate([kern[0][:to + 1][::-1], zeros(t - 1 - to)], axis=0)
                + jnp.concatenate([zeros(to), kern[1][:t - to]], axis=0) for to in range(t)]
    toep_t = jnp.stack(rows_out, axis=0).transpose(2, 0, 3, 1, 4).reshape(ng, t * nh, t * nh)

    def pack(comps):
        x = jnp.stack(comps, axis=0).transpose(2, 1, 3, 0, 4).reshape(ng, t * nh, 4, ns)
        even = (jnp.arange(ng) % 2 == 0)[:, None, None, None]
        x = jnp.concatenate([jnp.where(even, x, 0.0), jnp.where(even, 0.0, x)], axis=-1)
        return x.reshape(ng, t * nh, 4 * 2 * ns)

    w_state = pack(times_pow(t - 1 - ti, 0, w_re[0], w_im[0]) + times_pow(ti, 1, w_re[1], w_im[1]))
    cf_re, cf_im = times_pow(ti + 1, 0, c_re, c_im)
    cb_re, cb_im = times_pow(t - ti, 1, c_re, c_im)
    carry_t = pack((cf_re, -cf_im, cb_re, -cb_im))
    a_pow = jnp.stack([pw_re[t, 0], pw_im[t, 0], pw_re[t, 1], pw_im[t, 1]], axis=0).reshape(4, 1, ng * ns)
    return toep_t.astype(BF16), w_state.astype(BF16), carry_t.astype(BF16), a_pow


def _glu_kernel(y_ref, u_ref, d_ref, wv_ref, wg_ref, o_ref, vs_ref):
    @pl.when(pl.program_id(1) == 0)
    def _():
        v = y_ref[...] + d_ref[...] * u_ref[...].astype(F32)
        v = 0.5 * v * (1.0 + jnp.tanh(math.sqrt(2.0 / math.pi) * (v + 0.044715 * (v * v * v))))
        vs_ref[...] = v.astype(BF16)

    v = vs_ref[...]
    a = jnp.dot(v, wv_ref[...], preferred_element_type=F32)
    b = jnp.dot(v, wg_ref[...], preferred_element_type=F32)
    o_ref[...] = (a * jax.nn.sigmoid(b)).astype(o_ref.dtype)


def _s5_glu(y, proj, d_skip, w_glu, dims, cols):
    m, wd = y.shape
    tm, tn = min(TM, dims["lat_len"]), 512
    nj = wd // tn
    return pl.pallas_call(
        _glu_kernel,
        out_shape=jax.ShapeDtypeStruct((m, wd), BF16),
        grid=(m // tm, nj),
        in_specs=[pl.BlockSpec((tm, wd), lambda i, j: (i, 0)),
                  pl.BlockSpec((tm, wd), lambda i, j: (i, cols["s5u"][0] // wd)),
                  pl.BlockSpec((1, wd), lambda i, j: (0, 0)),
                  pl.BlockSpec((wd, tn), lambda i, j: (0, j)),
                  pl.BlockSpec((wd, tn), lambda i, j: (0, j + nj))],
        out_specs=pl.BlockSpec((tm, tn), lambda i, j: (i, j)),
        scratch_shapes=[pltpu.VMEM((tm, wd), BF16)],
        compiler_params=_cparams(("parallel", "arbitrary")),
    )(y, proj, d_skip.reshape(1, -1), w_glu, w_glu)


def _columns(d_model, branch_w):
    kvw = GQA_KV_HEADS * GQA_HEAD_DIM
    xbc = branch_w + 2 * SSD_GROUPS * SSD_STATE
    order = (("gate", N_BRANCH * d_model), ("gq", branch_w), ("sz", branch_w), ("s5u", branch_w),
             ("sxbc", xbc), ("mqd", MLA_Q_LORA), ("ckv", MLA_KV_LORA), ("gk", kvw), ("gv", kvw))
    cols, off = {}, 0
    for name, width in order:
        cols[name] = (off, width)
        off += width
    return cols, off


def _prep_w_in(w_in, d_model, branch_w, n_dt):
    kvw = GQA_KV_HEADS * GQA_HEAD_DIM
    xbc = branch_w + 2 * SSD_GROUPS * SSD_STATE
    splits = (N_BRANCH * d_model, branch_w, kvw, kvw, branch_w, xbc, n_dt, MLA_Q_LORA, MLA_KV_LORA + MLA_ROPE, branch_w)
    bounds, acc = [], 0
    for wd in splits[:-1]:
        acc += wd
        bounds.append(acc)
    gate, gq, gk, gv, sz, sxbc, sdt, mqd, mkvd, s5u = jnp.split(w_in, bounds, axis=-1)
    ckv, kpe = mkvd[..., :MLA_KV_LORA], mkvd[..., MLA_KV_LORA:]
    main = jnp.concatenate([gate, gq, sz, s5u, sxbc, mqd, ckv, gk, gv], axis=-1).astype(BF16)
    pad = jnp.zeros(w_in.shape[:-1] + (LANE - MLA_ROPE - n_dt,), w_in.dtype)
    small = jnp.concatenate([kpe, sdt, pad], axis=-1).astype(BF16)
    return main, small


def _prep_mla(w_uq, w_ukv):
    depth = w_uq.shape[0]
    qk = MLA_NOPE + MLA_ROPE
    wq = w_uq.reshape(depth, MLA_Q_LORA, MLA_HEADS, qk)
    wq = jnp.pad(wq, ((0, 0), (0, 0), (0, 0), (0, MLA_QK - qk))).reshape(depth, MLA_Q_LORA, MLA_HEADS * MLA_QK)
    wkv = w_ukv.reshape(depth, MLA_KV_LORA, MLA_HEADS, MLA_NOPE + LANE)
    k_nope, v = wkv[..., :MLA_NOPE], wkv[..., MLA_NOPE:]
    k_top = jnp.pad(k_nope, ((0, 0), (0, 0), (0, 0), (0, MLA_QK - MLA_NOPE)))
    eye = jnp.eye(LANE, MLA_QK, k=MLA_NOPE, dtype=w_ukv.dtype) * (jnp.arange(LANE) < MLA_ROPE)[:, None]
    k_bot = jnp.broadcast_to(eye[None, :, None, :], (depth, LANE, MLA_HEADS, MLA_QK))
    k_all = jnp.concatenate([k_top, k_bot], axis=1).reshape(depth, MLA_KV_IN, MLA_HEADS * MLA_QK)
    v_all = jnp.pad(v, ((0, 0), (0, LANE), (0, 0), (0, 0))).reshape(depth, MLA_KV_IN, MLA_HEADS * LANE)
    return wq.astype(BF16), jnp.concatenate([k_all, v_all], axis=-1).astype(BF16)


def kernel(x_prompt, x_sample, cache_gqa_k, cache_gqa_v, cache_mla_ckv, cache_mla_kpe, state_ssd, state_s5, c, c_ctx, norm1_g, norm2_g, w_mod, b_mod, w_in, gqa_qn_g, gqa_kn_g, ssd_conv_w, ssd_conv_b, ssd_a_log, ssd_dt_bias, ssd_d, ssd_norm_g, mla_qn_g, mla_w_uq, mla_kvn_g, mla_w_ukv, s5_lam_re, s5_lam_im, s5_log_step, s5_b_re, s5_b_im, s5_c_re, s5_c_im, s5_d, s5_w_glu, w_branch, w_out, w_ffn_in, w_ffn_out, final_g):
    nb_ctx, len_ctx, d_model = x_prompt.shape
    nb_lat, len_lat, _ = x_sample.shape
    depth = w_in.shape[0]
    past = cache_gqa_k.shape[2]
    branch_w = w_branch.shape[2]
    n_heads = ssd_d.shape[1]
    ctx_rows, lat_rows = nb_ctx * len_ctx, nb_lat * len_lat
    m = ctx_rows + lat_rows
    dims = {"ctx_rows": ctx_rows, "lat_len": len_lat}
    cols, _ = _columns(d_model, branch_w)
    kvw = GQA_KV_HEADS * GQA_HEAD_DIM
    n_s5 = branch_w // S5_GROUP_CH

    w_main, w_small = _prep_w_in(w_in, d_model, branch_w, 2 * n_heads)
    w_q, w_kv = _prep_mla(mla_w_uq, mla_w_ukv)
    s5_w = jax.vmap(_s5_weights)(s5_lam_re, s5_lam_im, s5_log_step, s5_b_re, s5_b_im, s5_c_re, s5_c_im)
    tabs_a = _rope_tables(len_lat, GQA_HEAD_DIM)
    tabs_c = _rope_tables(len_lat, MLA_ROPE)
    n_mod = -(-(nb_lat + 1) // 8) * 8
    cvec = jnp.concatenate([c_ctx[None], c, jnp.zeros((n_mod - nb_lat - 1, d_model), F32)], axis=0)
    mod_all = _modulation(cvec, w_mod, b_mod).reshape(depth, n_mod, 6, 1, d_model)
    zpad = jnp.zeros(cache_mla_kpe.shape[:-1] + (LANE - MLA_ROPE,), F32)
    layer_in = dict(
        mod=mod_all, norm1_g=norm1_g, norm2_g=norm2_g, w_main=w_main, w_small=w_small,
        gqa_qn_g=gqa_qn_g, gqa_kn_g=gqa_kn_g, conv_w=ssd_conv_w, conv_b=ssd_conv_b, a_log=ssd_a_log,
        dt_bias=ssd_dt_bias, d_skip=jnp.repeat(ssd_d, SSD_HEAD_DIM, axis=-1), ssd_norm_g=ssd_norm_g,
        mla_qn_g=mla_qn_g, mla_kvn_g=mla_kvn_g, w_q=w_q, w_kv=w_kv,
        s5_toep=s5_w[0], s5_state=s5_w[1], s5_carry=s5_w[2], s5_apow=s5_w[3], s5_d=s5_d,
        w_glu=s5_w_glu.astype(BF16), w_branch=w_branch.astype(BF16), w_out=w_out.astype(BF16),
        w_ffn_in=w_ffn_in.astype(BF16), w_ffn_out=w_ffn_out.astype(BF16),
        cache_k=jnp.moveaxis(cache_gqa_k, 1, 0).reshape(depth, nb_lat, past, kvw).astype(BF16),
        cache_v=jnp.moveaxis(cache_gqa_v, 1, 0).reshape(depth, nb_lat, past, kvw).astype(BF16),
        cache_kv=jnp.moveaxis(jnp.concatenate([cache_mla_ckv, cache_mla_kpe, zpad], axis=-1), 1, 0).astype(BF16),
        h0_ssd=jnp.moveaxis(state_ssd, (1, 2), (0, 1)).reshape(
            depth, 2, nb_lat, SSD_GROUPS, branch_w // SSD_GROUPS, SSD_STATE),
        h0_s5=jnp.moveaxis(state_s5, 1, 0).reshape(depth, nb_lat, 4, 1, n_s5 * S5_STATE),
    )

    def layer(x, lp):
        mod = lp["mod"]
        proj, small = _in_proj(x, lp["norm1_g"], mod, lp["w_main"], lp["w_small"], dims)
        new_buf = lambda: jnp.zeros((m, branch_w), BF16)

        qn, kn, k_own = _gqa_prep(proj, lp["gqa_qn_g"], lp["gqa_kn_g"], None,
                                  row0=0, nrows=ctx_rows, seq_len=len_ctx, cols=cols)
        gqa = dict(n_kv=GQA_KV_HEADS, n_rep=cols["gq"][1] // kvw, dk=GQA_HEAD_DIM, dv=GQA_HEAD_DIM)
        o_a = _attention(qn, kn.reshape(nb_ctx, len_ctx, kvw), proj.reshape(-1, len_ctx, proj.shape[1]), new_buf(),
                         nseq=nb_ctx, seq_len=len_ctx, tq=len_ctx, kv_per_step=GQA_KV_HEADS, k_off=0,
                         v_off=cols["gv"][0], row0=0, **gqa)
        v_own = proj[:ctx_rows, cols["gv"][0]:cols["gv"][0] + kvw]
        qn, kn = _gqa_prep(proj, lp["gqa_qn_g"], lp["gqa_kn_g"], tabs_a,
                           row0=ctx_rows, nrows=lat_rows, seq_len=len_lat, cols=cols)
        k_lat = jnp.concatenate([kn.reshape(nb_lat, len_lat, kvw), lp["cache_k"]], axis=1)
        v_lat = jnp.concatenate([proj[ctx_rows:, cols["gv"][0]:cols["gv"][0] + kvw].reshape(nb_lat, len_lat, kvw),
                                 lp["cache_v"]], axis=1)
        o_a = _attention(qn, k_lat, v_lat, o_a, nseq=nb_lat, seq_len=len_lat, tq=min(TQ_LAT, len_lat),
                         kv_per_step=1, k_off=0, v_off=0, row0=ctx_rows, **gqa)

        mla = dict(n_kv=MLA_HEADS, n_rep=1, dk=MLA_QK, dv=LANE, k_off=0, v_off=MLA_HEADS * MLA_QK)
        qdn, kv_in, ckv_own = _mla_prep(proj, small, lp["mla_qn_g"], lp["mla_kvn_g"], None,
                                        row0=0, nrows=ctx_rows, seq_len=len_ctx, cols=cols)
        kpe_own = small[:ctx_rows, :MLA_ROPE]
        kv = _mla_kv(kv_in, lp["w_kv"]).reshape(nb_ctx, len_ctx, -1)
        o_c = _attention(_mla_q(qdn, lp["w_q"], None, seq_len=len_ctx), kv, kv, new_buf(),
                         nseq=nb_ctx, seq_len=len_ctx, tq=len_ctx, kv_per_step=MLA_HEADS, row0=0, **mla)
        qdn, kv_in = _mla_prep(proj, small, lp["mla_qn_g"], lp["mla_kvn_g"], tabs_c,
                               row0=ctx_rows, nrows=lat_rows, seq_len=len_lat, cols=cols)
        kv_in = jnp.concatenate([kv_in.reshape(nb_lat, len_lat, MLA_KV_IN), lp["cache_kv"]], axis=1)
        kv = _mla_kv(kv_in.reshape(-1, MLA_KV_IN), lp["w_kv"]).reshape(nb_lat, len_lat + past, -1)
        o_c = _attention(_mla_q(qdn, lp["w_q"], tabs_c, seq_len=len_lat), kv, kv, o_c,
                         nseq=nb_lat, seq_len=len_lat, tq=min(TQ_LAT, len_lat), kv_per_step=MLA_LAT_HEADS_PER_STEP,
                         row0=ctx_rows, **mla)

        dt_col = small[:, MLA_ROPE:MLA_ROPE + 2 * n_heads]
        dt_row = dt_col.T
        o_b = new_buf()
        ssd_new = None
        for nseq, seq_len, row0, h0 in ((nb_ctx, len_ctx, 0, None), (nb_lat, len_lat, ctx_rows, lp["h0_ssd"])):
            xbc = _ssd_conv(proj, lp["conv_w"], lp["conv_b"], seq0=row0 // seq_len, nseq=nseq, seq_len=seq_len,
                            cols=cols)
            common = dict(nseq=nseq, seq_len=seq_len, row0=row0, cols=cols)
            y_f, h_f = _ssd_pass(xbc, dt_col, dt_row, lp["dt_bias"], lp["a_log"], None if h0 is None else h0[0],
                                 None, None, d=0, **common)
            o_b, h_b = _ssd_pass(xbc, dt_col, dt_row, lp["dt_bias"], lp["a_log"], None if h0 is None else h0[1],
                                 (y_f, proj, lp["d_skip"], lp["ssd_norm_g"]), o_b, d=1, **common)
            if h0 is None:
                ssd_new = jnp.stack([h_f, h_b], axis=1).reshape(nb_ctx, 2, n_heads, SSD_HEAD_DIM, SSD_STATE)

        u = proj[:, cols["s5u"][0]:cols["s5u"][0] + branch_w]
        nchunk = m // S5_CHUNK
        u_t = u.reshape(nchunk, S5_CHUNK * branch_w).T.reshape(S5_CHUNK, n_s5, S5_GROUP_CH, nchunk)
        s5_args = (u_t, lp["s5_toep"], lp["s5_state"], lp["s5_carry"], lp["s5_apow"])
        y_t, s5_new = _s5_scan(*s5_args, jnp.zeros((nb_ctx, 4, 1, n_s5 * S5_STATE), F32),
                               jnp.zeros(u_t.shape, F32), nseq=nb_ctx, seq_len=len_ctx, row0=0)
        y_t, _ = _s5_scan(*s5_args, lp["h0_s5"], y_t, nseq=nb_lat, seq_len=len_lat, row0=ctx_rows // S5_CHUNK)
        y = y_t.reshape(S5_CHUNK * branch_w, nchunk).T.reshape(m, branch_w)
        o_d = _s5_glu(y, proj, lp["s5_d"], lp["w_glu"], dims, cols)

        mixed = _branch_mix([o_a, o_b, o_c, o_d], proj, lp["w_branch"], cols["gate"][0], dims)
        x = _resid_proj(mixed, lp["w_out"], x, mod, 2, TM, dims)
        hidden = _ffn_in(x, lp["norm2_g"], mod, lp["w_ffn_in"], dims)
        x = _resid_proj(hidden, lp["w_ffn_out"], x, mod, 5, TM_FFN_OUT, dims)
        new = (k_own.reshape(nb_ctx, len_ctx, GQA_KV_HEADS, GQA_HEAD_DIM),
               v_own.astype(F32).reshape(nb_ctx, len_ctx, GQA_KV_HEADS, GQA_HEAD_DIM),
               ckv_own.reshape(nb_ctx, len_ctx, MLA_KV_LORA),
               kpe_own.reshape(nb_ctx, len_ctx, MLA_ROPE),
               ssd_new,
               s5_new.reshape(nb_ctx, 2, 2, n_s5, S5_STATE))
        return x, new

    x0 = jnp.concatenate([x_prompt.reshape(ctx_rows, d_model), x_sample.reshape(lat_rows, d_model)], axis=0)
    x_fin, new = lax.scan(layer, x0, layer_in)
    y = _final_norm(x_fin, final_g)
    return (y[:ctx_rows].reshape(x_prompt.shape), y[ctx_rows:].reshape(x_sample.shape),
            *(jnp.moveaxis(t, 0, 1) for t in new))
```

```python
import functools
import math

import jax
import jax.numpy as jnp
from jax import lax
from jax.experimental import pallas as pl
from jax.experimental.pallas import tpu as pltpu

F32 = jnp.float32
BF16 = jnp.bfloat16
HIGHEST = lax.Precision.HIGHEST

GRID_W = 64
N_BRANCH = 4
GQA_HEAD_DIM = 128
GQA_KV_HEADS = 2
SSD_HEAD_DIM = 64
SSD_GROUPS = 2
SSD_STATE = 128
SSD_CONV = 5
SSD_CHUNK = 128
MLA_HEADS = 8
MLA_NOPE = 128
MLA_ROPE = 64
MLA_Q_LORA = 512
MLA_KV_LORA = 256
S5_GROUP_CH = 16
S5_STATE = 64
S5_CHUNK = 16
ROPE_THETA = 10000.0
NORM_EPS = 1e-6
LANE = 128
VMEM_LIMIT = 56 * 1024 * 1024

TM = 1024
TN_IN = 1280
TN_FFN = 512
TN_OUT = 512
TR_ROWS = 1024
TQ_LAT = 512
MLA_LAT_HEADS_PER_STEP = 4
S5_GB = 8


def _cparams(sem):
    return pltpu.CompilerParams(dimension_semantics=sem, vmem_limit_bytes=VMEM_LIMIT)


def _row_group(i, tm, ctx_rows, lat_len):
    nct = ctx_rows // tm
    per = lat_len // tm
    return jnp.where(i < nct, 0, 1 + (i - nct) // per)


def _lspec(l, block, index_map):
    return pl.BlockSpec((None,) + tuple(block), lambda *ids: (l,) + tuple(index_map(*ids)))


def _silu(x):
    return x * jax.nn.sigmoid(x)


def _softplus(x):
    return jnp.maximum(x, 0.0) + jnp.log(1.0 + jnp.exp(-jnp.abs(x)))


def _rms(x, g):
    ms = jnp.mean(x * x, axis=-1, keepdims=True)
    return x * lax.rsqrt(ms + NORM_EPS) * g


def _mod_kernel(c_ref, w_ref, b_ref, o_ref):
    c = c_ref[...]
    s = _silu(c).astype(BF16)
    o_ref[0] = jnp.dot(s, w_ref[0].astype(BF16), preferred_element_type=F32) + b_ref[0]


def _modulation(cvec, w_mod, b_mod):
    depth, d, n = w_mod.shape
    tn = 1024
    return pl.pallas_call(
        _mod_kernel,
        out_shape=jax.ShapeDtypeStruct((depth, 8, n), F32),
        grid=(depth, n // tn),
        in_specs=[pl.BlockSpec((8, d), lambda l, j: (0, 0)),
                  pl.BlockSpec((1, d, tn), lambda l, j: (l, 0, j)),
                  pl.BlockSpec((1, 1, tn), lambda l, j: (l, 0, j))],
        out_specs=pl.BlockSpec((1, 8, tn), lambda l, j: (l, 0, j)),
        compiler_params=_cparams(("parallel", "parallel")),
    )(cvec, w_mod, b_mod.reshape(depth, 1, n))


def _norm_mod(x_ref, g_ref, sc_ref, sh_ref):
    y = _rms(x_ref[...], g_ref[...])
    return (y * (1.0 + sc_ref[...]) + sh_ref[...]).astype(BF16)


def _in_proj_kernel(x_ref, g_ref, sc_ref, sh_ref, w_ref, ws_ref, o_ref, os_ref, hs_ref):
    @pl.when(pl.program_id(1) == 0)
    def _():
        h = _norm_mod(x_ref, g_ref, sc_ref, sh_ref)
        hs_ref[...] = h
        os_ref[...] = jnp.dot(h, ws_ref[...], preferred_element_type=F32)

    o_ref[...] = jnp.dot(hs_ref[...], w_ref[...], preferred_element_type=F32).astype(o_ref.dtype)


def _in_proj(x, norm_g, mod, w_main, w_small, l, dims):
    m, d = x.shape
    npad = w_main.shape[2]
    tm, tn = min(TM, dims["lat_len"]), TN_IN
    grp = lambda i: _row_group(i, tm, dims["ctx_rows"], dims["lat_len"])
    return pl.pallas_call(
        _in_proj_kernel,
        out_shape=(jax.ShapeDtypeStruct((m, npad), BF16), jax.ShapeDtypeStruct((m, LANE), F32)),
        grid=(m // tm, npad // tn),
        in_specs=[pl.BlockSpec((tm, d), lambda i, j: (i, 0)),
                  _lspec(l, (1, d), lambda i, j: (0, 0)),
                  _lspec(l, (None, None, 1, d), lambda i, j: (grp(i), 1, 0, 0)),
                  _lspec(l, (None, None, 1, d), lambda i, j: (grp(i), 0, 0, 0)),
                  _lspec(l, (d, tn), lambda i, j: (0, j)),
                  _lspec(l, (d, LANE), lambda i, j: (0, 0))],
        out_specs=(pl.BlockSpec((tm, tn), lambda i, j: (i, j)),
                   pl.BlockSpec((tm, LANE), lambda i, j: (i, 0))),
        scratch_shapes=[pltpu.VMEM((tm, d), BF16)],
        compiler_params=_cparams(("parallel", "arbitrary")),
    )(x, norm_g, mod, mod, w_main, w_small)


def _ffn_in_kernel(x_ref, g_ref, sc_ref, sh_ref, wg_ref, wu_ref, o_ref, hs_ref):
    @pl.when(pl.program_id(1) == 0)
    def _():
        hs_ref[...] = _norm_mod(x_ref, g_ref, sc_ref, sh_ref)

    h = hs_ref[...]
    a = jnp.dot(h, wg_ref[...], preferred_element_type=F32)
    b = jnp.dot(h, wu_ref[...], preferred_element_type=F32)
    o_ref[...] = (_silu(a) * b).astype(o_ref.dtype)


def _ffn_in(x, norm_g, mod, w, l, dims):
    m, d = x.shape
    hid = w.shape[2] // 2
    tm, tn = min(TM, dims["lat_len"]), TN_FFN
    nj = hid // tn
    grp = lambda i: _row_group(i, tm, dims["ctx_rows"], dims["lat_len"])
    return pl.pallas_call(
        _ffn_in_kernel,
        out_shape=jax.ShapeDtypeStruct((m, hid), BF16),
        grid=(m // tm, nj),
        in_specs=[pl.BlockSpec((tm, d), lambda i, j: (i, 0)),
                  _lspec(l, (1, d), lambda i, j: (0, 0)),
                  _lspec(l, (None, None, 1, d), lambda i, j: (grp(i), 4, 0, 0)),
                  _lspec(l, (None, None, 1, d), lambda i, j: (grp(i), 3, 0, 0)),
                  _lspec(l, (d, tn), lambda i, j: (0, j)),
                  _lspec(l, (d, tn), lambda i, j: (0, j + nj))],
        out_specs=pl.BlockSpec((tm, tn), lambda i, j: (i, j)),
        scratch_shapes=[pltpu.VMEM((tm, d), BF16)],
        compiler_params=_cparams(("parallel", "arbitrary")),
    )(x, norm_g, mod, mod, w, w)


def _resid_kernel(x_ref, w_ref, r_ref, g_ref, o_ref):
    o_ref[...] = r_ref[...] + g_ref[...] * jnp.dot(x_ref[...], w_ref[...], preferred_element_type=F32)


def _resid_proj(xin, w, resid, mod, mod_idx, tm, l, dims):
    m, k = xin.shape
    n = w.shape[2]
    tm, tn = min(tm, dims["lat_len"]), TN_OUT
    grp = lambda i: _row_group(i, tm, dims["ctx_rows"], dims["lat_len"])
    return pl.pallas_call(
        _resid_kernel,
        out_shape=jax.ShapeDtypeStruct((m, n), F32),
        grid=(m // tm, n // tn),
        in_specs=[pl.BlockSpec((tm, k), lambda i, j: (i, 0)),
                  _lspec(l, (k, tn), lambda i, j: (0, j)),
                  pl.BlockSpec((tm, tn), lambda i, j: (i, j)),
                  _lspec(l, (None, None, 1, tn), lambda i, j: (grp(i), mod_idx, 0, j))],
        out_specs=pl.BlockSpec((tm, tn), lambda i, j: (i, j)),
        compiler_params=_cparams(("parallel", "arbitrary")),
    )(xin, w, resid, mod)


def _mix_kernel(oa_ref, ob_ref, oc_ref, od_ref, ga_ref, gb_ref, gc_ref, gd_ref, w_ref, o_ref):
    acc = None
    for n, (o_n, g_n) in enumerate(((oa_ref, ga_ref), (ob_ref, gb_ref), (oc_ref, gc_ref), (od_ref, gd_ref))):
        proj = jnp.dot(o_n[...], w_ref[n], preferred_element_type=F32)
        term = jax.nn.sigmoid(g_n[...].astype(F32)) * proj
        acc = term if acc is None else acc + term
    o_ref[...] = acc.astype(o_ref.dtype)


def _branch_mix(branches, proj_all, w_branch, col_gate, l, dims):
    m, bw = branches[0].shape
    d = w_branch.shape[3]
    tm, tn = min(TM, dims["lat_len"]), TN_OUT
    gate_specs = [pl.BlockSpec((tm, tn), functools.partial(lambda i, j, n: (i, (col_gate + n * d) // tn + j), n=n))
                  for n in range(N_BRANCH)]
    return pl.pallas_call(
        _mix_kernel,
        out_shape=jax.ShapeDtypeStruct((m, d), BF16),
        grid=(m // tm, d // tn),
        in_specs=[pl.BlockSpec((tm, bw), lambda i, j: (i, 0))] * N_BRANCH + gate_specs
                 + [_lspec(l, (N_BRANCH, bw, tn), lambda i, j: (0, 0, j))],
        out_specs=pl.BlockSpec((tm, tn), lambda i, j: (i, j)),
        compiler_params=_cparams(("parallel", "arbitrary")),
    )(*branches, proj_all, proj_all, proj_all, proj_all, w_branch)


def _final_norm_kernel(x_ref, g_ref, o_ref):
    o_ref[...] = _rms(x_ref[...], g_ref[...])


def _final_norm(x, g, row0, nrows):
    d = x.shape[1]
    tm = 512
    rb0 = row0 // tm
    return pl.pallas_call(
        _final_norm_kernel,
        out_shape=jax.ShapeDtypeStruct((nrows, d), F32),
        grid=(nrows // tm,),
        in_specs=[pl.BlockSpec((tm, d), lambda i: (rb0 + i, 0)), pl.BlockSpec((1, d), lambda i: (0, 0))],
        out_specs=pl.BlockSpec((tm, d), lambda i: (i, 0)),
        compiler_params=_cparams(("parallel",)),
    )(x, g.reshape(1, d))


def _rope_tables(seq_len, dim):
    nf = dim // 4
    t = jnp.arange(seq_len)
    row = (t // GRID_W).astype(F32)
    col = (t % GRID_W).astype(F32)
    inv = ROPE_THETA ** (-jnp.arange(nf, dtype=F32) / nf)
    ang_r, ang_c = row[:, None] * inv, col[:, None] * inv
    cos = jnp.concatenate([jnp.cos(ang_r), jnp.cos(ang_r), jnp.cos(ang_c), jnp.cos(ang_c)], axis=1)
    sin = jnp.concatenate([-jnp.sin(ang_r), jnp.sin(ang_r), -jnp.sin(ang_c), jnp.sin(ang_c)], axis=1)
    pad = LANE - dim
    if pad:
        cos = jnp.concatenate([cos, jnp.ones((seq_len, pad), F32)], axis=1)
        sin = jnp.concatenate([sin, jnp.zeros((seq_len, pad), F32)], axis=1)
    return cos, sin


def _rope(x, cos, sin, nf):
    lane = lax.broadcasted_iota(jnp.int32, x.shape, 1)
    swapped = jnp.where((lane % (2 * nf)) < nf, pltpu.roll(x, LANE - nf, 1), pltpu.roll(x, nf, 1))
    return x * cos + swapped * sin


def _attn_kernel(q_ref, k_ref, v_ref, o_ref, *, kv_per_step, n_rep, dk, dv):
    for a in range(kv_per_step):
        k = k_ref[0, :, a * dk:(a + 1) * dk]
        v = v_ref[0, :, a * dv:(a + 1) * dv]
        for r in range(n_rep):
            h = a * n_rep + r
            q = q_ref[:, h * dk:(h + 1) * dk]
            s = lax.dot_general(q, k, (((1,), (1,)), ((), ())), preferred_element_type=F32)
            p = jnp.exp(s - jnp.max(s, axis=-1, keepdims=True))
            l = jnp.sum(p, axis=-1, keepdims=True)
            o = jnp.dot(p.astype(BF16), v, preferred_element_type=F32)
            o_ref[:, h * dv:(h + 1) * dv] = (o / l).astype(o_ref.dtype)


def _skip_inputs(kern, positions, *refs):
    return kern(*(r for i, r in enumerate(refs) if i not in positions))


def _attention(q, k, v, out_buf, *, nseq, seq_len, tq, n_kv, kv_per_step, n_rep, dk, dv, k_off, v_off, row0):
    lk = k.shape[1]
    nq = seq_len // tq
    rb0 = row0 // tq
    kw, vw = kv_per_step * dk, kv_per_step * dv
    assert k_off % kw == 0 and v_off % vw == 0 and n_kv % kv_per_step == 0
    kb0, vb0 = k_off // kw, v_off // vw
    kern = functools.partial(_attn_kernel, kv_per_step=kv_per_step, n_rep=n_rep, dk=dk, dv=dv)
    return pl.pallas_call(
        functools.partial(_skip_inputs, kern, (3,)),
        out_shape=jax.ShapeDtypeStruct(out_buf.shape, out_buf.dtype),
        grid=(nseq, n_kv // kv_per_step, nq),
        in_specs=[pl.BlockSpec((tq, n_rep * kw), lambda b, g, i: (b * nq + i, g)),
                  pl.BlockSpec((1, lk, kw), lambda b, g, i: (b, 0, kb0 + g)),
                  pl.BlockSpec((1, lk, vw), lambda b, g, i: (b, 0, vb0 + g)),
                  pl.BlockSpec(memory_space=pl.ANY)],
        out_specs=pl.BlockSpec((tq, n_rep * vw), lambda b, g, i: (rb0 + b * nq + i, g)),
        input_output_aliases={3: 0},
        compiler_params=_cparams(("parallel", "parallel", "arbitrary")),
    )(q, k, v, out_buf)


def _gqa_prep_kernel(*refs, rope, scale, n_q, n_kv):
    if rope:
        q_ref, k_ref, qg_ref, kg_ref, cos_ref, sin_ref, qo_ref, ko_ref = refs
    else:
        q_ref, k_ref, qg_ref, kg_ref, qo_ref, ko_ref, k32_ref = refs
    hd = GQA_HEAD_DIM
    for h in range(n_q):
        y = _rms(q_ref[:, h * hd:(h + 1) * hd].astype(F32), qg_ref[...])
        if rope:
            y = _rope(y, cos_ref[...], sin_ref[...], hd // 4)
        qo_ref[:, h * hd:(h + 1) * hd] = (y * scale).astype(qo_ref.dtype)
    for h in range(n_kv):
        y = _rms(k_ref[:, h * hd:(h + 1) * hd].astype(F32), kg_ref[...])
        if rope:
            y = _rope(y, cos_ref[...], sin_ref[...], hd // 4)
        else:
            k32_ref[:, h * hd:(h + 1) * hd] = y
        ko_ref[:, h * hd:(h + 1) * hd] = y.astype(ko_ref.dtype)


def _gqa_prep(proj, qg, kg, tabs, *, l, row0, nrows, seq_len, cols):
    tr = TR_ROWS if tabs is None else min(TR_ROWS, seq_len)
    rb0 = row0 // tr
    qw, kw = cols["gq"][1], cols["gk"][1]
    n_q, n_kv = qw // GQA_HEAD_DIM, kw // GQA_HEAD_DIM
    rope = tabs is not None
    in_specs = [pl.BlockSpec((tr, qw), lambda i: (rb0 + i, cols["gq"][0] // qw)),
                pl.BlockSpec((tr, kw), lambda i: (rb0 + i, cols["gk"][0] // kw)),
                _lspec(l, (1, GQA_HEAD_DIM), lambda i: (0, 0)),
                _lspec(l, (1, GQA_HEAD_DIM), lambda i: (0, 0))]
    args = [proj, proj, qg, kg]
    out_shape = [jax.ShapeDtypeStruct((nrows, qw), BF16), jax.ShapeDtypeStruct((nrows, kw), BF16)]
    out_specs = [pl.BlockSpec((tr, qw), lambda i: (i, 0)), pl.BlockSpec((tr, kw), lambda i: (i, 0))]
    if rope:
        per = seq_len // tr
        in_specs += [pl.BlockSpec((tr, LANE), lambda i: (i % per, 0))] * 2
        args += list(tabs)
    else:
        out_shape.append(jax.ShapeDtypeStruct((nrows, kw), F32))
        out_specs.append(pl.BlockSpec((tr, kw), lambda i: (i, 0)))
    return pl.pallas_call(
        functools.partial(_gqa_prep_kernel, rope=rope, scale=GQA_HEAD_DIM ** -0.5, n_q=n_q, n_kv=n_kv),
        out_shape=tuple(out_shape),
        grid=(nrows // tr,),
        in_specs=in_specs,
        out_specs=tuple(out_specs),
        compiler_params=_cparams(("parallel",)),
    )(*args)


MLA_KV_IN = MLA_KV_LORA + LANE
MLA_QK = 2 * LANE


def _mla_prep_kernel(*refs, rope):
    if rope:
        qd_ref, ckv_ref, sm_ref, qg_ref, kvg_ref, cos_ref, sin_ref, qo_ref, kvo_ref = refs
    else:
        qd_ref, ckv_ref, sm_ref, qg_ref, kvg_ref, qo_ref, kvo_ref, ckv32_ref = refs
    qo_ref[...] = _rms(qd_ref[...].astype(F32), qg_ref[...]).astype(qo_ref.dtype)
    ckv = _rms(ckv_ref[...].astype(F32), kvg_ref[...])
    kvo_ref[:, :MLA_KV_LORA] = ckv.astype(kvo_ref.dtype)
    sm = sm_ref[...]
    if rope:
        sm = _rope(sm, cos_ref[...], sin_ref[...], MLA_ROPE // 4)
    else:
        ckv32_ref[...] = ckv
    lane = lax.broadcasted_iota(jnp.int32, sm.shape, 1)
    kvo_ref[:, MLA_KV_LORA:] = jnp.where(lane < MLA_ROPE, sm, 0.0).astype(kvo_ref.dtype)


def _mla_prep(proj, small, qg, kvg, tabs, *, l, row0, nrows, seq_len, cols):
    tr = TR_ROWS if tabs is None else min(TR_ROWS, seq_len)
    rb0 = row0 // tr
    qw, cw = cols["mqd"][1], cols["ckv"][1]
    rope = tabs is not None
    in_specs = [pl.BlockSpec((tr, qw), lambda i: (rb0 + i, cols["mqd"][0] // qw)),
                pl.BlockSpec((tr, cw), lambda i: (rb0 + i, cols["ckv"][0] // cw)),
                pl.BlockSpec((tr, LANE), lambda i: (rb0 + i, 0)),
                _lspec(l, (1, qw), lambda i: (0, 0)),
                _lspec(l, (1, cw), lambda i: (0, 0))]
    args = [proj, proj, small, qg, kvg]
    out_shape = [jax.ShapeDtypeStruct((nrows, qw), BF16), jax.ShapeDtypeStruct((nrows, MLA_KV_IN), BF16)]
    out_specs = [pl.BlockSpec((tr, qw), lambda i: (i, 0)), pl.BlockSpec((tr, MLA_KV_IN), lambda i: (i, 0))]
    if rope:
        per = seq_len // tr
        in_specs += [pl.BlockSpec((tr, LANE), lambda i: (i % per, 0))] * 2
        args += list(tabs)
    else:
        out_shape.append(jax.ShapeDtypeStruct((nrows, cw), F32))
        out_specs.append(pl.BlockSpec((tr, cw), lambda i: (i, 0)))
    return pl.pallas_call(
        functools.partial(_mla_prep_kernel, rope=rope),
        out_shape=tuple(out_shape),
        grid=(nrows // tr,),
        in_specs=in_specs,
        out_specs=tuple(out_specs),
        compiler_params=_cparams(("parallel",)),
    )(*args)


def _mla_q_kernel(*refs, rope, scale):
    if rope:
        x_ref, w_ref, cos_ref, sin_ref, o_ref = refs
    else:
        x_ref, w_ref, o_ref = refs
    acc = jnp.dot(x_ref[...], w_ref[...], preferred_element_type=F32) * scale
    if rope:
        o_ref[:, :LANE] = acc[:, :LANE].astype(o_ref.dtype)
        o_ref[:, LANE:] = _rope(acc[:, LANE:], cos_ref[...], sin_ref[...], MLA_ROPE // 4).astype(o_ref.dtype)
    else:
        o_ref[...] = acc.astype(o_ref.dtype)


def _mla_q(qdn, w_q, tabs, *, l, seq_len):
    nrows, k = qdn.shape
    rope = tabs is not None
    tm = min(TR_ROWS, seq_len) if rope else TR_ROWS
    in_specs = [pl.BlockSpec((tm, k), lambda i, h: (i, 0)), _lspec(l, (k, MLA_QK), lambda i, h: (0, h))]
    args = [qdn, w_q]
    if rope:
        per = seq_len // tm
        in_specs += [pl.BlockSpec((tm, LANE), lambda i, h: (i % per, 0))] * 2
        args += list(tabs)
    return pl.pallas_call(
        functools.partial(_mla_q_kernel, rope=rope, scale=(MLA_NOPE + MLA_ROPE) ** -0.5),
        out_shape=jax.ShapeDtypeStruct((nrows, MLA_HEADS * MLA_QK), BF16),
        grid=(nrows // tm, MLA_HEADS),
        in_specs=in_specs,
        out_specs=pl.BlockSpec((tm, MLA_QK), lambda i, h: (i, h)),
        compiler_params=_cparams(("parallel", "arbitrary")),
    )(*args)


def _matmul_kernel(x_ref, w_ref, o_ref):
    o_ref[...] = jnp.dot(x_ref[...], w_ref[...], preferred_element_type=F32).astype(o_ref.dtype)


def _mla_kv(kv_in, w_kv, l):
    nrows, k = kv_in.shape
    n = w_kv.shape[2]
    tm = 512 if nrows % 512 == 0 else 256
    return pl.pallas_call(
        _matmul_kernel,
        out_shape=jax.ShapeDtypeStruct((nrows, n), BF16),
        grid=(nrows // tm,),
        in_specs=[pl.BlockSpec((tm, k), lambda i: (i, 0)), _lspec(l, (k, n), lambda i: (0, 0))],
        out_specs=pl.BlockSpec((tm, n), lambda i: (i, 0)),
        compiler_params=_cparams(("parallel",)),
    )(kv_in, w_kv)


def _conv_kernel(x_ref, w_ref, b_ref, o_ref):
    x = x_ref[0].astype(F32)
    seq = x.shape[0]
    row = lax.broadcasted_iota(jnp.int32, x.shape, 0)
    pad = SSD_CONV // 2
    acc = x * w_ref[pad:pad + 1, :] + b_ref[...]
    for k in range(SSD_CONV):
        d = k - pad
        if d == 0:
            continue
        shifted = pltpu.roll(x, (-d) % seq, 0)
        valid = (row + d >= 0) & (row + d < seq)
        acc = acc + jnp.where(valid, shifted, 0.0) * w_ref[k:k + 1, :]
    o_ref[0] = _silu(acc).astype(o_ref.dtype)


def _ssd_conv(proj, conv_w, conv_b, *, l, seq0, nseq, seq_len, cols):
    npad = proj.shape[1]
    c0, cw = cols["sxbc"]
    tc = 256
    view = proj.reshape(-1, seq_len, npad)
    return pl.pallas_call(
        _conv_kernel,
        out_shape=jax.ShapeDtypeStruct((nseq, seq_len, cw), BF16),
        grid=(nseq, cw // tc),
        in_specs=[pl.BlockSpec((1, seq_len, tc), lambda s, c: (seq0 + s, 0, c0 // tc + c)),
                  _lspec(l, (SSD_CONV, tc), lambda s, c: (0, c)),
                  _lspec(l, (1, tc), lambda s, c: (0, c))],
        out_specs=pl.BlockSpec((1, seq_len, tc), lambda s, c: (s, 0, c)),
        compiler_params=_cparams(("parallel", "parallel")),
    )(view, conv_w, conv_b).reshape(nseq * seq_len, cw)


def _pair_cols(vals, h0):
    q = vals.shape[0]
    lane = lax.broadcasted_iota(jnp.int32, (q, LANE), 1)
    return jnp.where(lane < SSD_HEAD_DIM, vals[:, h0:h0 + 1], vals[:, h0 + 1:h0 + 2])


def _ssd_kernel(*refs, reverse, zero_init, final, d):
    if final:
        (x_ref, b_ref, c_ref, dtc_ref, dtr_ref, biasc_ref, biasr_ref, alogc_ref, alogr_ref, h0_ref,
         yf_ref, z_ref, dskip_ref, ng_ref, y_ref, hout_ref, st_ref) = refs
    else:
        (x_ref, b_ref, c_ref, dtc_ref, dtr_ref, biasc_ref, biasr_ref, alogc_ref, alogr_ref, h0_ref,
         y_ref, hout_ref, st_ref) = refs
    q = SSD_CHUNK
    nh = dtc_ref.shape[1] // 2
    hpg = nh // SSD_GROUPS
    gw = hpg * SSD_HEAD_DIM
    c = pl.program_id(1)

    @pl.when(c == 0)
    def _():
        for g in range(SSD_GROUPS):
            if zero_init:
                st_ref[g] = jnp.zeros(st_ref.shape[1:], F32)
            else:
                st_ref[g] = h0_ref[0, g].T

    lo = d * nh
    dt_c = _softplus(dtc_ref[:, lo:lo + nh] + biasc_ref[:, lo:lo + nh])
    dt_r = _softplus(dtr_ref[lo:lo + nh, :] + biasr_ref[lo:lo + nh, :])
    a_c = dt_c * (-jnp.exp(alogc_ref[:, lo:lo + nh]))
    a_r = dt_r * (-jnp.exp(alogr_ref[lo:lo + nh, :]))
    ii = lax.broadcasted_iota(jnp.int32, (q, q), 0)
    jj = lax.broadcasted_iota(jnp.int32, (q, q), 1)
    causal = (jj >= ii) if reverse else (jj <= ii)
    tri = causal.astype(F32)
    tri_t = ((ii >= jj) if reverse else (ii <= jj)).astype(F32)
    cum_c = jnp.dot(tri, a_c, precision=HIGHEST, preferred_element_type=F32)
    cum_r = jnp.dot(a_r, tri_t, precision=HIGHEST, preferred_element_type=F32)
    tot_c = jnp.sum(a_c, axis=0, keepdims=True)
    e_in = jnp.exp(cum_c)
    w_out = dt_c * jnp.exp(tot_c - cum_c)
    e_tot = jnp.exp(tot_c)

    x = x_ref[...]
    lane = lax.broadcasted_iota(jnp.int32, (q, LANE), 1)
    lane_row = lax.broadcasted_iota(jnp.int32, (1, LANE), 1)
    y_groups = []
    for g in range(SSD_GROUPS):
        bg = b_ref[:, g * SSD_STATE:(g + 1) * SSD_STATE]
        cg = c_ref[:, g * SSD_STATE:(g + 1) * SSD_STATE]
        cb = lax.dot_general(cg, bg, (((1,), (1,)), ((), ())), preferred_element_type=F32)
        st = st_ref[g]
        y_in = jnp.dot(cg, st.astype(BF16), preferred_element_type=F32)
        y_pairs, xs_pairs, dec_pairs = [], [], []
        for pr in range(hpg // 2):
            h0 = g * hpg + 2 * pr
            x_pair = x[:, h0 * SSD_HEAD_DIM:(h0 + 2) * SSD_HEAD_DIM]
            yd = []
            for h in (h0, h0 + 1):
                seg = cum_c[:, h:h + 1] - cum_r[h:h + 1, :]
                decay = jnp.where(causal, jnp.exp(jnp.minimum(seg, 0.0)), 0.0)
                att = (cb * decay * dt_r[h:h + 1, :]).astype(BF16)
                yd.append(jnp.dot(att, x_pair, preferred_element_type=F32))
            y_pair = jnp.where(lane < SSD_HEAD_DIM, yd[0], yd[1])
            y_pair = y_pair + _pair_cols(e_in, h0) * y_in[:, 2 * pr * SSD_HEAD_DIM:(2 * pr + 2) * SSD_HEAD_DIM]
            y_pairs.append(y_pair)
            xs_pairs.append((x_pair.astype(F32) * _pair_cols(w_out, h0)).astype(BF16))
            dec_pairs.append(jnp.where(lane_row < SSD_HEAD_DIM, e_tot[:, h0:h0 + 1], e_tot[:, h0 + 1:h0 + 2]))
        xs_dec = jnp.concatenate(xs_pairs, axis=1)
        upd = lax.dot_general(bg, xs_dec, (((0,), (0,)), ((), ())), preferred_element_type=F32)
        st_ref[g] = st * jnp.concatenate(dec_pairs, axis=1) + upd
        y_groups.append(jnp.concatenate(y_pairs, axis=1))
    y = jnp.concatenate(y_groups, axis=1)

    if final:
        y = y + yf_ref[...] + dskip_ref[...] * x.astype(F32)
        y = y * _silu(z_ref[...].astype(F32))
        y_ref[...] = _rms(y, ng_ref[...]).astype(y_ref.dtype)
    else:
        y_ref[...] = y

    @pl.when(c == pl.num_programs(1) - 1)
    def _():
        for g in range(SSD_GROUPS):
            hout_ref[0, g] = st_ref[g].T


def _ssd_pass(xbc, dt_col, dt_row, dt_bias, a_log, h0, extra, out_buf, state_buf, *, l, d, nseq, seq_len, row0,
              cols):
    q = SSD_CHUNK
    nc = seq_len // q
    nrows = nseq * seq_len
    nh2 = dt_col.shape[1]
    inner = cols["sz"][1]
    gw = inner // SSD_GROUPS
    bw = SSD_GROUPS * SSD_STATE
    rb0 = row0 // q
    reverse = d == 1
    final = extra is not None
    cidx = (lambda c: nc - 1 - c) if reverse else (lambda c: c)
    loc = lambda s, c: s * nc + cidx(c)
    zero_init = h0 is None
    st_block = (1, SSD_GROUPS, gw, SSD_STATE)
    if zero_init:
        h0 = jnp.zeros(st_block, F32)
        h0_spec = pl.BlockSpec(st_block, lambda s, c: (0, 0, 0, 0))
    else:
        h0_spec = pl.BlockSpec((None, None) + st_block, lambda s, c: (l, d, s, 0, 0, 0))
    in_specs = [pl.BlockSpec((q, inner), lambda s, c: (loc(s, c), 0)),
                pl.BlockSpec((q, bw), lambda s, c: (loc(s, c), inner // bw)),
                pl.BlockSpec((q, bw), lambda s, c: (loc(s, c), inner // bw + 1)),
                pl.BlockSpec((q, nh2), lambda s, c: (rb0 + loc(s, c), 0)),
                pl.BlockSpec((nh2, q), lambda s, c: (0, rb0 + loc(s, c))),
                _lspec(l, (1, nh2), lambda s, c: (0, 0)),
                _lspec(l, (nh2, 1), lambda s, c: (0, 0)),
                _lspec(l, (1, nh2), lambda s, c: (0, 0)),
                _lspec(l, (nh2, 1), lambda s, c: (0, 0)),
                h0_spec]
    args = [xbc, xbc, xbc, dt_col, dt_row, dt_bias[0], dt_bias[1], a_log[0], a_log[1], h0]
    kern = functools.partial(_ssd_kernel, reverse=reverse, zero_init=zero_init, final=final, d=d)
    aliases, skipped = {}, []
    if final:
        y_fwd, proj, d_skip, norm_g = extra
        in_specs += [pl.BlockSpec((q, inner), lambda s, c: (loc(s, c), 0)),
                     pl.BlockSpec((q, inner), lambda s, c: (rb0 + loc(s, c), cols["sz"][0] // inner)),
                     _lspec(l, (1, inner), lambda s, c: (0, 0)),
                     _lspec(l, (1, inner), lambda s, c: (0, 0)),
                     pl.BlockSpec(memory_space=pl.ANY)]
        args += [y_fwd, proj, d_skip, norm_g, out_buf]
        aliases[len(args) - 1] = 0
        skipped.append(len(args) - 1)
        y_shape = jax.ShapeDtypeStruct(out_buf.shape, out_buf.dtype)
        y_spec = pl.BlockSpec((q, inner), lambda s, c: (rb0 + loc(s, c), 0))
    else:
        y_shape = jax.ShapeDtypeStruct((nrows, inner), F32)
        y_spec = pl.BlockSpec((q, inner), lambda s, c: (loc(s, c), 0))
    if state_buf is None:
        st_shape = jax.ShapeDtypeStruct((nseq,) + st_block[1:], F32)
        st_spec = pl.BlockSpec(st_block, lambda s, c: (s, 0, 0, 0))
    else:
        in_specs.append(pl.BlockSpec(memory_space=pl.ANY))
        args.append(state_buf)
        aliases[len(args) - 1] = 1
        skipped.append(len(args) - 1)
        st_shape = jax.ShapeDtypeStruct(state_buf.shape, state_buf.dtype)
        st_spec = pl.BlockSpec((1, None, None) + st_block[1:], lambda s, c: (s, l, d, 0, 0, 0))
    return pl.pallas_call(
        functools.partial(_skip_inputs, kern, tuple(skipped)),
        out_shape=(y_shape, st_shape),
        grid=(nseq, nc),
        in_specs=in_specs,
        out_specs=(y_spec, st_spec),
        scratch_shapes=[pltpu.VMEM((SSD_GROUPS, SSD_STATE, gw), F32)],
        input_output_aliases=aliases,
        compiler_params=_cparams(("parallel", "arbitrary")),
    )(*args)


def _s5_kernel(u_ref, wt_ref, ws_ref, wc_ref, a_ref, h0_ref, y_ref, hout_ref, s_scr, hin_scr, *, nseq, nc):
    gb = u_ref.shape[1]
    rows = u_ref.shape[3]
    depth = S5_CHUNK * S5_GROUP_CH
    npair = gb // 2
    w = npair * LANE
    contract_rows = (((0,), (0,)), ((), ()))
    contract_cols = (((1,), (1,)), ((), ()))

    def u_t(g):
        return u_ref[:, g].reshape(depth, rows)

    for p in range(npair):
        s = (lax.dot_general(u_t(2 * p), ws_ref[2 * p], contract_rows, preferred_element_type=F32)
             + lax.dot_general(u_t(2 * p + 1), ws_ref[2 * p + 1], contract_rows, preferred_element_type=F32))
        for comp in range(4):
            s_scr[:, comp * w + p * LANE:comp * w + (p + 1) * LANE] = s[:, comp * LANE:(comp + 1) * LANE]

    af_re, af_im, ab_re, ab_im = a_ref[0], a_ref[1], a_ref[2], a_ref[3]

    def one_sequence(sq, _):
        def step(i, carry):
            hf_re, hf_im, hb_re, hb_im = carry
            rf = sq * nc + i
            rb = sq * nc + nc - 1 - i
            hin_scr[pl.ds(rf, 1), 0:w] = hf_re
            hin_scr[pl.ds(rf, 1), w:2 * w] = hf_im
            hin_scr[pl.ds(rb, 1), 2 * w:3 * w] = hb_re
            hin_scr[pl.ds(rb, 1), 3 * w:4 * w] = hb_im
            sf_re = s_scr[pl.ds(rf, 1), 0:w]
            sf_im = s_scr[pl.ds(rf, 1), w:2 * w]
            sb_re = s_scr[pl.ds(rb, 1), 2 * w:3 * w]
            sb_im = s_scr[pl.ds(rb, 1), 3 * w:4 * w]
            return (af_re * hf_re - af_im * hf_im + sf_re,
                    af_re * hf_im + af_im * hf_re + sf_im,
                    ab_re * hb_re - ab_im * hb_im + sb_re,
                    ab_re * hb_im + ab_im * hb_re + sb_im)

        last = lax.fori_loop(0, nc, step, tuple(h0_ref[sq, comp] for comp in range(4)))
        for comp in range(4):
            hout_ref[sq, comp] = last[comp]
        return 0

    lax.fori_loop(0, nseq, one_sequence, 0)

    for p in range(npair):
        hin = jnp.concatenate([hin_scr[:, comp * w + p * LANE:comp * w + (p + 1) * LANE] for comp in range(4)],
                              axis=1).astype(BF16)
        for e in range(2):
            g = 2 * p + e
            y = (jnp.dot(wt_ref[g], u_t(g), preferred_element_type=F32)
                 + lax.dot_general(wc_ref[g], hin, contract_cols, preferred_element_type=F32))
            y_ref[:, g] = y.reshape(S5_CHUNK, S5_GROUP_CH, rows)


def _s5_scan(u_t, w_toep, w_state, w_carry, a_pow, h0, y_buf, *, l, nseq, seq_len, row0):
    nt, ng, nh, _ = u_t.shape
    nc = seq_len // S5_CHUNK
    rows = nseq * nc
    rblk = row0 // rows
    gb = S5_GB
    kw = nt * nh
    w = (gb // 2) * LANE
    kern = functools.partial(_s5_kernel, nseq=nseq, nc=nc)
    st_block = (nseq, 4, 1, w)
    if h0 is None:
        h0 = jnp.zeros((nseq, 4, 1, ng * S5_STATE), F32)
        h0_spec = pl.BlockSpec(st_block, lambda j: (0, 0, 0, j))
    else:
        h0_spec = _lspec(l, st_block, lambda j: (0, 0, 0, j))
    return pl.pallas_call(
        functools.partial(_skip_inputs, kern, (6,)),
        out_shape=(jax.ShapeDtypeStruct(y_buf.shape, y_buf.dtype),
                   jax.ShapeDtypeStruct((nseq, 4, 1, ng * S5_STATE), F32)),
        grid=(ng // gb,),
        in_specs=[pl.BlockSpec((nt, gb, nh, rows), lambda j: (0, j, 0, rblk)),
                  _lspec(l, (gb, kw, kw), lambda j: (j, 0, 0)),
                  _lspec(l, (gb, kw, 4 * LANE), lambda j: (j, 0, 0)),
                  _lspec(l, (gb, kw, 4 * LANE), lambda j: (j, 0, 0)),
                  _lspec(l, (4, 1, w), lambda j: (0, 0, j)),
                  h0_spec,
                  pl.BlockSpec(memory_space=pl.ANY)],
        out_specs=(pl.BlockSpec((nt, gb, nh, rows), lambda j: (0, j, 0, rblk)),
                   pl.BlockSpec((nseq, 4, 1, w), lambda j: (0, 0, 0, j))),
        scratch_shapes=[pltpu.VMEM((rows, 4 * w), F32), pltpu.VMEM((rows, 4 * w), F32)],
        input_output_aliases={6: 0},
        compiler_params=_cparams(("parallel",)),
    )(u_t, w_toep, w_state, w_carry, a_pow, h0, y_buf)


def _s5_weights(lam_re, lam_im, log_step, b_re, b_im, c_re, c_im):
    t = S5_CHUNK
    ng, ns, nh = b_re.shape
    step = jnp.exp(log_step)[..., None]
    lr, li = lam_re * step, lam_im * step
    n = jnp.arange(t + 1, dtype=F32)[:, None, None, None]
    mag = jnp.exp(lr[None] * n)
    pw_re, pw_im = mag * jnp.cos(li[None] * n), mag * jnp.sin(li[None] * n)
    a_re, a_im = pw_re[1], pw_im[1]
    den = lam_re * lam_re + lam_im * lam_im
    k_re = ((a_re - 1.0) * lam_re + a_im * lam_im) / den
    k_im = (a_im * lam_re - (a_re - 1.0) * lam_im) / den
    bt_re, bt_im = b_re.transpose(0, 2, 1), b_im.transpose(0, 2, 1)
    w_re = k_re[:, :, None, :] * bt_re[None] - k_im[:, :, None, :] * bt_im[None]
    w_im = k_re[:, :, None, :] * bt_im[None] + k_im[:, :, None, :] * bt_re[None]

    def times_pow(idx, d, x_re, x_im):
        p_re, p_im = pw_re[idx, d][:, :, None, :], pw_im[idx, d][:, :, None, :]
        return p_re * x_re[None] - p_im * x_im[None], p_re * x_im[None] + p_im * x_re[None]

    ti = jnp.arange(t)
    kern = []
    for d in range(2):
        aw_re, aw_im = times_pow(ti, d, w_re[d], w_im[d])
        kern.append(jnp.einsum("gop,tgip->tgoi", c_re, aw_re, precision=HIGHEST)
                    - jnp.einsum("gop,tgip->tgoi", c_im, aw_im, precision=HIGHEST))
    lag = ti[:, None] - ti[None, :]
    sel_f = (lag[:, :, None] == ti[None, None, :]).astype(F32)
    sel_b = (-lag[:, :, None] == ti[None, None, :]).astype(F32)
    toep_t = (jnp.einsum("abt,tgoi->gaobi", sel_f, kern[0], precision=HIGHEST)
              + jnp.einsum("abt,tgoi->gaobi", sel_b, kern[1], precision=HIGHEST)).reshape(ng, t * nh, t * nh)

    def pack(comps):
        x = jnp.stack(comps, axis=0).transpose(2, 1, 3, 0, 4)
        side = jax.nn.one_hot(jnp.arange(ng) % 2, 2, dtype=F32)
        x = x[:, :, :, :, None, :] * side[:, None, None, None, :, None]
        return x.reshape(ng, t * nh, 4 * 2 * ns)

    w_state = pack(times_pow(t - 1 - ti, 0, w_re[0], w_im[0]) + times_pow(ti, 1, w_re[1], w_im[1]))
    cf_re, cf_im = times_pow(ti + 1, 0, c_re, c_im)
    cb_re, cb_im = times_pow(t - ti, 1, c_re, c_im)
    carry_t = pack((cf_re, -cf_im, cb_re, -cb_im))
    a_pow = jnp.stack([pw_re[t, 0], pw_im[t, 0], pw_re[t, 1], pw_im[t, 1]], axis=0).reshape(4, 1, ng * ns)
    return toep_t.astype(BF16), w_state.astype(BF16), carry_t.astype(BF16), a_pow


def _glu_kernel(y_ref, u_ref, d_ref, wv_ref, wg_ref, o_ref, vs_ref):
    @pl.when(pl.program_id(1) == 0)
    def _():
        v = y_ref[...] + d_ref[...] * u_ref[...].astype(F32)
        v = 0.5 * v * (1.0 + jnp.tanh(math.sqrt(2.0 / math.pi) * (v + 0.044715 * (v * v * v))))
        vs_ref[...] = v.astype(BF16)

    v = vs_ref[...]
    a = jnp.dot(v, wv_ref[...], preferred_element_type=F32)
    b = jnp.dot(v, wg_ref[...], preferred_element_type=F32)
    o_ref[...] = (a * jax.nn.sigmoid(b)).astype(o_ref.dtype)


def _s5_glu(y, proj, d_skip, w_glu, l, dims, cols):
    m, wd = y.shape
    tm, tn = min(TM, dims["lat_len"]), 512
    nj = wd // tn
    return pl.pallas_call(
        _glu_kernel,
        out_shape=jax.ShapeDtypeStruct((m, wd), BF16),
        grid=(m // tm, nj),
        in_specs=[pl.BlockSpec((tm, wd), lambda i, j: (i, 0)),
                  pl.BlockSpec((tm, wd), lambda i, j: (i, cols["s5u"][0] // wd)),
                  _lspec(l, (1, wd), lambda i, j: (0, 0)),
                  _lspec(l, (wd, tn), lambda i, j: (0, j)),
                  _lspec(l, (wd, tn), lambda i, j: (0, j + nj))],
        out_specs=pl.BlockSpec((tm, tn), lambda i, j: (i, j)),
        scratch_shapes=[pltpu.VMEM((tm, wd), BF16)],
        compiler_params=_cparams(("parallel", "arbitrary")),
    )(y, proj, d_skip, w_glu, w_glu)


def _columns(d_model, branch_w):
    kvw = GQA_KV_HEADS * GQA_HEAD_DIM
    xbc = branch_w + 2 * SSD_GROUPS * SSD_STATE
    order = (("gate", N_BRANCH * d_model), ("gq", branch_w), ("sz", branch_w), ("s5u", branch_w),
             ("sxbc", xbc), ("mqd", MLA_Q_LORA), ("ckv", MLA_KV_LORA), ("gk", kvw), ("gv", kvw))
    cols, off = {}, 0
    for name, width in order:
        cols[name] = (off, width)
        off += width
    return cols, off


def _prep_w_in(w_in, d_model, branch_w, n_dt):
    kvw = GQA_KV_HEADS * GQA_HEAD_DIM
    xbc = branch_w + 2 * SSD_GROUPS * SSD_STATE
    splits = (N_BRANCH * d_model, branch_w, kvw, kvw, branch_w, xbc, n_dt, MLA_Q_LORA, MLA_KV_LORA + MLA_ROPE, branch_w)
    bounds, acc = [], 0
    for wd in splits[:-1]:
        acc += wd
        bounds.append(acc)
    gate, gq, gk, gv, sz, sxbc, sdt, mqd, mkvd, s5u = jnp.split(w_in, bounds, axis=-1)
    ckv, kpe = mkvd[..., :MLA_KV_LORA], mkvd[..., MLA_KV_LORA:]
    main = jnp.concatenate([gate, gq, sz, s5u, sxbc, mqd, ckv, gk, gv], axis=-1).astype(BF16)
    pad = jnp.zeros(w_in.shape[:-1] + (LANE - MLA_ROPE - n_dt,), w_in.dtype)
    small = jnp.concatenate([kpe, sdt, pad], axis=-1).astype(BF16)
    return main, small


def _prep_mla(w_uq, w_ukv):
    depth = w_uq.shape[0]
    qk = MLA_NOPE + MLA_ROPE
    wq = w_uq.reshape(depth, MLA_Q_LORA, MLA_HEADS, qk)
    wq = jnp.pad(wq, ((0, 0), (0, 0), (0, 0), (0, MLA_QK - qk))).reshape(depth, MLA_Q_LORA, MLA_HEADS * MLA_QK)
    wkv = w_ukv.reshape(depth, MLA_KV_LORA, MLA_HEADS, MLA_NOPE + LANE)
    k_nope, v = wkv[..., :MLA_NOPE], wkv[..., MLA_NOPE:]
    k_top = jnp.pad(k_nope, ((0, 0), (0, 0), (0, 0), (0, MLA_QK - MLA_NOPE)))
    eye = jnp.eye(LANE, MLA_QK, k=MLA_NOPE, dtype=w_ukv.dtype) * (jnp.arange(LANE) < MLA_ROPE)[:, None]
    k_bot = jnp.broadcast_to(eye[None, :, None, :], (depth, LANE, MLA_HEADS, MLA_QK))
    k_all = jnp.concatenate([k_top, k_bot], axis=1).reshape(depth, MLA_KV_IN, MLA_HEADS * MLA_QK)
    v_all = jnp.pad(v, ((0, 0), (0, LANE), (0, 0), (0, 0))).reshape(depth, MLA_KV_IN, MLA_HEADS * LANE)
    return wq.astype(BF16), jnp.concatenate([k_all, v_all], axis=-1).astype(BF16)


def kernel(x_prompt, x_sample, cache_gqa_k, cache_gqa_v, cache_mla_ckv, cache_mla_kpe, state_ssd, state_s5, c, c_ctx, norm1_g, norm2_g, w_mod, b_mod, w_in, gqa_qn_g, gqa_kn_g, ssd_conv_w, ssd_conv_b, ssd_a_log, ssd_dt_bias, ssd_d, ssd_norm_g, mla_qn_g, mla_w_uq, mla_kvn_g, mla_w_ukv, s5_lam_re, s5_lam_im, s5_log_step, s5_b_re, s5_b_im, s5_c_re, s5_c_im, s5_d, s5_w_glu, w_branch, w_out, w_ffn_in, w_ffn_out, final_g):
    nb_ctx, len_ctx, d_model = x_prompt.shape
    nb_lat, len_lat, _ = x_sample.shape
    depth = w_in.shape[0]
    past = cache_gqa_k.shape[2]
    branch_w = w_branch.shape[2]
    n_heads = ssd_d.shape[1]
    ctx_rows, lat_rows = nb_ctx * len_ctx, nb_lat * len_lat
    m = ctx_rows + lat_rows
    dims = {"ctx_rows": ctx_rows, "lat_len": len_lat}
    cols, _ = _columns(d_model, branch_w)
    kvw = GQA_KV_HEADS * GQA_HEAD_DIM
    n_s5 = branch_w // S5_GROUP_CH
    gw = branch_w // SSD_GROUPS

    row = lambda a: a.reshape(depth, 1, -1)
    col = lambda a: a.reshape(depth, -1, 1)
    w_main, w_small = _prep_w_in(w_in, d_model, branch_w, 2 * n_heads)
    w_q, w_kv = _prep_mla(mla_w_uq, mla_w_ukv)
    s5_toep, s5_state, s5_carry, s5_apow = jax.vmap(_s5_weights)(
        s5_lam_re, s5_lam_im, s5_log_step, s5_b_re, s5_b_im, s5_c_re, s5_c_im)
    tabs_a = _rope_tables(len_lat, GQA_HEAD_DIM)
    tabs_c = _rope_tables(len_lat, MLA_ROPE)
    n_mod = -(-(nb_lat + 1) // 8) * 8
    cvec = jnp.concatenate([c_ctx[None], c, jnp.zeros((n_mod - nb_lat - 1, d_model), F32)], axis=0)
    mod = _modulation(cvec, w_mod, b_mod).reshape(depth, n_mod, 6, 1, d_model)
    norm1_g, norm2_g = row(norm1_g), row(norm2_g)
    gqa_qn_g, gqa_kn_g, mla_qn_g, mla_kvn_g = row(gqa_qn_g), row(gqa_kn_g), row(mla_qn_g), row(mla_kvn_g)
    conv_b, ssd_norm_g, s5_d = row(ssd_conv_b), row(ssd_norm_g), row(s5_d)
    d_skip = row(jnp.repeat(ssd_d, SSD_HEAD_DIM, axis=-1))
    dt_bias, a_log = (row(ssd_dt_bias), col(ssd_dt_bias)), (row(ssd_a_log), col(ssd_a_log))
    w_glu, w_branch, w_out = s5_w_glu.astype(BF16), w_branch.astype(BF16), w_out.astype(BF16)
    w_ffn_in, w_ffn_out = w_ffn_in.astype(BF16), w_ffn_out.astype(BF16)
    cache_k = cache_gqa_k.reshape(nb_lat, depth, past, kvw).astype(BF16)
    cache_v = cache_gqa_v.reshape(nb_lat, depth, past, kvw).astype(BF16)
    zpad = jnp.zeros(cache_mla_kpe.shape[:-1] + (LANE - MLA_ROPE,), F32)
    cache_kv = jnp.concatenate([cache_mla_ckv, cache_mla_kpe, zpad], axis=-1).astype(BF16)
    h0_ssd = jnp.moveaxis(state_ssd, (1, 2), (0, 1)).reshape(depth, 2, nb_lat, SSD_GROUPS, gw, SSD_STATE)
    h0_s5 = jnp.moveaxis(state_s5, 1, 0).reshape(depth, nb_lat, 4, 1, n_s5 * S5_STATE)
    ssd_new = jnp.zeros((nb_ctx, depth, 2, SSD_GROUPS, gw, SSD_STATE), F32)
    new_buf = lambda: jnp.zeros((m, branch_w), BF16)
    gqa = dict(n_kv=GQA_KV_HEADS, n_rep=cols["gq"][1] // kvw, dk=GQA_HEAD_DIM, dv=GQA_HEAD_DIM)
    mla = dict(n_kv=MLA_HEADS, n_rep=1, dk=MLA_QK, dv=LANE, k_off=0, v_off=MLA_HEADS * MLA_QK)
    nchunk = m // S5_CHUNK

    x = jnp.concatenate([x_prompt.reshape(ctx_rows, d_model), x_sample.reshape(lat_rows, d_model)], axis=0)
    new = []
    for l in range(depth):
        proj, small = _in_proj(x, norm1_g, mod, w_main, w_small, l, dims)

        qn, kn, k_own = _gqa_prep(proj, gqa_qn_g, gqa_kn_g, None, l=l, row0=0, nrows=ctx_rows, seq_len=len_ctx,
                                  cols=cols)
        o_a = _attention(qn, kn.reshape(nb_ctx, len_ctx, kvw), proj.reshape(-1, len_ctx, proj.shape[1]), new_buf(),
                         nseq=nb_ctx, seq_len=len_ctx, tq=len_ctx, kv_per_step=GQA_KV_HEADS, k_off=0,
                         v_off=cols["gv"][0], row0=0, **gqa)
        v_own = proj[:ctx_rows, cols["gv"][0]:cols["gv"][0] + kvw]
        qn, kn = _gqa_prep(proj, gqa_qn_g, gqa_kn_g, tabs_a, l=l, row0=ctx_rows, nrows=lat_rows, seq_len=len_lat,
                           cols=cols)
        k_lat = jnp.concatenate([kn.reshape(nb_lat, len_lat, kvw), cache_k[:, l]], axis=1)
        v_lat = jnp.concatenate([proj[ctx_rows:, cols["gv"][0]:cols["gv"][0] + kvw].reshape(nb_lat, len_lat, kvw),
                                 cache_v[:, l]], axis=1)
        o_a = _attention(qn, k_lat, v_lat, o_a, nseq=nb_lat, seq_len=len_lat, tq=min(TQ_LAT, len_lat),
                         kv_per_step=1, k_off=0, v_off=0, row0=ctx_rows, **gqa)

        qdn, kv_in, ckv_own = _mla_prep(proj, small, mla_qn_g, mla_kvn_g, None, l=l, row0=0, nrows=ctx_rows,
                                        seq_len=len_ctx, cols=cols)
        kpe_own = small[:ctx_rows, :MLA_ROPE]
        kv = _mla_kv(kv_in, w_kv, l).reshape(nb_ctx, len_ctx, -1)
        o_c = _attention(_mla_q(qdn, w_q, None, l=l, seq_len=len_ctx), kv, kv, new_buf(),
                         nseq=nb_ctx, seq_len=len_ctx, tq=len_ctx, kv_per_step=MLA_HEADS, row0=0, **mla)
        qdn, kv_in = _mla_prep(proj, small, mla_qn_g, mla_kvn_g, tabs_c, l=l, row0=ctx_rows, nrows=lat_rows,
                               seq_len=len_lat, cols=cols)
        kv_in = jnp.concatenate([kv_in.reshape(nb_lat, len_lat, MLA_KV_IN), cache_kv[:, l]], axis=1)
        kv = _mla_kv(kv_in.reshape(-1, MLA_KV_IN), w_kv, l).reshape(nb_lat, len_lat + past, -1)
        o_c = _attention(_mla_q(qdn, w_q, tabs_c, l=l, seq_len=len_lat), kv, kv, o_c,
                         nseq=nb_lat, seq_len=len_lat, tq=min(TQ_LAT, len_lat), kv_per_step=MLA_LAT_HEADS_PER_STEP,
                         row0=ctx_rows, **mla)

        dt_col = small[:, MLA_ROPE:MLA_ROPE + 2 * n_heads]
        dt_row = dt_col.T
        o_b = new_buf()
        for nseq, seq_len, row0, h0 in ((nb_ctx, len_ctx, 0, None), (nb_lat, len_lat, ctx_rows, h0_ssd)):
            xbc = _ssd_conv(proj, ssd_conv_w, conv_b, l=l, seq0=row0 // seq_len, nseq=nseq, seq_len=seq_len,
                            cols=cols)
            common = dict(l=l, nseq=nseq, seq_len=seq_len, row0=row0, cols=cols)
            keep = h0 is None
            y_f, st = _ssd_pass(xbc, dt_col, dt_row, dt_bias, a_log, h0, None, None, ssd_new if keep else None,
                                d=0, **common)
            ssd_new = st if keep else ssd_new
            o_b, st = _ssd_pass(xbc, dt_col, dt_row, dt_bias, a_log, h0, (y_f, proj, d_skip, ssd_norm_g), o_b,
                                ssd_new if keep else None, d=1, **common)
            ssd_new = st if keep else ssd_new

        u = proj[:, cols["s5u"][0]:cols["s5u"][0] + branch_w]
        u_t = u.reshape(nchunk, S5_CHUNK * branch_w).T.reshape(S5_CHUNK, n_s5, S5_GROUP_CH, nchunk)
        s5_args = (u_t, s5_toep, s5_state, s5_carry, s5_apow)
        y_t, s5_new = _s5_scan(*s5_args, None, jnp.zeros(u_t.shape, F32), l=l, nseq=nb_ctx, seq_len=len_ctx, row0=0)
        y_t, _ = _s5_scan(*s5_args, h0_s5, y_t, l=l, nseq=nb_lat, seq_len=len_lat, row0=ctx_rows // S5_CHUNK)
        y = y_t.reshape(S5_CHUNK * branch_w, nchunk).T.reshape(m, branch_w)
        o_d = _s5_glu(y, proj, s5_d, w_glu, l, dims, cols)

        mixed = _branch_mix([o_a, o_b, o_c, o_d], proj, w_branch, cols["gate"][0], l, dims)
        x = _resid_proj(mixed, w_out, x, mod, 2, 2 * TM, l, dims)
        hidden = _ffn_in(x, norm2_g, mod, w_ffn_in, l, dims)
        x = _resid_proj(hidden, w_ffn_out, x, mod, 5, TM, l, dims)
        new.append((k_own.reshape(nb_ctx, len_ctx, GQA_KV_HEADS, GQA_HEAD_DIM),
                    v_own.astype(F32).reshape(nb_ctx, len_ctx, GQA_KV_HEADS, GQA_HEAD_DIM),
                    ckv_own.reshape(nb_ctx, len_ctx, MLA_KV_LORA),
                    kpe_own.reshape(nb_ctx, len_ctx, MLA_ROPE),
                    s5_new.reshape(nb_ctx, 2, 2, n_s5, S5_STATE)))

    stacked = [jnp.stack([layer_out[i] for layer_out in new], axis=1) for i in range(5)]
    return (_final_norm(x, final_g, 0, ctx_rows).reshape(x_prompt.shape),
            _final_norm(x, final_g, ctx_rows, lat_rows).reshape(x_sample.shape),
            *stacked[:4],
            ssd_new.reshape(nb_ctx, depth, 2, n_heads, SSD_HEAD_DIM, SSD_STATE),
            stacked[4])
```

```python
import functools
import math

import jax
import jax.numpy as jnp
from jax import lax
from jax.experimental import pallas as pl
from jax.experimental.pallas import tpu as pltpu

F32 = jnp.float32
BF16 = jnp.bfloat16
HIGHEST = lax.Precision.HIGHEST

GRID_W = 64
N_BRANCH = 4
GQA_HEAD_DIM = 128
GQA_KV_HEADS = 2
SSD_HEAD_DIM = 64
SSD_GROUPS = 2
SSD_STATE = 128
SSD_CONV = 5
SSD_CHUNK = 128
MLA_HEADS = 8
MLA_NOPE = 128
MLA_ROPE = 64
MLA_Q_LORA = 512
MLA_KV_LORA = 256
S5_GROUP_CH = 16
S5_STATE = 64
S5_CHUNK = 16
ROPE_THETA = 10000.0
NORM_EPS = 1e-6
LANE = 128
VMEM_LIMIT = 56 * 1024 * 1024

TM = 1024
TN_IN = 1280
TN_FFN = 512
TN_OUT = 512
TR_ROWS = 1024
TQ_LAT = 512
MLA_LAT_HEADS_PER_STEP = 4
S5_GB = 8


def _cparams(sem):
    return pltpu.CompilerParams(dimension_semantics=sem, vmem_limit_bytes=VMEM_LIMIT)


def _row_group(i, tm, ctx_rows, lat_len):
    nct = ctx_rows // tm
    per = lat_len // tm
    return jnp.where(i < nct, 0, 1 + (i - nct) // per)


def _lspec(l, block, index_map):
    return pl.BlockSpec((None,) + tuple(block), lambda *ids: (l,) + tuple(index_map(*ids)))


def _silu(x):
    return x * jax.nn.sigmoid(x)


def _softplus(x):
    return jnp.maximum(x, 0.0) + jnp.log(1.0 + jnp.exp(-jnp.abs(x)))


def _rms(x, g):
    ms = jnp.mean(x * x, axis=-1, keepdims=True)
    return x * lax.rsqrt(ms + NORM_EPS) * g


def _mod_kernel(c_ref, w_ref, b_ref, o_ref):
    c = c_ref[...]
    s = _silu(c).astype(BF16)
    o_ref[0] = jnp.dot(s, w_ref[0].astype(BF16), preferred_element_type=F32) + b_ref[0]


def _modulation(cvec, w_mod, b_mod):
    depth, d, n = w_mod.shape
    tn = 1024
    return pl.pallas_call(
        _mod_kernel,
        out_shape=jax.ShapeDtypeStruct((depth, 8, n), F32),
        grid=(depth, n // tn),
        in_specs=[pl.BlockSpec((8, d), lambda l, j: (0, 0)),
                  pl.BlockSpec((1, d, tn), lambda l, j: (l, 0, j)),
                  pl.BlockSpec((1, 1, tn), lambda l, j: (l, 0, j))],
        out_specs=pl.BlockSpec((1, 8, tn), lambda l, j: (l, 0, j)),
        compiler_params=_cparams(("parallel", "parallel")),
    )(cvec, w_mod, b_mod.reshape(depth, 1, n))


NORM_ROWS = 256


def _norm_mod(x_ref, g_ref, sc_ref, sh_ref, hs_ref):
    def body(r, _):
        rows = pl.ds(pl.multiple_of(r * NORM_ROWS, NORM_ROWS), NORM_ROWS)
        y = _rms(x_ref[rows, :], g_ref[...])
        hs_ref[rows, :] = (y * (1.0 + sc_ref[...]) + sh_ref[...]).astype(BF16)
        return 0

    lax.fori_loop(0, x_ref.shape[0] // NORM_ROWS, body, 0)


def _in_proj_kernel(x_ref, g_ref, sc_ref, sh_ref, w_ref, ws_ref, o_ref, os_ref, hs_ref):
    @pl.when(pl.program_id(1) == 0)
    def _():
        _norm_mod(x_ref, g_ref, sc_ref, sh_ref, hs_ref)
        os_ref[...] = jnp.dot(hs_ref[...], ws_ref[...], preferred_element_type=F32)

    o_ref[...] = jnp.dot(hs_ref[...], w_ref[...], preferred_element_type=F32).astype(o_ref.dtype)


def _in_proj(x, norm_g, mod, w_main, w_small, l, dims):
    m, d = x.shape
    npad = w_main.shape[2]
    tm, tn = min(TM, dims["lat_len"]), TN_IN
    grp = lambda i: _row_group(i, tm, dims["ctx_rows"], dims["lat_len"])
    return pl.pallas_call(
        _in_proj_kernel,
        out_shape=(jax.ShapeDtypeStruct((m, npad), BF16), jax.ShapeDtypeStruct((m, LANE), F32)),
        grid=(m // tm, npad // tn),
        in_specs=[pl.BlockSpec((tm, d), lambda i, j: (i, 0)),
                  _lspec(l, (1, d), lambda i, j: (0, 0)),
                  _lspec(l, (None, None, 1, d), lambda i, j: (grp(i), 1, 0, 0)),
                  _lspec(l, (None, None, 1, d), lambda i, j: (grp(i), 0, 0, 0)),
                  _lspec(l, (d, tn), lambda i, j: (0, j)),
                  _lspec(l, (d, LANE), lambda i, j: (0, 0))],
        out_specs=(pl.BlockSpec((tm, tn), lambda i, j: (i, j)),
                   pl.BlockSpec((tm, LANE), lambda i, j: (i, 0))),
        scratch_shapes=[pltpu.VMEM((tm, d), BF16)],
        compiler_params=_cparams(("parallel", "arbitrary")),
    )(x, norm_g, mod, mod, w_main, w_small)


def _ffn_in_kernel(x_ref, g_ref, sc_ref, sh_ref, wg_ref, wu_ref, o_ref, hs_ref):
    @pl.when(pl.program_id(1) == 0)
    def _():
        _norm_mod(x_ref, g_ref, sc_ref, sh_ref, hs_ref)

    h = hs_ref[...]
    a = jnp.dot(h, wg_ref[...], preferred_element_type=F32)
    b = jnp.dot(h, wu_ref[...], preferred_element_type=F32)
    o_ref[...] = (_silu(a) * b).astype(o_ref.dtype)


def _ffn_in(x, norm_g, mod, w, l, dims):
    m, d = x.shape
    hid = w.shape[2] // 2
    tm, tn = min(TM, dims["lat_len"]), TN_FFN
    nj = hid // tn
    grp = lambda i: _row_group(i, tm, dims["ctx_rows"], dims["lat_len"])
    return pl.pallas_call(
        _ffn_in_kernel,
        out_shape=jax.ShapeDtypeStruct((m, hid), BF16),
        grid=(m // tm, nj),
        in_specs=[pl.BlockSpec((tm, d), lambda i, j: (i, 0)),
                  _lspec(l, (1, d), lambda i, j: (0, 0)),
                  _lspec(l, (None, None, 1, d), lambda i, j: (grp(i), 4, 0, 0)),
                  _lspec(l, (None, None, 1, d), lambda i, j: (grp(i), 3, 0, 0)),
                  _lspec(l, (d, tn), lambda i, j: (0, j)),
                  _lspec(l, (d, tn), lambda i, j: (0, j + nj))],
        out_specs=pl.BlockSpec((tm, tn), lambda i, j: (i, j)),
        scratch_shapes=[pltpu.VMEM((tm, d), BF16)],
        compiler_params=_cparams(("parallel", "arbitrary")),
    )(x, norm_g, mod, mod, w, w)


def _resid_kernel(x_ref, w_ref, r_ref, g_ref, o_ref):
    o_ref[...] = r_ref[...] + g_ref[...] * jnp.dot(x_ref[...], w_ref[...], preferred_element_type=F32)


def _resid_proj(xin, w, resid, mod, mod_idx, tm, l, dims):
    m, k = xin.shape
    n = w.shape[2]
    tm, tn = min(tm, dims["lat_len"]), TN_OUT
    grp = lambda i: _row_group(i, tm, dims["ctx_rows"], dims["lat_len"])
    return pl.pallas_call(
        _resid_kernel,
        out_shape=jax.ShapeDtypeStruct((m, n), F32),
        grid=(m // tm, n // tn),
        in_specs=[pl.BlockSpec((tm, k), lambda i, j: (i, 0)),
                  _lspec(l, (k, tn), lambda i, j: (0, j)),
                  pl.BlockSpec((tm, tn), lambda i, j: (i, j)),
                  _lspec(l, (None, None, 1, tn), lambda i, j: (grp(i), mod_idx, 0, j))],
        out_specs=pl.BlockSpec((tm, tn), lambda i, j: (i, j)),
        compiler_params=_cparams(("parallel", "arbitrary")),
    )(xin, w, resid, mod)


def _mix_kernel(oa_ref, ob_ref, oc_ref, od_ref, ga_ref, gb_ref, gc_ref, gd_ref, w_ref, o_ref):
    acc = None
    for n, (o_n, g_n) in enumerate(((oa_ref, ga_ref), (ob_ref, gb_ref), (oc_ref, gc_ref), (od_ref, gd_ref))):
        proj = jnp.dot(o_n[...], w_ref[n], preferred_element_type=F32)
        term = jax.nn.sigmoid(g_n[...].astype(F32)) * proj
        acc = term if acc is None else acc + term
    o_ref[...] = acc.astype(o_ref.dtype)


def _branch_mix(branches, proj_all, w_branch, col_gate, l, dims):
    m, bw = branches[0].shape
    d = w_branch.shape[3]
    tm, tn = min(TM, dims["lat_len"]), TN_OUT
    gate_specs = [pl.BlockSpec((tm, tn), functools.partial(lambda i, j, n: (i, (col_gate + n * d) // tn + j), n=n))
                  for n in range(N_BRANCH)]
    return pl.pallas_call(
        _mix_kernel,
        out_shape=jax.ShapeDtypeStruct((m, d), BF16),
        grid=(m // tm, d // tn),
        in_specs=[pl.BlockSpec((tm, bw), lambda i, j: (i, 0))] * N_BRANCH + gate_specs
                 + [_lspec(l, (N_BRANCH, bw, tn), lambda i, j: (0, 0, j))],
        out_specs=pl.BlockSpec((tm, tn), lambda i, j: (i, j)),
        compiler_params=_cparams(("parallel", "arbitrary")),
    )(*branches, proj_all, proj_all, proj_all, proj_all, w_branch)


def _final_norm_kernel(x_ref, g_ref, o_ref):
    o_ref[...] = _rms(x_ref[...], g_ref[...])


def _final_norm(x, g, row0, nrows):
    d = x.shape[1]
    tm = 512
    rb0 = row0 // tm
    return pl.pallas_call(
        _final_norm_kernel,
        out_shape=jax.ShapeDtypeStruct((nrows, d), F32),
        grid=(nrows // tm,),
        in_specs=[pl.BlockSpec((tm, d), lambda i: (rb0 + i, 0)), pl.BlockSpec((1, d), lambda i: (0, 0))],
        out_specs=pl.BlockSpec((tm, d), lambda i: (i, 0)),
        compiler_params=_cparams(("parallel",)),
    )(x, g.reshape(1, d))


def _rope_tables(seq_len, dim):
    nf = dim // 4
    t = jnp.arange(seq_len)
    row = (t // GRID_W).astype(F32)
    col = (t % GRID_W).astype(F32)
    inv = ROPE_THETA ** (-jnp.arange(nf, dtype=F32) / nf)
    ang_r, ang_c = row[:, None] * inv, col[:, None] * inv
    cos = jnp.concatenate([jnp.cos(ang_r), jnp.cos(ang_r), jnp.cos(ang_c), jnp.cos(ang_c)], axis=1)
    sin = jnp.concatenate([-jnp.sin(ang_r), jnp.sin(ang_r), -jnp.sin(ang_c), jnp.sin(ang_c)], axis=1)
    pad = LANE - dim
    if pad:
        cos = jnp.concatenate([cos, jnp.ones((seq_len, pad), F32)], axis=1)
        sin = jnp.concatenate([sin, jnp.zeros((seq_len, pad), F32)], axis=1)
    return cos, sin


def _rope(x, cos, sin, nf):
    lane = lax.broadcasted_iota(jnp.int32, x.shape, 1)
    swapped = jnp.where((lane % (2 * nf)) < nf, pltpu.roll(x, LANE - nf, 1), pltpu.roll(x, nf, 1))
    return x * cos + swapped * sin


def _attn_kernel(q_ref, k_ref, v_ref, o_ref, *, kv_per_step, n_rep, dk, dv):
    for a in range(kv_per_step):
        k = k_ref[0, :, a * dk:(a + 1) * dk]
        v = v_ref[0, :, a * dv:(a + 1) * dv]
        for r in range(n_rep):
            h = a * n_rep + r
            q = q_ref[:, h * dk:(h + 1) * dk]
            s = lax.dot_general(q, k, (((1,), (1,)), ((), ())), preferred_element_type=F32)
            p = jnp.exp(s - jnp.max(s, axis=-1, keepdims=True))
            l = jnp.sum(p, axis=-1, keepdims=True)
            o = jnp.dot(p.astype(BF16), v, preferred_element_type=F32)
            o_ref[:, h * dv:(h + 1) * dv] = (o / l).astype(o_ref.dtype)


def _skip_inputs(kern, positions, *refs):
    return kern(*(r for i, r in enumerate(refs) if i not in positions))


def _in_place(kern, args, in_specs, out_bufs):
    aliases, skipped = {}, []
    for out_idx, buf in out_bufs.items():
        if buf is None or isinstance(buf, jax.ShapeDtypeStruct):
            continue
        in_specs.append(pl.BlockSpec(memory_space=pl.ANY))
        args.append(buf)
        aliases[len(args) - 1] = out_idx
        skipped.append(len(args) - 1)
    if skipped:
        kern = functools.partial(_skip_inputs, kern, tuple(skipped))
    return kern, aliases


def _attention(q, k, v, out_buf, *, nseq, seq_len, tq, n_kv, kv_per_step, n_rep, dk, dv, k_off, v_off, row0):
    lk = k.shape[1]
    nq = seq_len // tq
    rb0 = row0 // tq
    kw, vw = kv_per_step * dk, kv_per_step * dv
    assert k_off % kw == 0 and v_off % vw == 0 and n_kv % kv_per_step == 0
    kb0, vb0 = k_off // kw, v_off // vw
    kern = functools.partial(_attn_kernel, kv_per_step=kv_per_step, n_rep=n_rep, dk=dk, dv=dv)
    args, in_specs = [q, k, v], [pl.BlockSpec((tq, n_rep * kw), lambda b, g, i: (b * nq + i, g)),
                                 pl.BlockSpec((1, lk, kw), lambda b, g, i: (b, 0, kb0 + g)),
                                 pl.BlockSpec((1, lk, vw), lambda b, g, i: (b, 0, vb0 + g))]
    kern, aliases = _in_place(kern, args, in_specs, {0: out_buf})
    return pl.pallas_call(
        kern,
        out_shape=jax.ShapeDtypeStruct(out_buf.shape, out_buf.dtype),
        grid=(nseq, n_kv // kv_per_step, nq),
        in_specs=in_specs,
        out_specs=pl.BlockSpec((tq, n_rep * vw), lambda b, g, i: (rb0 + b * nq + i, g)),
        input_output_aliases=aliases,
        compiler_params=_cparams(("parallel", "parallel", "arbitrary")),
    )(*args)


def _gqa_prep_kernel(*refs, rope, scale, n_q, n_kv):
    if rope:
        q_ref, k_ref, qg_ref, kg_ref, cos_ref, sin_ref, qo_ref, ko_ref = refs
    else:
        q_ref, k_ref, qg_ref, kg_ref, qo_ref, ko_ref, k32_ref = refs
    hd = GQA_HEAD_DIM
    for h in range(n_q):
        y = _rms(q_ref[:, h * hd:(h + 1) * hd].astype(F32), qg_ref[...])
        if rope:
            y = _rope(y, cos_ref[...], sin_ref[...], hd // 4)
        qo_ref[:, h * hd:(h + 1) * hd] = (y * scale).astype(qo_ref.dtype)
    for h in range(n_kv):
        y = _rms(k_ref[:, h * hd:(h + 1) * hd].astype(F32), kg_ref[...])
        if rope:
            y = _rope(y, cos_ref[...], sin_ref[...], hd // 4)
        else:
            k32_ref[:, h * hd:(h + 1) * hd] = y
        ko_ref[:, h * hd:(h + 1) * hd] = y.astype(ko_ref.dtype)


def _gqa_prep(proj, qg, kg, tabs, *, l, row0, nrows, seq_len, cols):
    tr = TR_ROWS if tabs is None else min(TR_ROWS, seq_len)
    rb0 = row0 // tr
    qw, kw = cols["gq"][1], cols["gk"][1]
    n_q, n_kv = qw // GQA_HEAD_DIM, kw // GQA_HEAD_DIM
    rope = tabs is not None
    in_specs = [pl.BlockSpec((tr, qw), lambda i: (rb0 + i, cols["gq"][0] // qw)),
                pl.BlockSpec((tr, kw), lambda i: (rb0 + i, cols["gk"][0] // kw)),
                _lspec(l, (1, GQA_HEAD_DIM), lambda i: (0, 0)),
                _lspec(l, (1, GQA_HEAD_DIM), lambda i: (0, 0))]
    args = [proj, proj, qg, kg]
    out_shape = [jax.ShapeDtypeStruct((nrows, qw), BF16), jax.ShapeDtypeStruct((nrows, kw), BF16)]
    out_specs = [pl.BlockSpec((tr, qw), lambda i: (i, 0)), pl.BlockSpec((tr, kw), lambda i: (i, 0))]
    if rope:
        per = seq_len // tr
        in_specs += [pl.BlockSpec((tr, LANE), lambda i: (i % per, 0))] * 2
        args += list(tabs)
    else:
        out_shape.append(jax.ShapeDtypeStruct((nrows, kw), F32))
        out_specs.append(pl.BlockSpec((tr, kw), lambda i: (i, 0)))
    return pl.pallas_call(
        functools.partial(_gqa_prep_kernel, rope=rope, scale=GQA_HEAD_DIM ** -0.5, n_q=n_q, n_kv=n_kv),
        out_shape=tuple(out_shape),
        grid=(nrows // tr,),
        in_specs=in_specs,
        out_specs=tuple(out_specs),
        compiler_params=_cparams(("parallel",)),
    )(*args)


MLA_KV_IN = MLA_KV_LORA + LANE
MLA_QK = 2 * LANE


def _mla_prep_kernel(*refs, rope):
    if rope:
        qd_ref, ckv_ref, sm_ref, qg_ref, kvg_ref, cos_ref, sin_ref, qo_ref, kvo_ref = refs
    else:
        qd_ref, ckv_ref, sm_ref, qg_ref, kvg_ref, qo_ref, kvo_ref, ckv32_ref = refs
    qo_ref[...] = _rms(qd_ref[...].astype(F32), qg_ref[...]).astype(qo_ref.dtype)
    ckv = _rms(ckv_ref[...].astype(F32), kvg_ref[...])
    kvo_ref[:, :MLA_KV_LORA] = ckv.astype(kvo_ref.dtype)
    sm = sm_ref[...]
    if rope:
        sm = _rope(sm, cos_ref[...], sin_ref[...], MLA_ROPE // 4)
    else:
        ckv32_ref[...] = ckv
    lane = lax.broadcasted_iota(jnp.int32, sm.shape, 1)
    kvo_ref[:, MLA_KV_LORA:] = jnp.where(lane < MLA_ROPE, sm, 0.0).astype(kvo_ref.dtype)


def _mla_prep(proj, small, qg, kvg, tabs, *, l, row0, nrows, seq_len, cols):
    tr = TR_ROWS if tabs is None else min(TR_ROWS, seq_len)
    rb0 = row0 // tr
    qw, cw = cols["mqd"][1], cols["ckv"][1]
    rope = tabs is not None
    in_specs = [pl.BlockSpec((tr, qw), lambda i: (rb0 + i, cols["mqd"][0] // qw)),
                pl.BlockSpec((tr, cw), lambda i: (rb0 + i, cols["ckv"][0] // cw)),
                pl.BlockSpec((tr, LANE), lambda i: (rb0 + i, 0)),
                _lspec(l, (1, qw), lambda i: (0, 0)),
                _lspec(l, (1, cw), lambda i: (0, 0))]
    args = [proj, proj, small, qg, kvg]
    out_shape = [jax.ShapeDtypeStruct((nrows, qw), BF16), jax.ShapeDtypeStruct((nrows, MLA_KV_IN), BF16)]
    out_specs = [pl.BlockSpec((tr, qw), lambda i: (i, 0)), pl.BlockSpec((tr, MLA_KV_IN), lambda i: (i, 0))]
    if rope:
        per = seq_len // tr
        in_specs += [pl.BlockSpec((tr, LANE), lambda i: (i % per, 0))] * 2
        args += list(tabs)
    else:
        out_shape.append(jax.ShapeDtypeStruct((nrows, cw), F32))
        out_specs.append(pl.BlockSpec((tr, cw), lambda i: (i, 0)))
    return pl.pallas_call(
        functools.partial(_mla_prep_kernel, rope=rope),
        out_shape=tuple(out_shape),
        grid=(nrows // tr,),
        in_specs=in_specs,
        out_specs=tuple(out_specs),
        compiler_params=_cparams(("parallel",)),
    )(*args)


def _mla_q_kernel(*refs, rope, scale):
    if rope:
        x_ref, w_ref, cos_ref, sin_ref, o_ref = refs
    else:
        x_ref, w_ref, o_ref = refs
    acc = jnp.dot(x_ref[...], w_ref[...], preferred_element_type=F32) * scale
    if rope:
        o_ref[:, :LANE] = acc[:, :LANE].astype(o_ref.dtype)
        o_ref[:, LANE:] = _rope(acc[:, LANE:], cos_ref[...], sin_ref[...], MLA_ROPE // 4).astype(o_ref.dtype)
    else:
        o_ref[...] = acc.astype(o_ref.dtype)


def _mla_q(qdn, w_q, tabs, *, l, seq_len):
    nrows, k = qdn.shape
    rope = tabs is not None
    tm = min(TR_ROWS, seq_len) if rope else TR_ROWS
    in_specs = [pl.BlockSpec((tm, k), lambda i, h: (i, 0)), _lspec(l, (k, MLA_QK), lambda i, h: (0, h))]
    args = [qdn, w_q]
    if rope:
        per = seq_len // tm
        in_specs += [pl.BlockSpec((tm, LANE), lambda i, h: (i % per, 0))] * 2
        args += list(tabs)
    return pl.pallas_call(
        functools.partial(_mla_q_kernel, rope=rope, scale=(MLA_NOPE + MLA_ROPE) ** -0.5),
        out_shape=jax.ShapeDtypeStruct((nrows, MLA_HEADS * MLA_QK), BF16),
        grid=(nrows // tm, MLA_HEADS),
        in_specs=in_specs,
        out_specs=pl.BlockSpec((tm, MLA_QK), lambda i, h: (i, h)),
        compiler_params=_cparams(("parallel", "arbitrary")),
    )(*args)


def _matmul_kernel(x_ref, w_ref, o_ref):
    o_ref[...] = jnp.dot(x_ref[...], w_ref[...], preferred_element_type=F32).astype(o_ref.dtype)


def _mla_kv(kv_in, w_kv, l):
    nrows, k = kv_in.shape
    n = w_kv.shape[2]
    tm = 512 if nrows % 512 == 0 else 256
    return pl.pallas_call(
        _matmul_kernel,
        out_shape=jax.ShapeDtypeStruct((nrows, n), BF16),
        grid=(nrows // tm,),
        in_specs=[pl.BlockSpec((tm, k), lambda i: (i, 0)), _lspec(l, (k, n), lambda i: (0, 0))],
        out_specs=pl.BlockSpec((tm, n), lambda i: (i, 0)),
        compiler_params=_cparams(("parallel",)),
    )(kv_in, w_kv)


def _conv_kernel(x_ref, w_ref, b_ref, o_ref):
    x = x_ref[0].astype(F32)
    seq = x.shape[0]
    edge = 8
    row = lax.broadcasted_iota(jnp.int32, (edge, x.shape[1]), 0)
    pad = SSD_CONV // 2
    acc = x * w_ref[pad:pad + 1, :] + b_ref[...]
    for k in range(SSD_CONV):
        d = k - pad
        if d == 0:
            continue
        if d > 0:
            src = jnp.concatenate([jnp.where(row < d, 0.0, x[:edge]), x[edge:]], axis=0)
        else:
            src = jnp.concatenate([x[:seq - edge], jnp.where(row >= edge + d, 0.0, x[seq - edge:])], axis=0)
        acc = acc + pltpu.roll(src, (-d) % seq, 0) * w_ref[k:k + 1, :]
    o_ref[0] = _silu(acc).astype(o_ref.dtype)


def _ssd_conv(proj, conv_w, conv_b, *, l, seq0, nseq, seq_len, cols):
    npad = proj.shape[1]
    c0, cw = cols["sxbc"]
    tc = 256
    view = proj.reshape(-1, seq_len, npad)
    return pl.pallas_call(
        _conv_kernel,
        out_shape=jax.ShapeDtypeStruct((nseq, seq_len, cw), BF16),
        grid=(nseq, cw // tc),
        in_specs=[pl.BlockSpec((1, seq_len, tc), lambda s, c: (seq0 + s, 0, c0 // tc + c)),
                  _lspec(l, (SSD_CONV, tc), lambda s, c: (0, c)),
                  _lspec(l, (1, tc), lambda s, c: (0, c))],
        out_specs=pl.BlockSpec((1, seq_len, tc), lambda s, c: (s, 0, c)),
        compiler_params=_cparams(("parallel", "parallel")),
    )(view, conv_w, conv_b).reshape(nseq * seq_len, cw)


def _pair_cols(vals, h0):
    q = vals.shape[0]
    lane = lax.broadcasted_iota(jnp.int32, (q, LANE), 1)
    return jnp.where(lane < SSD_HEAD_DIM, vals[:, h0:h0 + 1], vals[:, h0 + 1:h0 + 2])


def _ssd_kernel(*refs, reverse, zero_init, final, d):
    if final:
        (x_ref, b_ref, c_ref, dtc_ref, dtr_ref, biasc_ref, biasr_ref, alogc_ref, alogr_ref, h0_ref,
         yf_ref, z_ref, dskip_ref, ng_ref, y_ref, hout_ref, st_ref) = refs
    else:
        (x_ref, b_ref, c_ref, dtc_ref, dtr_ref, biasc_ref, biasr_ref, alogc_ref, alogr_ref, h0_ref,
         y_ref, hout_ref, st_ref) = refs
    q = SSD_CHUNK
    nh = dtc_ref.shape[1] // 2
    hpg = nh // SSD_GROUPS
    gw = hpg * SSD_HEAD_DIM
    c = pl.program_id(1)

    @pl.when(c == 0)
    def _():
        for g in range(SSD_GROUPS):
            if zero_init:
                st_ref[g] = jnp.zeros(st_ref.shape[1:], F32)
            else:
                st_ref[g] = h0_ref[0, g].T

    lo = d * nh
    dt_c = _softplus(dtc_ref[:, lo:lo + nh] + biasc_ref[:, lo:lo + nh])
    dt_r = _softplus(dtr_ref[lo:lo + nh, :] + biasr_ref[lo:lo + nh, :])
    a_c = dt_c * (-jnp.exp(alogc_ref[:, lo:lo + nh]))
    a_r = dt_r * (-jnp.exp(alogr_ref[lo:lo + nh, :]))
    ii = lax.broadcasted_iota(jnp.int32, (q, q), 0)
    jj = lax.broadcasted_iota(jnp.int32, (q, q), 1)
    causal = (jj >= ii) if reverse else (jj <= ii)
    tri = causal.astype(F32)
    tri_t = ((ii >= jj) if reverse else (ii <= jj)).astype(F32)
    cum_c = jnp.dot(tri, a_c, precision=HIGHEST, preferred_element_type=F32)
    cum_r = jnp.dot(a_r, tri_t, precision=HIGHEST, preferred_element_type=F32)
    tot_c = jnp.sum(a_c, axis=0, keepdims=True)
    e_in = jnp.exp(cum_c)
    w_out = dt_c * jnp.exp(tot_c - cum_c)
    e_tot = jnp.exp(tot_c)

    x = x_ref[...]
    lane = lax.broadcasted_iota(jnp.int32, (q, LANE), 1)
    lane_row = lax.broadcasted_iota(jnp.int32, (1, LANE), 1)
    y_groups = []
    for g in range(SSD_GROUPS):
        bg = b_ref[:, g * SSD_STATE:(g + 1) * SSD_STATE]
        cg = c_ref[:, g * SSD_STATE:(g + 1) * SSD_STATE]
        cb = lax.dot_general(cg, bg, (((1,), (1,)), ((), ())), preferred_element_type=F32)
        st = st_ref[g]
        y_in = jnp.dot(cg, st.astype(BF16), preferred_element_type=F32)
        y_pairs, xs_pairs, dec_pairs = [], [], []
        for pr in range(hpg // 2):
            h0 = g * hpg + 2 * pr
            x_pair = x[:, h0 * SSD_HEAD_DIM:(h0 + 2) * SSD_HEAD_DIM]
            yd = []
            for h in (h0, h0 + 1):
                seg = cum_c[:, h:h + 1] - cum_r[h:h + 1, :]
                decay = jnp.where(causal, jnp.exp(jnp.minimum(seg, 0.0)), 0.0)
                att = (cb * decay * dt_r[h:h + 1, :]).astype(BF16)
                yd.append(jnp.dot(att, x_pair, preferred_element_type=F32))
            y_pair = jnp.where(lane < SSD_HEAD_DIM, yd[0], yd[1])
            y_pair = y_pair + _pair_cols(e_in, h0) * y_in[:, 2 * pr * SSD_HEAD_DIM:(2 * pr + 2) * SSD_HEAD_DIM]
            y_pairs.append(y_pair)
            xs_pairs.append((x_pair.astype(F32) * _pair_cols(w_out, h0)).astype(BF16))
            dec_pairs.append(jnp.where(lane_row < SSD_HEAD_DIM, e_tot[:, h0:h0 + 1], e_tot[:, h0 + 1:h0 + 2]))
        xs_dec = jnp.concatenate(xs_pairs, axis=1)
        upd = lax.dot_general(bg, xs_dec, (((0,), (0,)), ((), ())), preferred_element_type=F32)
        st_ref[g] = st * jnp.concatenate(dec_pairs, axis=1) + upd
        y_groups.append(jnp.concatenate(y_pairs, axis=1))
    y = jnp.concatenate(y_groups, axis=1)

    if final:
        y = y + yf_ref[...] + dskip_ref[...] * x.astype(F32)
        y = y * _silu(z_ref[...].astype(F32))
        y_ref[...] = _rms(y, ng_ref[...]).astype(y_ref.dtype)
    else:
        y_ref[...] = y

    @pl.when(c == pl.num_programs(1) - 1)
    def _():
        for g in range(SSD_GROUPS):
            hout_ref[0, g] = st_ref[g].T


def _ssd_pass(xbc, dt_col, dt_row, dt_bias, a_log, h0, extra, out_buf, state_buf, *, l, d, nseq, seq_len, row0,
              cols):
    q = SSD_CHUNK
    nc = seq_len // q
    nrows = nseq * seq_len
    nh2 = dt_col.shape[1]
    inner = cols["sz"][1]
    gw = inner // SSD_GROUPS
    bw = SSD_GROUPS * SSD_STATE
    rb0 = row0 // q
    reverse = d == 1
    final = extra is not None
    cidx = (lambda c: nc - 1 - c) if reverse else (lambda c: c)
    loc = lambda s, c: s * nc + cidx(c)
    zero_init = h0 is None
    st_block = (1, SSD_GROUPS, gw, SSD_STATE)
    if zero_init:
        h0 = jnp.zeros(st_block, F32)
        h0_spec = pl.BlockSpec(st_block, lambda s, c: (0, 0, 0, 0))
    else:
        h0_spec = pl.BlockSpec((None, None) + st_block, lambda s, c: (l, d, s, 0, 0, 0))
    in_specs = [pl.BlockSpec((q, inner), lambda s, c: (loc(s, c), 0)),
                pl.BlockSpec((q, bw), lambda s, c: (loc(s, c), inner // bw)),
                pl.BlockSpec((q, bw), lambda s, c: (loc(s, c), inner // bw + 1)),
                pl.BlockSpec((q, nh2), lambda s, c: (rb0 + loc(s, c), 0)),
                pl.BlockSpec((nh2, q), lambda s, c: (0, rb0 + loc(s, c))),
                _lspec(l, (1, nh2), lambda s, c: (0, 0)),
                _lspec(l, (nh2, 1), lambda s, c: (0, 0)),
                _lspec(l, (1, nh2), lambda s, c: (0, 0)),
                _lspec(l, (nh2, 1), lambda s, c: (0, 0)),
                h0_spec]
    args = [xbc, xbc, xbc, dt_col, dt_row, dt_bias[0], dt_bias[1], a_log[0], a_log[1], h0]
    kern = functools.partial(_ssd_kernel, reverse=reverse, zero_init=zero_init, final=final, d=d)
    if final:
        y_fwd, proj, d_skip, norm_g = extra
        in_specs += [pl.BlockSpec((q, inner), lambda s, c: (loc(s, c), 0)),
                     pl.BlockSpec((q, inner), lambda s, c: (rb0 + loc(s, c), cols["sz"][0] // inner)),
                     _lspec(l, (1, inner), lambda s, c: (0, 0)),
                     _lspec(l, (1, inner), lambda s, c: (0, 0))]
        args += [y_fwd, proj, d_skip, norm_g]
        y_shape = jax.ShapeDtypeStruct(out_buf.shape, out_buf.dtype)
        y_spec = pl.BlockSpec((q, inner), lambda s, c: (rb0 + loc(s, c), 0))
    else:
        y_shape = jax.ShapeDtypeStruct((nrows, inner), F32)
        y_spec = pl.BlockSpec((q, inner), lambda s, c: (loc(s, c), 0))
    if state_buf is None:
        st_shape = jax.ShapeDtypeStruct((nseq,) + st_block[1:], F32)
        st_spec = pl.BlockSpec(st_block, lambda s, c: (s, 0, 0, 0))
    else:
        st_shape = jax.ShapeDtypeStruct(state_buf.shape, state_buf.dtype)
        st_spec = pl.BlockSpec((1, None, None) + st_block[1:], lambda s, c: (s, l, d, 0, 0, 0))
    kern, aliases = _in_place(kern, args, in_specs, {0: out_buf if final else None, 1: state_buf})
    return pl.pallas_call(
        kern,
        out_shape=(y_shape, st_shape),
        grid=(nseq, nc),
        in_specs=in_specs,
        out_specs=(y_spec, st_spec),
        scratch_shapes=[pltpu.VMEM((SSD_GROUPS, SSD_STATE, gw), F32)],
        input_output_aliases=aliases,
        compiler_params=_cparams(("parallel", "arbitrary")),
    )(*args)


def _s5_kernel(u_ref, wt_ref, ws_ref, wc_ref, a_ref, h0_ref, y_ref, hout_ref, s_scr, hin_scr, *, nseq, nc):
    gb = u_ref.shape[1]
    rows = u_ref.shape[3]
    depth = S5_CHUNK * S5_GROUP_CH
    npair = gb // 2
    w = npair * LANE
    contract_rows = (((0,), (0,)), ((), ()))
    contract_cols = (((1,), (1,)), ((), ()))

    def u_t(g):
        return u_ref[:, g].reshape(depth, rows)

    for p in range(npair):
        s = (lax.dot_general(u_t(2 * p), ws_ref[2 * p], contract_rows, preferred_element_type=F32)
             + lax.dot_general(u_t(2 * p + 1), ws_ref[2 * p + 1], contract_rows, preferred_element_type=F32))
        for comp in range(4):
            s_scr[:, comp * w + p * LANE:comp * w + (p + 1) * LANE] = s[:, comp * LANE:(comp + 1) * LANE]

    af_re, af_im, ab_re, ab_im = a_ref[0], a_ref[1], a_ref[2], a_ref[3]

    def one_sequence(sq, _):
        def step(i, carry):
            hf_re, hf_im, hb_re, hb_im = carry
            rf = sq * nc + i
            rb = sq * nc + nc - 1 - i
            hin_scr[pl.ds(rf, 1), 0:w] = hf_re
            hin_scr[pl.ds(rf, 1), w:2 * w] = hf_im
            hin_scr[pl.ds(rb, 1), 2 * w:3 * w] = hb_re
            hin_scr[pl.ds(rb, 1), 3 * w:4 * w] = hb_im
            sf_re = s_scr[pl.ds(rf, 1), 0:w]
            sf_im = s_scr[pl.ds(rf, 1), w:2 * w]
            sb_re = s_scr[pl.ds(rb, 1), 2 * w:3 * w]
            sb_im = s_scr[pl.ds(rb, 1), 3 * w:4 * w]
            return (af_re * hf_re - af_im * hf_im + sf_re,
                    af_re * hf_im + af_im * hf_re + sf_im,
                    ab_re * hb_re - ab_im * hb_im + sb_re,
                    ab_re * hb_im + ab_im * hb_re + sb_im)

        last = lax.fori_loop(0, nc, step, tuple(h0_ref[sq, comp] for comp in range(4)))
        for comp in range(4):
            hout_ref[sq, comp] = last[comp]
        return 0

    lax.fori_loop(0, nseq, one_sequence, 0)

    for p in range(npair):
        hin = jnp.concatenate([hin_scr[:, comp * w + p * LANE:comp * w + (p + 1) * LANE] for comp in range(4)],
                              axis=1).astype(BF16)
        for e in range(2):
            g = 2 * p + e
            y = (jnp.dot(wt_ref[g], u_t(g), preferred_element_type=F32)
                 + lax.dot_general(wc_ref[g], hin, contract_cols, preferred_element_type=F32))
            y_ref[:, g] = y.reshape(S5_CHUNK, S5_GROUP_CH, rows)


def _s5_scan(u_t, w_toep, w_state, w_carry, a_pow, h0, y_buf, *, l, nseq, seq_len, row0):
    nt, ng, nh, _ = u_t.shape
    nc = seq_len // S5_CHUNK
    rows = nseq * nc
    rblk = row0 // rows
    gb = S5_GB
    kw = nt * nh
    w = (gb // 2) * LANE
    kern = functools.partial(_s5_kernel, nseq=nseq, nc=nc)
    st_block = (nseq, 4, 1, w)
    if h0 is None:
        h0 = jnp.zeros((nseq, 4, 1, ng * S5_STATE), F32)
        h0_spec = pl.BlockSpec(st_block, lambda j: (0, 0, 0, j))
    else:
        h0_spec = _lspec(l, st_block, lambda j: (0, 0, 0, j))
    args = [u_t, w_toep, w_state, w_carry, a_pow, h0]
    in_specs = [pl.BlockSpec((nt, gb, nh, rows), lambda j: (0, j, 0, rblk)),
                _lspec(l, (gb, kw, kw), lambda j: (j, 0, 0)),
                _lspec(l, (gb, kw, 4 * LANE), lambda j: (j, 0, 0)),
                _lspec(l, (gb, kw, 4 * LANE), lambda j: (j, 0, 0)),
                _lspec(l, (4, 1, w), lambda j: (0, 0, j)),
                h0_spec]
    kern, aliases = _in_place(kern, args, in_specs, {0: y_buf})
    return pl.pallas_call(
        kern,
        out_shape=(jax.ShapeDtypeStruct(y_buf.shape, y_buf.dtype),
                   jax.ShapeDtypeStruct((nseq, 4, 1, ng * S5_STATE), F32)),
        grid=(ng // gb,),
        in_specs=in_specs,
        out_specs=(pl.BlockSpec((nt, gb, nh, rows), lambda j: (0, j, 0, rblk)),
                   pl.BlockSpec((nseq, 4, 1, w), lambda j: (0, 0, 0, j))),
        scratch_shapes=[pltpu.VMEM((rows, 4 * w), F32), pltpu.VMEM((rows, 4 * w), F32)],
        input_output_aliases=aliases,
        compiler_params=_cparams(("parallel",)),
    )(*args)


def _s5_weights(lam_re, lam_im, log_step, b_re, b_im, c_re, c_im):
    t = S5_CHUNK
    ng, ns, nh = b_re.shape
    step = jnp.exp(log_step)[..., None]
    lr, li = lam_re * step, lam_im * step
    n = jnp.arange(t + 1, dtype=F32)[:, None, None, None]
    mag = jnp.exp(lr[None] * n)
    pw_re, pw_im = mag * jnp.cos(li[None] * n), mag * jnp.sin(li[None] * n)
    a_re, a_im = pw_re[1], pw_im[1]
    den = lam_re * lam_re + lam_im * lam_im
    k_re = ((a_re - 1.0) * lam_re + a_im * lam_im) / den
    k_im = (a_im * lam_re - (a_re - 1.0) * lam_im) / den
    bt_re, bt_im = b_re.transpose(0, 2, 1), b_im.transpose(0, 2, 1)
    w_re = k_re[:, :, None, :] * bt_re[None] - k_im[:, :, None, :] * bt_im[None]
    w_im = k_re[:, :, None, :] * bt_im[None] + k_im[:, :, None, :] * bt_re[None]

    def times_pow(idx, d, x_re, x_im):
        p_re, p_im = pw_re[idx, d][:, :, None, :], pw_im[idx, d][:, :, None, :]
        return p_re * x_re[None] - p_im * x_im[None], p_re * x_im[None] + p_im * x_re[None]

    ti = jnp.arange(t)
    kern = []
    for d in range(2):
        aw_re, aw_im = times_pow(ti, d, w_re[d], w_im[d])
        kern.append(jnp.einsum("gop,tgip->tgoi", c_re, aw_re, precision=HIGHEST)
                    - jnp.einsum("gop,tgip->tgoi", c_im, aw_im, precision=HIGHEST))
    k_all = jnp.concatenate([kern[0][:0:-1], (kern[0][0] + kern[1][0])[None], kern[1][1:]], axis=0)
    k_flat = k_all.transpose(1, 2, 0, 3).reshape(ng, nh, (2 * t - 1) * nh)
    toep_t = jnp.stack([k_flat[:, :, (t - 1 - to) * nh:(2 * t - 1 - to) * nh] for to in range(t)], axis=1)
    toep_t = toep_t.reshape(ng, t * nh, t * nh)

    side = jax.nn.one_hot(jnp.arange(ng) % 2, 2, dtype=F32)
    width = 4 * 2 * ns

    def lane_pow(p, idx_f, idx_b):
        x = jnp.stack([p[idx_f, 0], p[idx_f, 0], p[idx_b, 1], p[idx_b, 1]], axis=2)
        x = x[:, :, :, None, :] * side[None, :, None, :, None]
        return x.reshape(t, ng, 1, width).transpose(1, 0, 2, 3)

    def lane_coef(comps):
        x = jnp.stack(comps, axis=2)[:, :, :, None, :]
        return jnp.broadcast_to(x, (ng, nh, 4, 2, ns)).reshape(ng, 1, nh, width)

    def state_matrix(idx_f, idx_b, re_coef, im_coef):
        mat = (lane_pow(pw_re, idx_f, idx_b) * lane_coef(re_coef)
               + lane_pow(pw_im, idx_f, idx_b) * lane_coef(im_coef))
        return mat.reshape(ng, t * nh, width).astype(BF16)

    w_state = state_matrix(t - 1 - ti, ti, (w_re[0], w_im[0], w_re[1], w_im[1]),
                           (-w_im[0], w_re[0], -w_im[1], w_re[1]))
    carry_t = state_matrix(ti + 1, t - ti, (c_re, -c_im, c_re, -c_im), (-c_im, -c_re, -c_im, -c_re))
    a_pow = jnp.stack([pw_re[t, 0], pw_im[t, 0], pw_re[t, 1], pw_im[t, 1]], axis=0).reshape(4, 1, ng * ns)
    return toep_t.astype(BF16), w_state, carry_t, a_pow


def _glu_kernel(y_ref, u_ref, d_ref, wv_ref, wg_ref, o_ref, vs_ref):
    @pl.when(pl.program_id(1) == 0)
    def _():
        v = y_ref[...] + d_ref[...] * u_ref[...].astype(F32)
        v = 0.5 * v * (1.0 + jnp.tanh(math.sqrt(2.0 / math.pi) * (v + 0.044715 * (v * v * v))))
        vs_ref[...] = v.astype(BF16)

    v = vs_ref[...]
    a = jnp.dot(v, wv_ref[...], preferred_element_type=F32)
    b = jnp.dot(v, wg_ref[...], preferred_element_type=F32)
    o_ref[...] = (a * jax.nn.sigmoid(b)).astype(o_ref.dtype)


def _s5_glu(y, proj, d_skip, w_glu, l, dims, cols):
    m, wd = y.shape
    tm, tn = min(TM, dims["lat_len"]), 512
    nj = wd // tn
    return pl.pallas_call(
        _glu_kernel,
        out_shape=jax.ShapeDtypeStruct((m, wd), BF16),
        grid=(m // tm, nj),
        in_specs=[pl.BlockSpec((tm, wd), lambda i, j: (i, 0)),
                  pl.BlockSpec((tm, wd), lambda i, j: (i, cols["s5u"][0] // wd)),
                  _lspec(l, (1, wd), lambda i, j: (0, 0)),
                  _lspec(l, (wd, tn), lambda i, j: (0, j)),
                  _lspec(l, (wd, tn), lambda i, j: (0, j + nj))],
        out_specs=pl.BlockSpec((tm, tn), lambda i, j: (i, j)),
        scratch_shapes=[pltpu.VMEM((tm, wd), BF16)],
        compiler_params=_cparams(("parallel", "arbitrary")),
    )(y, proj, d_skip, w_glu, w_glu)


def _columns(d_model, branch_w):
    kvw = GQA_KV_HEADS * GQA_HEAD_DIM
    xbc = branch_w + 2 * SSD_GROUPS * SSD_STATE
    order = (("gate", N_BRANCH * d_model), ("gq", branch_w), ("sz", branch_w), ("s5u", branch_w),
             ("sxbc", xbc), ("mqd", MLA_Q_LORA), ("ckv", MLA_KV_LORA), ("gk", kvw), ("gv", kvw))
    cols, off = {}, 0
    for name, width in order:
        cols[name] = (off, width)
        off += width
    return cols, off


def _prep_w_in(w_in, d_model, branch_w, n_dt):
    kvw = GQA_KV_HEADS * GQA_HEAD_DIM
    xbc = branch_w + 2 * SSD_GROUPS * SSD_STATE
    splits = (N_BRANCH * d_model, branch_w, kvw, kvw, branch_w, xbc, n_dt, MLA_Q_LORA, MLA_KV_LORA + MLA_ROPE, branch_w)
    bounds, acc = [], 0
    for wd in splits[:-1]:
        acc += wd
        bounds.append(acc)
    gate, gq, gk, gv, sz, sxbc, sdt, mqd, mkvd, s5u = jnp.split(w_in, bounds, axis=-1)
    ckv, kpe = mkvd[..., :MLA_KV_LORA], mkvd[..., MLA_KV_LORA:]
    main = jnp.concatenate([gate, gq, sz, s5u, sxbc, mqd, ckv, gk, gv], axis=-1).astype(BF16)
    pad = jnp.zeros(w_in.shape[:-1] + (LANE - MLA_ROPE - n_dt,), w_in.dtype)
    small = jnp.concatenate([kpe, sdt, pad], axis=-1).astype(BF16)
    return main, small


def _prep_mla(w_uq, w_ukv):
    depth = w_uq.shape[0]
    qk = MLA_NOPE + MLA_ROPE
    wq = w_uq.reshape(depth, MLA_Q_LORA, MLA_HEADS, qk)
    wq = jnp.pad(wq, ((0, 0), (0, 0), (0, 0), (0, MLA_QK - qk))).reshape(depth, MLA_Q_LORA, MLA_HEADS * MLA_QK)
    wkv = w_ukv.reshape(depth, MLA_KV_LORA, MLA_HEADS, MLA_NOPE + LANE)
    k_nope, v = wkv[..., :MLA_NOPE], wkv[..., MLA_NOPE:]
    k_top = jnp.pad(k_nope, ((0, 0), (0, 0), (0, 0), (0, MLA_QK - MLA_NOPE)))
    eye = jnp.eye(LANE, MLA_QK, k=MLA_NOPE, dtype=w_ukv.dtype) * (jnp.arange(LANE) < MLA_ROPE)[:, None]
    k_bot = jnp.broadcast_to(eye[None, :, None, :], (depth, LANE, MLA_HEADS, MLA_QK))
    k_all = jnp.concatenate([k_top, k_bot], axis=1).reshape(depth, MLA_KV_IN, MLA_HEADS * MLA_QK)
    v_all = jnp.pad(v, ((0, 0), (0, LANE), (0, 0), (0, 0))).reshape(depth, MLA_KV_IN, MLA_HEADS * LANE)
    return wq.astype(BF16), jnp.concatenate([k_all, v_all], axis=-1).astype(BF16)


def kernel(x_prompt, x_sample, cache_gqa_k, cache_gqa_v, cache_mla_ckv, cache_mla_kpe, state_ssd, state_s5, c, c_ctx, norm1_g, norm2_g, w_mod, b_mod, w_in, gqa_qn_g, gqa_kn_g, ssd_conv_w, ssd_conv_b, ssd_a_log, ssd_dt_bias, ssd_d, ssd_norm_g, mla_qn_g, mla_w_uq, mla_kvn_g, mla_w_ukv, s5_lam_re, s5_lam_im, s5_log_step, s5_b_re, s5_b_im, s5_c_re, s5_c_im, s5_d, s5_w_glu, w_branch, w_out, w_ffn_in, w_ffn_out, final_g):
    nb_ctx, len_ctx, d_model = x_prompt.shape
    nb_lat, len_lat, _ = x_sample.shape
    depth = w_in.shape[0]
    past = cache_gqa_k.shape[2]
    branch_w = w_branch.shape[2]
    n_heads = ssd_d.shape[1]
    ctx_rows, lat_rows = nb_ctx * len_ctx, nb_lat * len_lat
    m = ctx_rows + lat_rows
    dims = {"ctx_rows": ctx_rows, "lat_len": len_lat}
    cols, _ = _columns(d_model, branch_w)
    kvw = GQA_KV_HEADS * GQA_HEAD_DIM
    n_s5 = branch_w // S5_GROUP_CH
    gw = branch_w // SSD_GROUPS

    row = lambda a: a.reshape(depth, 1, -1)
    col = lambda a: a.reshape(depth, -1, 1)
    w_main, w_small = _prep_w_in(w_in, d_model, branch_w, 2 * n_heads)
    w_q, w_kv = _prep_mla(mla_w_uq, mla_w_ukv)
    s5_toep, s5_state, s5_carry, s5_apow = jax.vmap(_s5_weights)(
        s5_lam_re, s5_lam_im, s5_log_step, s5_b_re, s5_b_im, s5_c_re, s5_c_im)
    tabs_a = _rope_tables(len_lat, GQA_HEAD_DIM)
    tabs_c = _rope_tables(len_lat, MLA_ROPE)
    n_mod = -(-(nb_lat + 1) // 8) * 8
    cvec = jnp.concatenate([c_ctx[None], c, jnp.zeros((n_mod - nb_lat - 1, d_model), F32)], axis=0)
    mod = _modulation(cvec, w_mod, b_mod).reshape(depth, n_mod, 6, 1, d_model)
    norm1_g, norm2_g = row(norm1_g), row(norm2_g)
    gqa_qn_g, gqa_kn_g, mla_qn_g, mla_kvn_g = row(gqa_qn_g), row(gqa_kn_g), row(mla_qn_g), row(mla_kvn_g)
    conv_b, ssd_norm_g, s5_d = row(ssd_conv_b), row(ssd_norm_g), row(s5_d)
    d_skip = row(jnp.repeat(ssd_d, SSD_HEAD_DIM, axis=-1))
    dt_bias, a_log = (row(ssd_dt_bias), col(ssd_dt_bias)), (row(ssd_a_log), col(ssd_a_log))
    w_glu, w_branch, w_out = s5_w_glu.astype(BF16), w_branch.astype(BF16), w_out.astype(BF16)
    w_ffn_in, w_ffn_out = w_ffn_in.astype(BF16), w_ffn_out.astype(BF16)
    cache_k = cache_gqa_k.reshape(nb_lat, depth, past, kvw).astype(BF16)
    cache_v = cache_gqa_v.reshape(nb_lat, depth, past, kvw).astype(BF16)
    zpad = jnp.zeros(cache_mla_kpe.shape[:-1] + (LANE - MLA_ROPE,), F32)
    cache_kv = jnp.concatenate([cache_mla_ckv, cache_mla_kpe, zpad], axis=-1).astype(BF16)
    h0_ssd = jnp.moveaxis(state_ssd, (1, 2), (0, 1)).reshape(depth, 2, nb_lat, SSD_GROUPS, gw, SSD_STATE)
    h0_s5 = jnp.moveaxis(state_s5, 1, 0).reshape(depth, nb_lat, 4, 1, n_s5 * S5_STATE)
    ssd_new = jax.ShapeDtypeStruct((nb_ctx, depth, 2, SSD_GROUPS, gw, SSD_STATE), F32)
    new_buf = lambda: jax.ShapeDtypeStruct((m, branch_w), BF16)
    gqa = dict(n_kv=GQA_KV_HEADS, n_rep=cols["gq"][1] // kvw, dk=GQA_HEAD_DIM, dv=GQA_HEAD_DIM)
    mla = dict(n_kv=MLA_HEADS, n_rep=1, dk=MLA_QK, dv=LANE, k_off=0, v_off=MLA_HEADS * MLA_QK)
    nchunk = m // S5_CHUNK

    x = jnp.concatenate([x_prompt.reshape(ctx_rows, d_model), x_sample.reshape(lat_rows, d_model)], axis=0)
    new = []
    for l in range(depth):
        proj, small = _in_proj(x, norm1_g, mod, w_main, w_small, l, dims)

        qn, kn, k_own = _gqa_prep(proj, gqa_qn_g, gqa_kn_g, None, l=l, row0=0, nrows=ctx_rows, seq_len=len_ctx,
                                  cols=cols)
        o_a = _attention(qn, kn.reshape(nb_ctx, len_ctx, kvw), proj.reshape(-1, len_ctx, proj.shape[1]), new_buf(),
                         nseq=nb_ctx, seq_len=len_ctx, tq=len_ctx, kv_per_step=GQA_KV_HEADS, k_off=0,
                         v_off=cols["gv"][0], row0=0, **gqa)
        v_own = proj[:ctx_rows, cols["gv"][0]:cols["gv"][0] + kvw]
        qn, kn = _gqa_prep(proj, gqa_qn_g, gqa_kn_g, tabs_a, l=l, row0=ctx_rows, nrows=lat_rows, seq_len=len_lat,
                           cols=cols)
        k_lat = jnp.concatenate([kn.reshape(nb_lat, len_lat, kvw), cache_k[:, l]], axis=1)
        v_lat = jnp.concatenate([proj[ctx_rows:, cols["gv"][0]:cols["gv"][0] + kvw].reshape(nb_lat, len_lat, kvw),
                                 cache_v[:, l]], axis=1)
        o_a = _attention(qn, k_lat, v_lat, o_a, nseq=nb_lat, seq_len=len_lat, tq=min(TQ_LAT, len_lat),
                         kv_per_step=1, k_off=0, v_off=0, row0=ctx_rows, **gqa)

        qdn, kv_in, ckv_own = _mla_prep(proj, small, mla_qn_g, mla_kvn_g, None, l=l, row0=0, nrows=ctx_rows,
                                        seq_len=len_ctx, cols=cols)
        kpe_own = small[:ctx_rows, :MLA_ROPE]
        kv = _mla_kv(kv_in, w_kv, l).reshape(nb_ctx, len_ctx, -1)
        o_c = _attention(_mla_q(qdn, w_q, None, l=l, seq_len=len_ctx), kv, kv, new_buf(),
                         nseq=nb_ctx, seq_len=len_ctx, tq=len_ctx, kv_per_step=MLA_HEADS, row0=0, **mla)
        qdn, kv_in = _mla_prep(proj, small, mla_qn_g, mla_kvn_g, tabs_c, l=l, row0=ctx_rows, nrows=lat_rows,
                               seq_len=len_lat, cols=cols)
        kv_in = jnp.concatenate([kv_in.reshape(nb_lat, len_lat, MLA_KV_IN), cache_kv[:, l]], axis=1)
        kv = _mla_kv(kv_in.reshape(-1, MLA_KV_IN), w_kv, l).reshape(nb_lat, len_lat + past, -1)
        o_c = _attention(_mla_q(qdn, w_q, tabs_c, l=l, seq_len=len_lat), kv, kv, o_c,
                         nseq=nb_lat, seq_len=len_lat, tq=min(TQ_LAT, len_lat), kv_per_step=MLA_LAT_HEADS_PER_STEP,
                         row0=ctx_rows, **mla)

        dt_col = small[:, MLA_ROPE:MLA_ROPE + 2 * n_heads]
        dt_row = dt_col.T
        o_b = new_buf()
        for nseq, seq_len, row0, h0 in ((nb_ctx, len_ctx, 0, None), (nb_lat, len_lat, ctx_rows, h0_ssd)):
            xbc = _ssd_conv(proj, ssd_conv_w, conv_b, l=l, seq0=row0 // seq_len, nseq=nseq, seq_len=seq_len,
                            cols=cols)
            common = dict(l=l, nseq=nseq, seq_len=seq_len, row0=row0, cols=cols)
            keep = h0 is None
            y_f, st = _ssd_pass(xbc, dt_col, dt_row, dt_bias, a_log, h0, None, None, ssd_new if keep else None,
                                d=0, **common)
            ssd_new = st if keep else ssd_new
            o_b, st = _ssd_pass(xbc, dt_col, dt_row, dt_bias, a_log, h0, (y_f, proj, d_skip, ssd_norm_g), o_b,
                                ssd_new if keep else None, d=1, **common)
            ssd_new = st if keep else ssd_new

        u = proj[:, cols["s5u"][0]:cols["s5u"][0] + branch_w]
        u_t = u.reshape(nchunk, S5_CHUNK * branch_w).T.reshape(S5_CHUNK, n_s5, S5_GROUP_CH, nchunk)
        s5_args = (u_t, s5_toep, s5_state, s5_carry, s5_apow)
        y_t, s5_new = _s5_scan(*s5_args, None, jax.ShapeDtypeStruct(u_t.shape, F32), l=l, nseq=nb_ctx,
                               seq_len=len_ctx, row0=0)
        y_t, _ = _s5_scan(*s5_args, h0_s5, y_t, l=l, nseq=nb_lat, seq_len=len_lat, row0=ctx_rows // S5_CHUNK)
        y = y_t.reshape(S5_CHUNK * branch_w, nchunk).T.reshape(m, branch_w)
        o_d = _s5_glu(y, proj, s5_d, w_glu, l, dims, cols)

        mixed = _branch_mix([o_a, o_b, o_c, o_d], proj, w_branch, cols["gate"][0], l, dims)
        x = _resid_proj(mixed, w_out, x, mod, 2, 2 * TM, l, dims)
        hidden = _ffn_in(x, norm2_g, mod, w_ffn_in, l, dims)
        x = _resid_proj(hidden, w_ffn_out, x, mod, 5, TM, l, dims)
        new.append((k_own.reshape(nb_ctx, len_ctx, GQA_KV_HEADS, GQA_HEAD_DIM),
                    v_own.astype(F32).reshape(nb_ctx, len_ctx, GQA_KV_HEADS, GQA_HEAD_DIM),
                    ckv_own.reshape(nb_ctx, len_ctx, MLA_KV_LORA),
                    kpe_own.reshape(nb_ctx, len_ctx, MLA_ROPE),
                    s5_new.reshape(nb_ctx, 2, 2, n_s5, S5_STATE)))

    stacked = [jnp.stack([layer_out[i] for layer_out in new], axis=1) for i in range(5)]
    return (_final_norm(x, final_g, 0, ctx_rows).reshape(x_prompt.shape),
            _final_norm(x, final_g, ctx_rows, lat_rows).reshape(x_sample.shape),
            *stacked[:4],
            ssd_new.reshape(nb_ctx, depth, 2, n_heads, SSD_HEAD_DIM, SSD_STATE),
            stacked[4])
```

```python
import functools
import math

import jax
import jax.numpy as jnp
from jax import lax
from jax.experimental import pallas as pl
from jax.experimental.pallas import tpu as pltpu

F32 = jnp.float32
BF16 = jnp.bfloat16
HIGHEST = lax.Precision.HIGHEST

GRID_W = 64
N_BRANCH = 4
GQA_HEAD_DIM = 128
GQA_KV_HEADS = 2
SSD_HEAD_DIM = 64
SSD_GROUPS = 2
SSD_STATE = 128
SSD_CONV = 5
SSD_CHUNK = 128
MLA_HEADS = 8
MLA_NOPE = 128
MLA_ROPE = 64
MLA_Q_LORA = 512
MLA_KV_LORA = 256
S5_GROUP_CH = 16
S5_STATE = 64
S5_CHUNK = 16
ROPE_THETA = 10000.0
NORM_EPS = 1e-6
LANE = 128
VMEM_LIMIT = 56 * 1024 * 1024

TM = 1024
TN_IN = 1280
TN_FFN = 512
TN_OUT = 512
TR_ROWS = 1024
TQ_LAT = 512
MLA_LAT_HEADS_PER_STEP = 4
S5_GB = 8


def _cparams(sem):
    return pltpu.CompilerParams(dimension_semantics=sem, vmem_limit_bytes=VMEM_LIMIT)


def _row_group(i, tm, ctx_rows, lat_len):
    nct = ctx_rows // tm
    per = lat_len // tm
    return jnp.where(i < nct, 0, 1 + (i - nct) // per)


def _lspec(l, block, index_map):
    return pl.BlockSpec((None,) + tuple(block), lambda *ids: (l,) + tuple(index_map(*ids)))


def _silu(x):
    return x * jax.nn.sigmoid(x)


def _softplus(x):
    return jnp.maximum(x, 0.0) + jnp.log(1.0 + jnp.exp(-jnp.abs(x)))


def _rms(x, g):
    ms = jnp.mean(x * x, axis=-1, keepdims=True)
    return x * lax.rsqrt(ms + NORM_EPS) * g


def _mod_kernel(c_ref, w_ref, b_ref, o_ref):
    c = c_ref[...]
    s = _silu(c).astype(BF16)
    o_ref[0] = jnp.dot(s, w_ref[0].astype(BF16), preferred_element_type=F32) + b_ref[0]


def _modulation(cvec, w_mod, b_mod):
    depth, d, n = w_mod.shape
    tn = 1024
    return pl.pallas_call(
        _mod_kernel,
        out_shape=jax.ShapeDtypeStruct((depth, 8, n), F32),
        grid=(depth, n // tn),
        in_specs=[pl.BlockSpec((8, d), lambda l, j: (0, 0)),
                  pl.BlockSpec((1, d, tn), lambda l, j: (l, 0, j)),
                  pl.BlockSpec((1, 1, tn), lambda l, j: (l, 0, j))],
        out_specs=pl.BlockSpec((1, 8, tn), lambda l, j: (l, 0, j)),
        compiler_params=_cparams(("parallel", "parallel")),
    )(cvec, w_mod, b_mod.reshape(depth, 1, n))


def _norm_mod(x_ref, g_ref, sc_ref, sh_ref, hs_ref):
    y = _rms(x_ref[...], g_ref[...])
    hs_ref[...] = (y * (1.0 + sc_ref[...]) + sh_ref[...]).astype(BF16)


def _in_proj_kernel(x_ref, g_ref, sc_ref, sh_ref, w_ref, ws_ref, o_ref, os_ref, hs_ref):
    @pl.when(pl.program_id(1) == 0)
    def _():
        _norm_mod(x_ref, g_ref, sc_ref, sh_ref, hs_ref)
        os_ref[...] = jnp.dot(hs_ref[...], ws_ref[...], preferred_element_type=F32)

    o_ref[...] = jnp.dot(hs_ref[...], w_ref[...], preferred_element_type=F32).astype(o_ref.dtype)


def _in_proj(x, norm_g, mod, w_main, w_small, l, dims):
    m, d = x.shape
    npad = w_main.shape[2]
    tm, tn = min(TM, dims["lat_len"]), TN_IN
    grp = lambda i: _row_group(i, tm, dims["ctx_rows"], dims["lat_len"])
    return pl.pallas_call(
        _in_proj_kernel,
        out_shape=(jax.ShapeDtypeStruct((m, npad), BF16), jax.ShapeDtypeStruct((m, LANE), F32)),
        grid=(m // tm, npad // tn),
        in_specs=[pl.BlockSpec((tm, d), lambda i, j: (i, 0)),
                  _lspec(l, (1, d), lambda i, j: (0, 0)),
                  _lspec(l, (None, None, 1, d), lambda i, j: (grp(i), 1, 0, 0)),
                  _lspec(l, (None, None, 1, d), lambda i, j: (grp(i), 0, 0, 0)),
                  _lspec(l, (d, tn), lambda i, j: (0, j)),
                  _lspec(l, (d, LANE), lambda i, j: (0, 0))],
        out_specs=(pl.BlockSpec((tm, tn), lambda i, j: (i, j)),
                   pl.BlockSpec((tm, LANE), lambda i, j: (i, 0))),
        scratch_shapes=[pltpu.VMEM((tm, d), BF16)],
        compiler_params=_cparams(("parallel", "arbitrary")),
    )(x, norm_g, mod, mod, w_main, w_small)


def _ffn_in_kernel(x_ref, g_ref, sc_ref, sh_ref, wg_ref, wu_ref, o_ref, hs_ref):
    @pl.when(pl.program_id(1) == 0)
    def _():
        _norm_mod(x_ref, g_ref, sc_ref, sh_ref, hs_ref)

    h = hs_ref[...]
    a = jnp.dot(h, wg_ref[...], preferred_element_type=F32)
    b = jnp.dot(h, wu_ref[...], preferred_element_type=F32)
    o_ref[...] = (_silu(a) * b).astype(o_ref.dtype)


def _ffn_in(x, norm_g, mod, w, l, dims):
    m, d = x.shape
    hid = w.shape[2] // 2
    tm, tn = min(TM, dims["lat_len"]), TN_FFN
    nj = hid // tn
    grp = lambda i: _row_group(i, tm, dims["ctx_rows"], dims["lat_len"])
    return pl.pallas_call(
        _ffn_in_kernel,
        out_shape=jax.ShapeDtypeStruct((m, hid), BF16),
        grid=(m // tm, nj),
        in_specs=[pl.BlockSpec((tm, d), lambda i, j: (i, 0)),
                  _lspec(l, (1, d), lambda i, j: (0, 0)),
                  _lspec(l, (None, None, 1, d), lambda i, j: (grp(i), 4, 0, 0)),
                  _lspec(l, (None, None, 1, d), lambda i, j: (grp(i), 3, 0, 0)),
                  _lspec(l, (d, tn), lambda i, j: (0, j)),
                  _lspec(l, (d, tn), lambda i, j: (0, j + nj))],
        out_specs=pl.BlockSpec((tm, tn), lambda i, j: (i, j)),
        scratch_shapes=[pltpu.VMEM((tm, d), BF16)],
        compiler_params=_cparams(("parallel", "arbitrary")),
    )(x, norm_g, mod, mod, w, w)


def _resid_kernel(x_ref, w_ref, r_ref, g_ref, o_ref):
    o_ref[...] = r_ref[...] + g_ref[...] * jnp.dot(x_ref[...], w_ref[...], preferred_element_type=F32)


def _resid_proj(xin, w, resid, mod, mod_idx, tm, l, dims):
    m, k = xin.shape
    n = w.shape[2]
    tm, tn = min(tm, dims["lat_len"]), TN_OUT
    grp = lambda i: _row_group(i, tm, dims["ctx_rows"], dims["lat_len"])
    return pl.pallas_call(
        _resid_kernel,
        out_shape=jax.ShapeDtypeStruct((m, n), F32),
        grid=(m // tm, n // tn),
        in_specs=[pl.BlockSpec((tm, k), lambda i, j: (i, 0)),
                  _lspec(l, (k, tn), lambda i, j: (0, j)),
                  pl.BlockSpec((tm, tn), lambda i, j: (i, j)),
                  _lspec(l, (None, None, 1, tn), lambda i, j: (grp(i), mod_idx, 0, j))],
        out_specs=pl.BlockSpec((tm, tn), lambda i, j: (i, j)),
        compiler_params=_cparams(("parallel", "arbitrary")),
    )(xin, w, resid, mod)


def _mix_kernel(oa_ref, ob_ref, oc_ref, od_ref, ga_ref, gb_ref, gc_ref, gd_ref, w_ref, o_ref):
    acc = None
    for n, (o_n, g_n) in enumerate(((oa_ref, ga_ref), (ob_ref, gb_ref), (oc_ref, gc_ref), (od_ref, gd_ref))):
        proj = jnp.dot(o_n[...], w_ref[n], preferred_element_type=F32)
        term = jax.nn.sigmoid(g_n[...].astype(F32)) * proj
        acc = term if acc is None else acc + term
    o_ref[...] = acc.astype(o_ref.dtype)


def _branch_mix(branches, proj_all, w_branch, col_gate, l, dims):
    m, bw = branches[0].shape
    d = w_branch.shape[3]
    tm, tn = min(TM, dims["lat_len"]), TN_OUT
    gate_specs = [pl.BlockSpec((tm, tn), functools.partial(lambda i, j, n: (i, (col_gate + n * d) // tn + j), n=n))
                  for n in range(N_BRANCH)]
    return pl.pallas_call(
        _mix_kernel,
        out_shape=jax.ShapeDtypeStruct((m, d), BF16),
        grid=(m // tm, d // tn),
        in_specs=[pl.BlockSpec((tm, bw), lambda i, j: (i, 0))] * N_BRANCH + gate_specs
                 + [_lspec(l, (N_BRANCH, bw, tn), lambda i, j: (0, 0, j))],
        out_specs=pl.BlockSpec((tm, tn), lambda i, j: (i, j)),
        compiler_params=_cparams(("parallel", "arbitrary")),
    )(*branches, proj_all, proj_all, proj_all, proj_all, w_branch)


def _final_norm_kernel(x_ref, g_ref, o_ref):
    o_ref[...] = _rms(x_ref[...], g_ref[...])


def _final_norm(x, g, row0, nrows):
    d = x.shape[1]
    tm = 512
    rb0 = row0 // tm
    return pl.pallas_call(
        _final_norm_kernel,
        out_shape=jax.ShapeDtypeStruct((nrows, d), F32),
        grid=(nrows // tm,),
        in_specs=[pl.BlockSpec((tm, d), lambda i: (rb0 + i, 0)), pl.BlockSpec((1, d), lambda i: (0, 0))],
        out_specs=pl.BlockSpec((tm, d), lambda i: (i, 0)),
        compiler_params=_cparams(("parallel",)),
    )(x, g.reshape(1, d))


def _rope_tables(seq_len, dim):
    nf = dim // 4
    t = jnp.arange(seq_len)
    row = (t // GRID_W).astype(F32)
    col = (t % GRID_W).astype(F32)
    inv = ROPE_THETA ** (-jnp.arange(nf, dtype=F32) / nf)
    ang_r, ang_c = row[:, None] * inv, col[:, None] * inv
    cos = jnp.concatenate([jnp.cos(ang_r), jnp.cos(ang_r), jnp.cos(ang_c), jnp.cos(ang_c)], axis=1)
    sin = jnp.concatenate([-jnp.sin(ang_r), jnp.sin(ang_r), -jnp.sin(ang_c), jnp.sin(ang_c)], axis=1)
    pad = LANE - dim
    if pad:
        cos = jnp.concatenate([cos, jnp.ones((seq_len, pad), F32)], axis=1)
        sin = jnp.concatenate([sin, jnp.zeros((seq_len, pad), F32)], axis=1)
    return cos, sin


def _rope(x, cos, sin, nf):
    lane = lax.broadcasted_iota(jnp.int32, x.shape, 1)
    swapped = jnp.where((lane % (2 * nf)) < nf, pltpu.roll(x, LANE - nf, 1), pltpu.roll(x, nf, 1))
    return x * cos + swapped * sin


def _attn_kernel(q_ref, k_ref, v_ref, o_ref, *, kv_per_step, n_rep, dk, dv):
    for a in range(kv_per_step):
        k = k_ref[0, :, a * dk:(a + 1) * dk]
        v = v_ref[0, :, a * dv:(a + 1) * dv]
        v_ones = jnp.concatenate([v, jnp.ones_like(v)], axis=1)
        for r in range(n_rep):
            h = a * n_rep + r
            q = q_ref[:, h * dk:(h + 1) * dk]
            s = lax.dot_general(q, k, (((1,), (1,)), ((), ())), preferred_element_type=F32)
            p = jnp.exp((s - jnp.max(s, axis=-1, keepdims=True)).astype(BF16))
            o = jnp.dot(p, v_ones, preferred_element_type=F32)
            o_ref[:, h * dv:(h + 1) * dv] = (o[:, :dv] / o[:, dv:]).astype(o_ref.dtype)


def _skip_inputs(kern, positions, *refs):
    return kern(*(r for i, r in enumerate(refs) if i not in positions))


def _in_place(kern, args, in_specs, out_bufs):
    aliases, skipped = {}, []
    for out_idx, buf in out_bufs.items():
        if buf is None or isinstance(buf, jax.ShapeDtypeStruct):
            continue
        in_specs.append(pl.BlockSpec(memory_space=pl.ANY))
        args.append(buf)
        aliases[len(args) - 1] = out_idx
        skipped.append(len(args) - 1)
    if skipped:
        kern = functools.partial(_skip_inputs, kern, tuple(skipped))
    return kern, aliases


def _attention(q, k, v, out_buf, *, nseq, seq_len, tq, n_kv, kv_per_step, n_rep, dk, dv, k_off, v_off, row0):
    lk = k.shape[1]
    nq = seq_len // tq
    rb0 = row0 // tq
    kw, vw = kv_per_step * dk, kv_per_step * dv
    assert k_off % kw == 0 and v_off % vw == 0 and n_kv % kv_per_step == 0
    kb0, vb0 = k_off // kw, v_off // vw
    kern = functools.partial(_attn_kernel, kv_per_step=kv_per_step, n_rep=n_rep, dk=dk, dv=dv)
    args, in_specs = [q, k, v], [pl.BlockSpec((tq, n_rep * kw), lambda b, g, i: (b * nq + i, g)),
                                 pl.BlockSpec((1, lk, kw), lambda b, g, i: (b, 0, kb0 + g)),
                                 pl.BlockSpec((1, lk, vw), lambda b, g, i: (b, 0, vb0 + g))]
    kern, aliases = _in_place(kern, args, in_specs, {0: out_buf})
    return pl.pallas_call(
        kern,
        out_shape=jax.ShapeDtypeStruct(out_buf.shape, out_buf.dtype),
        grid=(nseq, n_kv // kv_per_step, nq),
        in_specs=in_specs,
        out_specs=pl.BlockSpec((tq, n_rep * vw), lambda b, g, i: (rb0 + b * nq + i, g)),
        input_output_aliases=aliases,
        compiler_params=_cparams(("parallel", "parallel", "arbitrary")),
    )(*args)


def _gqa_prep_kernel(*refs, rope, scale, n_q, n_kv):
    if rope:
        q_ref, k_ref, qg_ref, kg_ref, cos_ref, sin_ref, qo_ref, ko_ref = refs
    else:
        q_ref, k_ref, qg_ref, kg_ref, qo_ref, ko_ref, k32_ref = refs
    hd = GQA_HEAD_DIM
    for h in range(n_q):
        y = _rms(q_ref[:, h * hd:(h + 1) * hd].astype(F32), qg_ref[...])
        if rope:
            y = _rope(y, cos_ref[...], sin_ref[...], hd // 4)
        qo_ref[:, h * hd:(h + 1) * hd] = (y * scale).astype(qo_ref.dtype)
    for h in range(n_kv):
        y = _rms(k_ref[:, h * hd:(h + 1) * hd].astype(F32), kg_ref[...])
        if rope:
            y = _rope(y, cos_ref[...], sin_ref[...], hd // 4)
        else:
            k32_ref[:, h * hd:(h + 1) * hd] = y
        ko_ref[:, h * hd:(h + 1) * hd] = y.astype(ko_ref.dtype)


def _gqa_prep(proj, qg, kg, tabs, *, l, row0, nrows, seq_len, cols):
    tr = TR_ROWS if tabs is None else min(TR_ROWS, seq_len)
    rb0 = row0 // tr
    qw, kw = cols["gq"][1], cols["gk"][1]
    n_q, n_kv = qw // GQA_HEAD_DIM, kw // GQA_HEAD_DIM
    rope = tabs is not None
    in_specs = [pl.BlockSpec((tr, qw), lambda i: (rb0 + i, cols["gq"][0] // qw)),
                pl.BlockSpec((tr, kw), lambda i: (rb0 + i, cols["gk"][0] // kw)),
                _lspec(l, (1, GQA_HEAD_DIM), lambda i: (0, 0)),
                _lspec(l, (1, GQA_HEAD_DIM), lambda i: (0, 0))]
    args = [proj, proj, qg, kg]
    out_shape = [jax.ShapeDtypeStruct((nrows, qw), BF16), jax.ShapeDtypeStruct((nrows, kw), BF16)]
    out_specs = [pl.BlockSpec((tr, qw), lambda i: (i, 0)), pl.BlockSpec((tr, kw), lambda i: (i, 0))]
    if rope:
        per = seq_len // tr
        in_specs += [pl.BlockSpec((tr, LANE), lambda i: (i % per, 0))] * 2
        args += list(tabs)
    else:
        out_shape.append(jax.ShapeDtypeStruct((nrows, kw), F32))
        out_specs.append(pl.BlockSpec((tr, kw), lambda i: (i, 0)))
    return pl.pallas_call(
        functools.partial(_gqa_prep_kernel, rope=rope, scale=GQA_HEAD_DIM ** -0.5, n_q=n_q, n_kv=n_kv),
        out_shape=tuple(out_shape),
        grid=(nrows // tr,),
        in_specs=in_specs,
        out_specs=tuple(out_specs),
        compiler_params=_cparams(("parallel",)),
    )(*args)


MLA_KV_IN = MLA_KV_LORA + LANE
MLA_QK = 2 * LANE


def _mla_prep_kernel(*refs, rope):
    if rope:
        qd_ref, ckv_ref, sm_ref, qg_ref, kvg_ref, cos_ref, sin_ref, qo_ref, kvo_ref = refs
    else:
        qd_ref, ckv_ref, sm_ref, qg_ref, kvg_ref, qo_ref, kvo_ref, ckv32_ref = refs
    qo_ref[...] = _rms(qd_ref[...].astype(F32), qg_ref[...]).astype(qo_ref.dtype)
    ckv = _rms(ckv_ref[...].astype(F32), kvg_ref[...])
    kvo_ref[:, :MLA_KV_LORA] = ckv.astype(kvo_ref.dtype)
    sm = sm_ref[...]
    if rope:
        sm = _rope(sm, cos_ref[...], sin_ref[...], MLA_ROPE // 4)
    else:
        ckv32_ref[...] = ckv
    lane = lax.broadcasted_iota(jnp.int32, sm.shape, 1)
    kvo_ref[:, MLA_KV_LORA:] = jnp.where(lane < MLA_ROPE, sm, 0.0).astype(kvo_ref.dtype)


def _mla_prep(proj, small, qg, kvg, tabs, *, l, row0, nrows, seq_len, cols):
    tr = TR_ROWS if tabs is None else min(TR_ROWS, seq_len)
    rb0 = row0 // tr
    qw, cw = cols["mqd"][1], cols["ckv"][1]
    rope = tabs is not None
    in_specs = [pl.BlockSpec((tr, qw), lambda i: (rb0 + i, cols["mqd"][0] // qw)),
                pl.BlockSpec((tr, cw), lambda i: (rb0 + i, cols["ckv"][0] // cw)),
                pl.BlockSpec((tr, LANE), lambda i: (rb0 + i, 0)),
                _lspec(l, (1, qw), lambda i: (0, 0)),
                _lspec(l, (1, cw), lambda i: (0, 0))]
    args = [proj, proj, small, qg, kvg]
    out_shape = [jax.ShapeDtypeStruct((nrows, qw), BF16), jax.ShapeDtypeStruct((nrows, MLA_KV_IN), BF16)]
    out_specs = [pl.BlockSpec((tr, qw), lambda i: (i, 0)), pl.BlockSpec((tr, MLA_KV_IN), lambda i: (i, 0))]
    if rope:
        per = seq_len // tr
        in_specs += [pl.BlockSpec((tr, LANE), lambda i: (i % per, 0))] * 2
        args += list(tabs)
    else:
        out_shape.append(jax.ShapeDtypeStruct((nrows, cw), F32))
        out_specs.append(pl.BlockSpec((tr, cw), lambda i: (i, 0)))
    return pl.pallas_call(
        functools.partial(_mla_prep_kernel, rope=rope),
        out_shape=tuple(out_shape),
        grid=(nrows // tr,),
        in_specs=in_specs,
        out_specs=tuple(out_specs),
        compiler_params=_cparams(("parallel",)),
    )(*args)


def _mla_q_kernel(*refs, rope, scale):
    if rope:
        x_ref, w_ref, cos_ref, sin_ref, o_ref = refs
    else:
        x_ref, w_ref, o_ref = refs
    acc = jnp.dot(x_ref[...], w_ref[...], preferred_element_type=F32) * scale
    if rope:
        o_ref[:, :LANE] = acc[:, :LANE].astype(o_ref.dtype)
        o_ref[:, LANE:] = _rope(acc[:, LANE:], cos_ref[...], sin_ref[...], MLA_ROPE // 4).astype(o_ref.dtype)
    else:
        o_ref[...] = acc.astype(o_ref.dtype)


def _mla_q(qdn, w_q, tabs, *, l, seq_len):
    nrows, k = qdn.shape
    rope = tabs is not None
    tm = min(TR_ROWS, seq_len) if rope else TR_ROWS
    in_specs = [pl.BlockSpec((tm, k), lambda i, h: (i, 0)), _lspec(l, (k, MLA_QK), lambda i, h: (0, h))]
    args = [qdn, w_q]
    if rope:
        per = seq_len // tm
        in_specs += [pl.BlockSpec((tm, LANE), lambda i, h: (i % per, 0))] * 2
        args += list(tabs)
    return pl.pallas_call(
        functools.partial(_mla_q_kernel, rope=rope, scale=(MLA_NOPE + MLA_ROPE) ** -0.5),
        out_shape=jax.ShapeDtypeStruct((nrows, MLA_HEADS * MLA_QK), BF16),
        grid=(nrows // tm, MLA_HEADS),
        in_specs=in_specs,
        out_specs=pl.BlockSpec((tm, MLA_QK), lambda i, h: (i, h)),
        compiler_params=_cparams(("parallel", "arbitrary")),
    )(*args)


def _matmul_kernel(x_ref, w_ref, o_ref):
    o_ref[...] = jnp.dot(x_ref[...], w_ref[...], preferred_element_type=F32).astype(o_ref.dtype)


def _mla_kv(kv_in, w_kv, l):
    nrows, k = kv_in.shape
    n = w_kv.shape[2]
    tm = 512 if nrows % 512 == 0 else 256
    return pl.pallas_call(
        _matmul_kernel,
        out_shape=jax.ShapeDtypeStruct((nrows, n), BF16),
        grid=(nrows // tm,),
        in_specs=[pl.BlockSpec((tm, k), lambda i: (i, 0)), _lspec(l, (k, n), lambda i: (0, 0))],
        out_specs=pl.BlockSpec((tm, n), lambda i: (i, 0)),
        compiler_params=_cparams(("parallel",)),
    )(kv_in, w_kv)


def _conv_kernel(x_ref, w_ref, b_ref, o_ref):
    x = x_ref[0].astype(F32)
    seq = x.shape[0]
    edge = 8
    row = lax.broadcasted_iota(jnp.int32, (edge, x.shape[1]), 0)
    pad = SSD_CONV // 2
    acc = x * w_ref[pad:pad + 1, :] + b_ref[...]
    for k in range(SSD_CONV):
        d = k - pad
        if d == 0:
            continue
        if d > 0:
            src = jnp.concatenate([jnp.where(row < d, 0.0, x[:edge]), x[edge:]], axis=0)
        else:
            src = jnp.concatenate([x[:seq - edge], jnp.where(row >= edge + d, 0.0, x[seq - edge:])], axis=0)
        acc = acc + pltpu.roll(src, (-d) % seq, 0) * w_ref[k:k + 1, :]
    o_ref[0] = _silu(acc).astype(o_ref.dtype)


def _ssd_conv(proj, conv_w, conv_b, *, l, seq0, nseq, seq_len, cols):
    npad = proj.shape[1]
    c0, cw = cols["sxbc"]
    tc = 256
    view = proj.reshape(-1, seq_len, npad)
    return pl.pallas_call(
        _conv_kernel,
        out_shape=jax.ShapeDtypeStruct((nseq, seq_len, cw), BF16),
        grid=(nseq, cw // tc),
        in_specs=[pl.BlockSpec((1, seq_len, tc), lambda s, c: (seq0 + s, 0, c0 // tc + c)),
                  _lspec(l, (SSD_CONV, tc), lambda s, c: (0, c)),
                  _lspec(l, (1, tc), lambda s, c: (0, c))],
        out_specs=pl.BlockSpec((1, seq_len, tc), lambda s, c: (s, 0, c)),
        compiler_params=_cparams(("parallel", "parallel")),
    )(view, conv_w, conv_b).reshape(nseq * seq_len, cw)


def _pair_cols(vals, h0):
    q = vals.shape[0]
    lane = lax.broadcasted_iota(jnp.int32, (q, LANE), 1)
    return jnp.where(lane < SSD_HEAD_DIM, vals[:, h0:h0 + 1], vals[:, h0 + 1:h0 + 2])


def _ssd_kernel(*refs, reverse, zero_init, final, d):
    if final:
        (x_ref, b_ref, c_ref, dtc_ref, dtr_ref, biasc_ref, biasr_ref, alogc_ref, alogr_ref, h0_ref,
         yf_ref, z_ref, dskip_ref, ng_ref, y_ref, hout_ref, st_ref) = refs
    else:
        (x_ref, b_ref, c_ref, dtc_ref, dtr_ref, biasc_ref, biasr_ref, alogc_ref, alogr_ref, h0_ref,
         y_ref, hout_ref, st_ref) = refs
    q = SSD_CHUNK
    nh = dtc_ref.shape[1] // 2
    hpg = nh // SSD_GROUPS
    gw = hpg * SSD_HEAD_DIM
    c = pl.program_id(1)

    @pl.when(c == 0)
    def _():
        for g in range(SSD_GROUPS):
            if zero_init:
                st_ref[g] = jnp.zeros(st_ref.shape[1:], F32)
            else:
                st_ref[g] = h0_ref[0, g].T

    lo = d * nh
    dt_c = _softplus(dtc_ref[:, lo:lo + nh] + biasc_ref[:, lo:lo + nh])
    dt_r = _softplus(dtr_ref[lo:lo + nh, :] + biasr_ref[lo:lo + nh, :])
    a_c = dt_c * (-jnp.exp(alogc_ref[:, lo:lo + nh]))
    a_r = dt_r * (-jnp.exp(alogr_ref[lo:lo + nh, :]))
    ii = lax.broadcasted_iota(jnp.int32, (q, q), 0)
    jj = lax.broadcasted_iota(jnp.int32, (q, q), 1)
    causal = (jj >= ii) if reverse else (jj <= ii)
    tri = causal.astype(F32)
    tri_t = ((ii >= jj) if reverse else (ii <= jj)).astype(F32)
    cum_c = jnp.dot(tri, a_c, precision=HIGHEST, preferred_element_type=F32)
    cum_r = jnp.dot(a_r, tri_t, precision=HIGHEST, preferred_element_type=F32)
    cum_dt_r = cum_r - jnp.log(dt_r)
    tot_c = jnp.sum(a_c, axis=0, keepdims=True)
    e_in = jnp.exp(cum_c)
    w_out = dt_c * jnp.exp(tot_c - cum_c)
    e_tot = jnp.exp(tot_c)

    x = x_ref[...]
    lane = lax.broadcasted_iota(jnp.int32, (q, LANE), 1)
    lane_row = lax.broadcasted_iota(jnp.int32, (1, LANE), 1)
    y_groups = []
    for g in range(SSD_GROUPS):
        bg = b_ref[:, g * SSD_STATE:(g + 1) * SSD_STATE]
        cg = c_ref[:, g * SSD_STATE:(g + 1) * SSD_STATE]
        cb = lax.dot_general(cg, bg, (((1,), (1,)), ((), ())), preferred_element_type=F32)
        st = st_ref[g]
        y_in = jnp.dot(cg, st.astype(BF16), preferred_element_type=F32)
        y_pairs, xs_pairs, dec_pairs = [], [], []
        for pr in range(hpg // 2):
            h0 = g * hpg + 2 * pr
            x_pair = x[:, h0 * SSD_HEAD_DIM:(h0 + 2) * SSD_HEAD_DIM]
            att = []
            for h in (h0, h0 + 1):
                seg = cum_c[:, h:h + 1] - cum_dt_r[h:h + 1, :]
                att.append((cb * jnp.exp(jnp.where(causal, seg, -jnp.inf))).astype(BF16))
            yd = jnp.dot(jnp.concatenate(att, axis=0), x_pair, preferred_element_type=F32)
            y_pair = jnp.where(lane < SSD_HEAD_DIM, yd[:q], yd[q:])
            y_pair = y_pair + _pair_cols(e_in, h0) * y_in[:, 2 * pr * SSD_HEAD_DIM:(2 * pr + 2) * SSD_HEAD_DIM]
            y_pairs.append(y_pair)
            xs_pairs.append((x_pair.astype(F32) * _pair_cols(w_out, h0)).astype(BF16))
            dec_pairs.append(jnp.where(lane_row < SSD_HEAD_DIM, e_tot[:, h0:h0 + 1], e_tot[:, h0 + 1:h0 + 2]))
        xs_dec = jnp.concatenate(xs_pairs, axis=1)
        upd = lax.dot_general(bg, xs_dec, (((0,), (0,)), ((), ())), preferred_element_type=F32)
        st_ref[g] = st * jnp.concatenate(dec_pairs, axis=1) + upd
        y_groups.append(jnp.concatenate(y_pairs, axis=1))
    y = jnp.concatenate(y_groups, axis=1)

    if final:
        y = y + yf_ref[...] + dskip_ref[...] * x.astype(F32)
        y = y * _silu(z_ref[...].astype(F32))
        y_ref[...] = _rms(y, ng_ref[...]).astype(y_ref.dtype)
    else:
        y_ref[...] = y

    @pl.when(c == pl.num_programs(1) - 1)
    def _():
        for g in range(SSD_GROUPS):
            hout_ref[0, g] = st_ref[g].T


def _ssd_pass(xbc, dt_col, dt_row, dt_bias, a_log, h0, extra, out_buf, state_buf, *, l, d, nseq, seq_len, row0,
              cols):
    q = SSD_CHUNK
    nc = seq_len // q
    nrows = nseq * seq_len
    nh2 = dt_col.shape[1]
    inner = cols["sz"][1]
    gw = inner // SSD_GROUPS
    bw = SSD_GROUPS * SSD_STATE
    rb0 = row0 // q
    reverse = d == 1
    final = extra is not None
    cidx = (lambda c: nc - 1 - c) if reverse else (lambda c: c)
    loc = lambda s, c: s * nc + cidx(c)
    zero_init = h0 is None
    st_block = (1, SSD_GROUPS, gw, SSD_STATE)
    if zero_init:
        h0 = jnp.zeros(st_block, F32)
        h0_spec = pl.BlockSpec(st_block, lambda s, c: (0, 0, 0, 0))
    else:
        h0_spec = pl.BlockSpec((None, None) + st_block, lambda s, c: (l, d, s, 0, 0, 0))
    in_specs = [pl.BlockSpec((q, inner), lambda s, c: (loc(s, c), 0)),
                pl.BlockSpec((q, bw), lambda s, c: (loc(s, c), inner // bw)),
                pl.BlockSpec((q, bw), lambda s, c: (loc(s, c), inner // bw + 1)),
                pl.BlockSpec((q, nh2), lambda s, c: (rb0 + loc(s, c), 0)),
                pl.BlockSpec((nh2, q), lambda s, c: (0, rb0 + loc(s, c))),
                _lspec(l, (1, nh2), lambda s, c: (0, 0)),
                _lspec(l, (nh2, 1), lambda s, c: (0, 0)),
                _lspec(l, (1, nh2), lambda s, c: (0, 0)),
                _lspec(l, (nh2, 1), lambda s, c: (0, 0)),
                h0_spec]
    args = [xbc, xbc, xbc, dt_col, dt_row, dt_bias[0], dt_bias[1], a_log[0], a_log[1], h0]
    kern = functools.partial(_ssd_kernel, reverse=reverse, zero_init=zero_init, final=final, d=d)
    if final:
        y_fwd, proj, d_skip, norm_g = extra
        in_specs += [pl.BlockSpec((q, inner), lambda s, c: (loc(s, c), 0)),
                     pl.BlockSpec((q, inner), lambda s, c: (rb0 + loc(s, c), cols["sz"][0] // inner)),
                     _lspec(l, (1, inner), lambda s, c: (0, 0)),
                     _lspec(l, (1, inner), lambda s, c: (0, 0))]
        args += [y_fwd, proj, d_skip, norm_g]
        y_shape = jax.ShapeDtypeStruct(out_buf.shape, out_buf.dtype)
        y_spec = pl.BlockSpec((q, inner), lambda s, c: (rb0 + loc(s, c), 0))
    else:
        y_shape = jax.ShapeDtypeStruct((nrows, inner), F32)
        y_spec = pl.BlockSpec((q, inner), lambda s, c: (loc(s, c), 0))
    if state_buf is None:
        st_shape = jax.ShapeDtypeStruct((nseq,) + st_block[1:], F32)
        st_spec = pl.BlockSpec(st_block, lambda s, c: (s, 0, 0, 0))
    else:
        st_shape = jax.ShapeDtypeStruct(state_buf.shape, state_buf.dtype)
        st_spec = pl.BlockSpec((1, None, None) + st_block[1:], lambda s, c: (s, l, d, 0, 0, 0))
    kern, aliases = _in_place(kern, args, in_specs, {0: out_buf if final else None, 1: state_buf})
    return pl.pallas_call(
        kern,
        out_shape=(y_shape, st_shape),
        grid=(nseq, nc),
        in_specs=in_specs,
        out_specs=(y_spec, st_spec),
        scratch_shapes=[pltpu.VMEM((SSD_GROUPS, SSD_STATE, gw), F32)],
        input_output_aliases=aliases,
        compiler_params=_cparams(("parallel", "arbitrary")),
    )(*args)


def _s5_kernel(u_ref, wt_ref, ws_ref, wc_ref, a_ref, h0_ref, y_ref, hout_ref, s_scr, hin_scr, *, nseq, nc):
    gb = u_ref.shape[1]
    rows = u_ref.shape[3]
    depth = S5_CHUNK * S5_GROUP_CH
    npair = gb // 2
    w = npair * LANE
    contract_rows = (((0,), (0,)), ((), ()))
    contract_cols = (((1,), (1,)), ((), ()))

    def u_t(g):
        return u_ref[:, g].reshape(depth, rows)

    for p in range(npair):
        s = (lax.dot_general(u_t(2 * p), ws_ref[2 * p], contract_rows, preferred_element_type=F32)
             + lax.dot_general(u_t(2 * p + 1), ws_ref[2 * p + 1], contract_rows, preferred_element_type=F32))
        for comp in range(4):
            s_scr[:, comp * w + p * LANE:comp * w + (p + 1) * LANE] = s[:, comp * LANE:(comp + 1) * LANE]

    af_re, af_im, ab_re, ab_im = a_ref[0], a_ref[1], a_ref[2], a_ref[3]

    def one_sequence(sq, _):
        def step(i, carry):
            hf_re, hf_im, hb_re, hb_im = carry
            rf = sq * nc + i
            rb = sq * nc + nc - 1 - i
            hin_scr[pl.ds(rf, 1), 0:w] = hf_re
            hin_scr[pl.ds(rf, 1), w:2 * w] = hf_im
            hin_scr[pl.ds(rb, 1), 2 * w:3 * w] = hb_re
            hin_scr[pl.ds(rb, 1), 3 * w:4 * w] = hb_im
            sf_re = s_scr[pl.ds(rf, 1), 0:w]
            sf_im = s_scr[pl.ds(rf, 1), w:2 * w]
            sb_re = s_scr[pl.ds(rb, 1), 2 * w:3 * w]
            sb_im = s_scr[pl.ds(rb, 1), 3 * w:4 * w]
            return (af_re * hf_re - af_im * hf_im + sf_re,
                    af_re * hf_im + af_im * hf_re + sf_im,
                    ab_re * hb_re - ab_im * hb_im + sb_re,
                    ab_re * hb_im + ab_im * hb_re + sb_im)

        last = lax.fori_loop(0, nc, step, tuple(h0_ref[sq, comp] for comp in range(4)))
        for comp in range(4):
            hout_ref[sq, comp] = last[comp]
        return 0

    lax.fori_loop(0, nseq, one_sequence, 0)

    for p in range(npair):
        hin = jnp.concatenate([hin_scr[:, comp * w + p * LANE:comp * w + (p + 1) * LANE] for comp in range(4)],
                              axis=1).astype(BF16)
        for e in range(2):
            g = 2 * p + e
            y = (jnp.dot(wt_ref[g], u_t(g), preferred_element_type=F32)
                 + lax.dot_general(wc_ref[g], hin, contract_cols, preferred_element_type=F32))
            y_ref[:, g] = y.astype(y_ref.dtype).reshape(S5_CHUNK, S5_GROUP_CH, rows)


def _s5_scan(u_t, w_toep, w_state, w_carry, a_pow, h0, y_buf, *, l, nseq, seq_len, row0):
    nt, ng, nh, _ = u_t.shape
    nc = seq_len // S5_CHUNK
    rows = nseq * nc
    rblk = row0 // rows
    gb = S5_GB
    kw = nt * nh
    w = (gb // 2) * LANE
    kern = functools.partial(_s5_kernel, nseq=nseq, nc=nc)
    st_block = (nseq, 4, 1, w)
    if h0 is None:
        h0 = jnp.zeros((nseq, 4, 1, ng * S5_STATE), F32)
        h0_spec = pl.BlockSpec(st_block, lambda j: (0, 0, 0, j))
    else:
        h0_spec = _lspec(l, st_block, lambda j: (0, 0, 0, j))
    args = [u_t, w_toep, w_state, w_carry, a_pow, h0]
    in_specs = [pl.BlockSpec((nt, gb, nh, rows), lambda j: (0, j, 0, rblk)),
                _lspec(l, (gb, kw, kw), lambda j: (j, 0, 0)),
                _lspec(l, (gb, kw, 4 * LANE), lambda j: (j, 0, 0)),
                _lspec(l, (gb, kw, 4 * LANE), lambda j: (j, 0, 0)),
                _lspec(l, (4, 1, w), lambda j: (0, 0, j)),
                h0_spec]
    kern, aliases = _in_place(kern, args, in_specs, {0: y_buf})
    return pl.pallas_call(
        kern,
        out_shape=(jax.ShapeDtypeStruct(y_buf.shape, y_buf.dtype),
                   jax.ShapeDtypeStruct((nseq, 4, 1, ng * S5_STATE), F32)),
        grid=(ng // gb,),
        in_specs=in_specs,
        out_specs=(pl.BlockSpec((nt, gb, nh, rows), lambda j: (0, j, 0, rblk)),
                   pl.BlockSpec((nseq, 4, 1, w), lambda j: (0, 0, 0, j))),
        scratch_shapes=[pltpu.VMEM((rows, 4 * w), F32), pltpu.VMEM((rows, 4 * w), F32)],
        input_output_aliases=aliases,
        compiler_params=_cparams(("parallel",)),
    )(*args)


def _s5_weights(lam_re, lam_im, log_step, b_re, b_im, c_re, c_im):
    t = S5_CHUNK
    ng, ns, nh = b_re.shape
    step = jnp.exp(log_step)[..., None]
    lr, li = lam_re * step, lam_im * step
    n = jnp.arange(t + 1, dtype=F32)[:, None, None, None]
    mag = jnp.exp(lr[None] * n)
    pw_re, pw_im = mag * jnp.cos(li[None] * n), mag * jnp.sin(li[None] * n)
    a_re, a_im = pw_re[1], pw_im[1]
    den = lam_re * lam_re + lam_im * lam_im
    k_re = ((a_re - 1.0) * lam_re + a_im * lam_im) / den
    k_im = (a_im * lam_re - (a_re - 1.0) * lam_im) / den
    bt_re, bt_im = b_re.transpose(0, 2, 1), b_im.transpose(0, 2, 1)
    w_re = k_re[:, :, None, :] * bt_re[None] - k_im[:, :, None, :] * bt_im[None]
    w_im = k_re[:, :, None, :] * bt_im[None] + k_im[:, :, None, :] * bt_re[None]

    def times_pow(idx, d, x_re, x_im):
        p_re, p_im = pw_re[idx, d][:, :, None, :], pw_im[idx, d][:, :, None, :]
        return p_re * x_re[None] - p_im * x_im[None], p_re * x_im[None] + p_im * x_re[None]

    ti = jnp.arange(t)
    kern = []
    for d in range(2):
        aw_re, aw_im = times_pow(ti, d, w_re[d], w_im[d])
        kern.append(jnp.einsum("gop,tgip->tgoi", c_re, aw_re, precision=HIGHEST)
                    - jnp.einsum("gop,tgip->tgoi", c_im, aw_im, precision=HIGHEST))
    k_all = jnp.concatenate([kern[0][:0:-1], (kern[0][0] + kern[1][0])[None], kern[1][1:]], axis=0)
    k_flat = k_all.transpose(1, 2, 0, 3).reshape(ng, nh, (2 * t - 1) * nh)
    toep_t = jnp.stack([k_flat[:, :, (t - 1 - to) * nh:(2 * t - 1 - to) * nh] for to in range(t)], axis=1)
    toep_t = toep_t.reshape(ng, t * nh, t * nh)

    side = jax.nn.one_hot(jnp.arange(ng) % 2, 2, dtype=F32)
    width = 4 * 2 * ns

    def lane_pow(p, idx_f, idx_b):
        x = jnp.stack([p[idx_f, 0], p[idx_f, 0], p[idx_b, 1], p[idx_b, 1]], axis=2)
        x = x[:, :, :, None, :] * side[None, :, None, :, None]
        return x.reshape(t, ng, 1, width).transpose(1, 0, 2, 3)

    def lane_coef(comps):
        x = jnp.stack(comps, axis=2)[:, :, :, None, :]
        return jnp.broadcast_to(x, (ng, nh, 4, 2, ns)).reshape(ng, 1, nh, width)

    def state_matrix(idx_f, idx_b, re_coef, im_coef):
        mat = (lane_pow(pw_re, idx_f, idx_b) * lane_coef(re_coef)
               + lane_pow(pw_im, idx_f, idx_b) * lane_coef(im_coef))
        return mat.reshape(ng, t * nh, width).astype(BF16)

    w_state = state_matrix(t - 1 - ti, ti, (w_re[0], w_im[0], w_re[1], w_im[1]),
                           (-w_im[0], w_re[0], -w_im[1], w_re[1]))
    carry_t = state_matrix(ti + 1, t - ti, (c_re, -c_im, c_re, -c_im), (-c_im, -c_re, -c_im, -c_re))
    a_pow = jnp.stack([pw_re[t, 0], pw_im[t, 0], pw_re[t, 1], pw_im[t, 1]], axis=0).reshape(4, 1, ng * ns)
    return toep_t.astype(BF16), w_state, carry_t, a_pow


def _glu_kernel(y_ref, u_ref, d_ref, wv_ref, wg_ref, o_ref, vs_ref):
    @pl.when(pl.program_id(1) == 0)
    def _():
        v = y_ref[...].astype(F32) + d_ref[...] * u_ref[...].astype(F32)
        v = 0.5 * v * (1.0 + jnp.tanh(math.sqrt(2.0 / math.pi) * (v + 0.044715 * (v * v * v))))
        vs_ref[...] = v.astype(BF16)

    v = vs_ref[...]
    a = jnp.dot(v, wv_ref[...], preferred_element_type=F32)
    b = jnp.dot(v, wg_ref[...], preferred_element_type=F32)
    o_ref[...] = (a * jax.nn.sigmoid(b)).astype(o_ref.dtype)


def _s5_glu(y, proj, d_skip, w_glu, l, dims, cols):
    m, wd = y.shape
    tm, tn = min(TM, dims["lat_len"]), 512
    nj = wd // tn
    return pl.pallas_call(
        _glu_kernel,
        out_shape=jax.ShapeDtypeStruct((m, wd), BF16),
        grid=(m // tm, nj),
        in_specs=[pl.BlockSpec((tm, wd), lambda i, j: (i, 0)),
                  pl.BlockSpec((tm, wd), lambda i, j: (i, cols["s5u"][0] // wd)),
                  _lspec(l, (1, wd), lambda i, j: (0, 0)),
                  _lspec(l, (wd, tn), lambda i, j: (0, j)),
                  _lspec(l, (wd, tn), lambda i, j: (0, j + nj))],
        out_specs=pl.BlockSpec((tm, tn), lambda i, j: (i, j)),
        scratch_shapes=[pltpu.VMEM((tm, wd), BF16)],
        compiler_params=_cparams(("parallel", "arbitrary")),
    )(y, proj, d_skip, w_glu, w_glu)


def _columns(d_model, branch_w):
    kvw = GQA_KV_HEADS * GQA_HEAD_DIM
    xbc = branch_w + 2 * SSD_GROUPS * SSD_STATE
    order = (("gate", N_BRANCH * d_model), ("gq", branch_w), ("sz", branch_w), ("s5u", branch_w),
             ("sxbc", xbc), ("mqd", MLA_Q_LORA), ("ckv", MLA_KV_LORA), ("gk", kvw), ("gv", kvw))
    cols, off = {}, 0
    for name, width in order:
        cols[name] = (off, width)
        off += width
    return cols, off


def _prep_w_in(w_in, d_model, branch_w, n_dt):
    kvw = GQA_KV_HEADS * GQA_HEAD_DIM
    xbc = branch_w + 2 * SSD_GROUPS * SSD_STATE
    splits = (N_BRANCH * d_model, branch_w, kvw, kvw, branch_w, xbc, n_dt, MLA_Q_LORA, MLA_KV_LORA + MLA_ROPE, branch_w)
    bounds, acc = [], 0
    for wd in splits[:-1]:
        acc += wd
        bounds.append(acc)
    gate, gq, gk, gv, sz, sxbc, sdt, mqd, mkvd, s5u = jnp.split(w_in, bounds, axis=-1)
    ckv, kpe = mkvd[..., :MLA_KV_LORA], mkvd[..., MLA_KV_LORA:]
    main = jnp.concatenate([gate, gq, sz, s5u, sxbc, mqd, ckv, gk, gv], axis=-1).astype(BF16)
    pad = jnp.zeros(w_in.shape[:-1] + (LANE - MLA_ROPE - n_dt,), w_in.dtype)
    small = jnp.concatenate([kpe, sdt, pad], axis=-1).astype(BF16)
    return main, small


def _prep_mla(w_uq, w_ukv):
    depth = w_uq.shape[0]
    qk = MLA_NOPE + MLA_ROPE
    wq = w_uq.reshape(depth, MLA_Q_LORA, MLA_HEADS, qk)
    wq = jnp.pad(wq, ((0, 0), (0, 0), (0, 0), (0, MLA_QK - qk))).reshape(depth, MLA_Q_LORA, MLA_HEADS * MLA_QK)
    wkv = w_ukv.reshape(depth, MLA_KV_LORA, MLA_HEADS, MLA_NOPE + LANE)
    k_nope, v = wkv[..., :MLA_NOPE], wkv[..., MLA_NOPE:]
    k_top = jnp.pad(k_nope, ((0, 0), (0, 0), (0, 0), (0, MLA_QK - MLA_NOPE)))
    eye = jnp.eye(LANE, MLA_QK, k=MLA_NOPE, dtype=w_ukv.dtype) * (jnp.arange(LANE) < MLA_ROPE)[:, None]
    k_bot = jnp.broadcast_to(eye[None, :, None, :], (depth, LANE, MLA_HEADS, MLA_QK))
    k_all = jnp.concatenate([k_top, k_bot], axis=1).reshape(depth, MLA_KV_IN, MLA_HEADS * MLA_QK)
    v_all = jnp.pad(v, ((0, 0), (0, LANE), (0, 0), (0, 0))).reshape(depth, MLA_KV_IN, MLA_HEADS * LANE)
    return wq.astype(BF16), jnp.concatenate([k_all, v_all], axis=-1).astype(BF16)


def kernel(x_prompt, x_sample, cache_gqa_k, cache_gqa_v, cache_mla_ckv, cache_mla_kpe, state_ssd, state_s5, c, c_ctx, norm1_g, norm2_g, w_mod, b_mod, w_in, gqa_qn_g, gqa_kn_g, ssd_conv_w, ssd_conv_b, ssd_a_log, ssd_dt_bias, ssd_d, ssd_norm_g, mla_qn_g, mla_w_uq, mla_kvn_g, mla_w_ukv, s5_lam_re, s5_lam_im, s5_log_step, s5_b_re, s5_b_im, s5_c_re, s5_c_im, s5_d, s5_w_glu, w_branch, w_out, w_ffn_in, w_ffn_out, final_g):
    nb_ctx, len_ctx, d_model = x_prompt.shape
    nb_lat, len_lat, _ = x_sample.shape
    depth = w_in.shape[0]
    past = cache_gqa_k.shape[2]
    branch_w = w_branch.shape[2]
    n_heads = ssd_d.shape[1]
    ctx_rows, lat_rows = nb_ctx * len_ctx, nb_lat * len_lat
    m = ctx_rows + lat_rows
    dims = {"ctx_rows": ctx_rows, "lat_len": len_lat}
    cols, _ = _columns(d_model, branch_w)
    kvw = GQA_KV_HEADS * GQA_HEAD_DIM
    n_s5 = branch_w // S5_GROUP_CH
    gw = branch_w // SSD_GROUPS

    row = lambda a: a.reshape(depth, 1, -1)
    col = lambda a: a.reshape(depth, -1, 1)
    w_main, w_small = _prep_w_in(w_in, d_model, branch_w, 2 * n_heads)
    w_q, w_kv = _prep_mla(mla_w_uq, mla_w_ukv)
    s5_toep, s5_state, s5_carry, s5_apow = jax.vmap(_s5_weights)(
        s5_lam_re, s5_lam_im, s5_log_step, s5_b_re, s5_b_im, s5_c_re, s5_c_im)
    tabs_a = _rope_tables(len_lat, GQA_HEAD_DIM)
    tabs_c = _rope_tables(len_lat, MLA_ROPE)
    n_mod = -(-(nb_lat + 1) // 8) * 8
    cvec = jnp.concatenate([c_ctx[None], c, jnp.zeros((n_mod - nb_lat - 1, d_model), F32)], axis=0)
    mod = _modulation(cvec, w_mod, b_mod).reshape(depth, n_mod, 6, 1, d_model)
    norm1_g, norm2_g = row(norm1_g), row(norm2_g)
    gqa_qn_g, gqa_kn_g, mla_qn_g, mla_kvn_g = row(gqa_qn_g), row(gqa_kn_g), row(mla_qn_g), row(mla_kvn_g)
    conv_b, ssd_norm_g, s5_d = row(ssd_conv_b), row(ssd_norm_g), row(s5_d)
    d_skip = row(jnp.repeat(ssd_d, SSD_HEAD_DIM, axis=-1))
    dt_bias, a_log = (row(ssd_dt_bias), col(ssd_dt_bias)), (row(ssd_a_log), col(ssd_a_log))
    w_glu, w_branch, w_out = s5_w_glu.astype(BF16), w_branch.astype(BF16), w_out.astype(BF16)
    w_ffn_in, w_ffn_out = w_ffn_in.astype(BF16), w_ffn_out.astype(BF16)
    cache_k = cache_gqa_k.reshape(nb_lat, depth, past, kvw).astype(BF16)
    cache_v = cache_gqa_v.reshape(nb_lat, depth, past, kvw).astype(BF16)
    zpad = jnp.zeros(cache_mla_kpe.shape[:-1] + (LANE - MLA_ROPE,), F32)
    cache_kv = jnp.concatenate([cache_mla_ckv, cache_mla_kpe, zpad], axis=-1).astype(BF16)
    h0_ssd = jnp.moveaxis(state_ssd, (1, 2), (0, 1)).reshape(depth, 2, nb_lat, SSD_GROUPS, gw, SSD_STATE)
    h0_s5 = jnp.moveaxis(state_s5, 1, 0).reshape(depth, nb_lat, 4, 1, n_s5 * S5_STATE)
    ssd_new = jax.ShapeDtypeStruct((nb_ctx, depth, 2, SSD_GROUPS, gw, SSD_STATE), F32)
    new_buf = lambda: jax.ShapeDtypeStruct((m, branch_w), BF16)
    gqa = dict(n_kv=GQA_KV_HEADS, n_rep=cols["gq"][1] // kvw, dk=GQA_HEAD_DIM, dv=GQA_HEAD_DIM)
    mla = dict(n_kv=MLA_HEADS, n_rep=1, dk=MLA_QK, dv=LANE, k_off=0, v_off=MLA_HEADS * MLA_QK)
    nchunk = m // S5_CHUNK

    x = jnp.concatenate([x_prompt.reshape(ctx_rows, d_model), x_sample.reshape(lat_rows, d_model)], axis=0)
    new = []
    for l in range(depth):
        proj, small = _in_proj(x, norm1_g, mod, w_main, w_small, l, dims)

        qn, kn, k_own = _gqa_prep(proj, gqa_qn_g, gqa_kn_g, None, l=l, row0=0, nrows=ctx_rows, seq_len=len_ctx,
                                  cols=cols)
        o_a = _attention(qn, kn.reshape(nb_ctx, len_ctx, kvw), proj.reshape(-1, len_ctx, proj.shape[1]), new_buf(),
                         nseq=nb_ctx, seq_len=len_ctx, tq=len_ctx, kv_per_step=GQA_KV_HEADS, k_off=0,
                         v_off=cols["gv"][0], row0=0, **gqa)
        v_own = proj[:ctx_rows, cols["gv"][0]:cols["gv"][0] + kvw]
        qn, kn = _gqa_prep(proj, gqa_qn_g, gqa_kn_g, tabs_a, l=l, row0=ctx_rows, nrows=lat_rows, seq_len=len_lat,
                           cols=cols)
        k_lat = jnp.concatenate([kn.reshape(nb_lat, len_lat, kvw), cache_k[:, l]], axis=1)
        v_lat = jnp.concatenate([proj[ctx_rows:, cols["gv"][0]:cols["gv"][0] + kvw].reshape(nb_lat, len_lat, kvw),
                                 cache_v[:, l]], axis=1)
        o_a = _attention(qn, k_lat, v_lat, o_a, nseq=nb_lat, seq_len=len_lat, tq=min(TQ_LAT, len_lat),
                         kv_per_step=1, k_off=0, v_off=0, row0=ctx_rows, **gqa)

        qdn, kv_in, ckv_own = _mla_prep(proj, small, mla_qn_g, mla_kvn_g, None, l=l, row0=0, nrows=ctx_rows,
                                        seq_len=len_ctx, cols=cols)
        kpe_own = small[:ctx_rows, :MLA_ROPE]
        kv = _mla_kv(kv_in, w_kv, l).reshape(nb_ctx, len_ctx, -1)
        o_c = _attention(_mla_q(qdn, w_q, None, l=l, seq_len=len_ctx), kv, kv, new_buf(),
                         nseq=nb_ctx, seq_len=len_ctx, tq=len_ctx, kv_per_step=MLA_HEADS, row0=0, **mla)
        qdn, kv_in = _mla_prep(proj, small, mla_qn_g, mla_kvn_g, tabs_c, l=l, row0=ctx_rows, nrows=lat_rows,
                               seq_len=len_lat, cols=cols)
        kv_in = jnp.concatenate([kv_in.reshape(nb_lat, len_lat, MLA_KV_IN), cache_kv[:, l]], axis=1)
        kv = _mla_kv(kv_in.reshape(-1, MLA_KV_IN), w_kv, l).reshape(nb_lat, len_lat + past, -1)
        o_c = _attention(_mla_q(qdn, w_q, tabs_c, l=l, seq_len=len_lat), kv, kv, o_c,
                         nseq=nb_lat, seq_len=len_lat, tq=min(TQ_LAT, len_lat), kv_per_step=MLA_LAT_HEADS_PER_STEP,
                         row0=ctx_rows, **mla)

        dt_col = small[:, MLA_ROPE:MLA_ROPE + 2 * n_heads]
        dt_row = dt_col.T
        o_b = new_buf()
        for nseq, seq_len, row0, h0 in ((nb_ctx, len_ctx, 0, None), (nb_lat, len_lat, ctx_rows, h0_ssd)):
            xbc = _ssd_conv(proj, ssd_conv_w, conv_b, l=l, seq0=row0 // seq_len, nseq=nseq, seq_len=seq_len,
                            cols=cols)
            common = dict(l=l, nseq=nseq, seq_len=seq_len, row0=row0, cols=cols)
            keep = h0 is None
            y_f, st = _ssd_pass(xbc, dt_col, dt_row, dt_bias, a_log, h0, None, None, ssd_new if keep else None,
                                d=0, **common)
            ssd_new = st if keep else ssd_new
            o_b, st = _ssd_pass(xbc, dt_col, dt_row, dt_bias, a_log, h0, (y_f, proj, d_skip, ssd_norm_g), o_b,
                                ssd_new if keep else None, d=1, **common)
            ssd_new = st if keep else ssd_new

        u = proj[:, cols["s5u"][0]:cols["s5u"][0] + branch_w]
        u_t = u.reshape(nchunk, S5_CHUNK * branch_w).T.reshape(S5_CHUNK, n_s5, S5_GROUP_CH, nchunk)
        s5_args = (u_t, s5_toep, s5_state, s5_carry, s5_apow)
        y_t, s5_new = _s5_scan(*s5_args, None, jax.ShapeDtypeStruct(u_t.shape, BF16), l=l, nseq=nb_ctx,
                               seq_len=len_ctx, row0=0)
        y_t, _ = _s5_scan(*s5_args, h0_s5, y_t, l=l, nseq=nb_lat, seq_len=len_lat, row0=ctx_rows // S5_CHUNK)
        y = y_t.reshape(S5_CHUNK * branch_w, nchunk).T.reshape(m, branch_w)
        o_d = _s5_glu(y, proj, s5_d, w_glu, l, dims, cols)

        mixed = _branch_mix([o_a, o_b, o_c, o_d], proj, w_branch, cols["gate"][0], l, dims)
        x = _resid_proj(mixed, w_out, x, mod, 2, 2 * TM, l, dims)
        hidden = _ffn_in(x, norm2_g, mod, w_ffn_in, l, dims)
        x = _resid_proj(hidden, w_ffn_out, x, mod, 5, TM, l, dims)
        new.append((k_own.reshape(nb_ctx, len_ctx, GQA_KV_HEADS, GQA_HEAD_DIM),
                    v_own.astype(F32).reshape(nb_ctx, len_ctx, GQA_KV_HEADS, GQA_HEAD_DIM),
                    ckv_own.reshape(nb_ctx, len_ctx, MLA_KV_LORA),
                    kpe_own.reshape(nb_ctx, len_ctx, MLA_ROPE),
                    s5_new.reshape(nb_ctx, 2, 2, n_s5, S5_STATE)))

    stacked = [jnp.stack([layer_out[i] for layer_out in new], axis=1) for i in range(5)]
    return (_final_norm(x, final_g, 0, ctx_rows).reshape(x_prompt.shape),
            _final_norm(x, final_g, ctx_rows, lat_rows).reshape(x_sample.shape),
            *stacked[:4],
            ssd_new.reshape(nb_ctx, depth, 2, n_heads, SSD_HEAD_DIM, SSD_STATE),
            stacked[4])
```

```python
import functools
import math

import jax
import jax.numpy as jnp
from jax import lax
from jax.experimental import pallas as pl
from jax.experimental.pallas import tpu as pltpu

F32 = jnp.float32
BF16 = jnp.bfloat16
HIGHEST = lax.Precision.HIGHEST

GRID_W = 64
N_BRANCH = 4
GQA_HEAD_DIM = 128
GQA_KV_HEADS = 2
SSD_HEAD_DIM = 64
SSD_GROUPS = 2
SSD_STATE = 128
SSD_CONV = 5
SSD_CHUNK = 128
MLA_HEADS = 8
MLA_NOPE = 128
MLA_ROPE = 64
MLA_Q_LORA = 512
MLA_KV_LORA = 256
S5_GROUP_CH = 16
S5_STATE = 64
S5_CHUNK = 16
ROPE_THETA = 10000.0
NORM_EPS = 1e-6
LANE = 128
VMEM_LIMIT = 56 * 1024 * 1024

TM = 1024
TN_IN = 1280
TN_FFN = 512
TN_OUT = 512
TR_ROWS = 1024
TQ_LAT = 512
MLA_LAT_HEADS_PER_STEP = 4
S5_GB = 8
SSD_SEQ_PER_STEP = 2


def _cparams(sem):
    return pltpu.CompilerParams(dimension_semantics=sem, vmem_limit_bytes=VMEM_LIMIT)


def _row_group(i, tm, ctx_rows, lat_len):
    nct = ctx_rows // tm
    per = lat_len // tm
    return jnp.where(i < nct, 0, 1 + (i - nct) // per)


def _lspec(l, block, index_map):
    return pl.BlockSpec((None,) + tuple(block), lambda *ids: (l,) + tuple(index_map(*ids)))


def _silu(x):
    return x * jax.nn.sigmoid(x)


def _softplus(x):
    return jnp.maximum(x, 0.0) + jnp.log(1.0 + jnp.exp(-jnp.abs(x)))


def _rms(x, g):
    ms = jnp.mean(x * x, axis=-1, keepdims=True)
    return x * lax.rsqrt(ms + NORM_EPS) * g


def _mod_kernel(c_ref, w_ref, b_ref, o_ref):
    c = c_ref[...]
    s = _silu(c).astype(BF16)
    o_ref[0] = jnp.dot(s, w_ref[0].astype(BF16), preferred_element_type=F32) + b_ref[0]


def _modulation(cvec, w_mod, b_mod):
    depth, d, n = w_mod.shape
    tn = 1024
    return pl.pallas_call(
        _mod_kernel,
        out_shape=jax.ShapeDtypeStruct((depth, 8, n), F32),
        grid=(depth, n // tn),
        in_specs=[pl.BlockSpec((8, d), lambda l, j: (0, 0)),
                  pl.BlockSpec((1, d, tn), lambda l, j: (l, 0, j)),
                  pl.BlockSpec((1, 1, tn), lambda l, j: (l, 0, j))],
        out_specs=pl.BlockSpec((1, 8, tn), lambda l, j: (l, 0, j)),
        compiler_params=_cparams(("parallel", "parallel")),
    )(cvec, w_mod, b_mod.reshape(depth, 1, n))


def _norm_mod(x_ref, g_ref, sc_ref, sh_ref, hs_ref):
    y = _rms(x_ref[...], g_ref[...])
    hs_ref[...] = (y * (1.0 + sc_ref[...]) + sh_ref[...]).astype(BF16)


def _in_proj_kernel(x_ref, g_ref, sc_ref, sh_ref, w_ref, ws_ref, o_ref, os_ref, hs_ref):
    @pl.when(pl.program_id(1) == 0)
    def _():
        _norm_mod(x_ref, g_ref, sc_ref, sh_ref, hs_ref)
        os_ref[...] = jnp.dot(hs_ref[...], ws_ref[...], preferred_element_type=F32)

    o_ref[...] = jnp.dot(hs_ref[...], w_ref[...], preferred_element_type=F32).astype(o_ref.dtype)


def _in_proj(x, norm_g, mod, w_main, w_small, l, dims):
    m, d = x.shape
    npad = w_main.shape[2]
    tm, tn = min(TM, dims["lat_len"]), TN_IN
    grp = lambda i: _row_group(i, tm, dims["ctx_rows"], dims["lat_len"])
    return pl.pallas_call(
        _in_proj_kernel,
        out_shape=(jax.ShapeDtypeStruct((m, npad), BF16), jax.ShapeDtypeStruct((m, LANE), F32)),
        grid=(m // tm, npad // tn),
        in_specs=[pl.BlockSpec((tm, d), lambda i, j: (i, 0)),
                  _lspec(l, (1, d), lambda i, j: (0, 0)),
                  _lspec(l, (None, None, 1, d), lambda i, j: (grp(i), 1, 0, 0)),
                  _lspec(l, (None, None, 1, d), lambda i, j: (grp(i), 0, 0, 0)),
                  _lspec(l, (d, tn), lambda i, j: (0, j)),
                  _lspec(l, (d, LANE), lambda i, j: (0, 0))],
        out_specs=(pl.BlockSpec((tm, tn), lambda i, j: (i, j)),
                   pl.BlockSpec((tm, LANE), lambda i, j: (i, 0))),
        scratch_shapes=[pltpu.VMEM((tm, d), BF16)],
        compiler_params=_cparams(("parallel", "arbitrary")),
    )(x, norm_g, mod, mod, w_main, w_small)


def _ffn_in_kernel(x_ref, g_ref, sc_ref, sh_ref, wg_ref, wu_ref, o_ref, hs_ref):
    @pl.when(pl.program_id(1) == 0)
    def _():
        _norm_mod(x_ref, g_ref, sc_ref, sh_ref, hs_ref)

    h = hs_ref[...]
    a = jnp.dot(h, wg_ref[...], preferred_element_type=F32)
    b = jnp.dot(h, wu_ref[...], preferred_element_type=F32)
    o_ref[...] = (_silu(a) * b).astype(o_ref.dtype)


def _ffn_in(x, norm_g, mod, w, l, dims):
    m, d = x.shape
    hid = w.shape[2] // 2
    tm, tn = min(TM, dims["lat_len"]), TN_FFN
    nj = hid // tn
    grp = lambda i: _row_group(i, tm, dims["ctx_rows"], dims["lat_len"])
    return pl.pallas_call(
        _ffn_in_kernel,
        out_shape=jax.ShapeDtypeStruct((m, hid), BF16),
        grid=(m // tm, nj),
        in_specs=[pl.BlockSpec((tm, d), lambda i, j: (i, 0)),
                  _lspec(l, (1, d), lambda i, j: (0, 0)),
                  _lspec(l, (None, None, 1, d), lambda i, j: (grp(i), 4, 0, 0)),
                  _lspec(l, (None, None, 1, d), lambda i, j: (grp(i), 3, 0, 0)),
                  _lspec(l, (d, tn), lambda i, j: (0, j)),
                  _lspec(l, (d, tn), lambda i, j: (0, j + nj))],
        out_specs=pl.BlockSpec((tm, tn), lambda i, j: (i, j)),
        scratch_shapes=[pltpu.VMEM((tm, d), BF16)],
        compiler_params=_cparams(("parallel", "arbitrary")),
    )(x, norm_g, mod, mod, w, w)


def _resid_kernel(x_ref, w_ref, r_ref, g_ref, o_ref):
    o_ref[...] = r_ref[...] + g_ref[...] * jnp.dot(x_ref[...], w_ref[...], preferred_element_type=F32)


def _resid_proj(xin, w, resid, mod, mod_idx, tm, l, dims):
    m, k = xin.shape
    n = w.shape[2]
    tm, tn = min(tm, dims["lat_len"]), TN_OUT
    grp = lambda i: _row_group(i, tm, dims["ctx_rows"], dims["lat_len"])
    return pl.pallas_call(
        _resid_kernel,
        out_shape=jax.ShapeDtypeStruct((m, n), F32),
        grid=(m // tm, n // tn),
        in_specs=[pl.BlockSpec((tm, k), lambda i, j: (i, 0)),
                  _lspec(l, (k, tn), lambda i, j: (0, j)),
                  pl.BlockSpec((tm, tn), lambda i, j: (i, j)),
                  _lspec(l, (None, None, 1, tn), lambda i, j: (grp(i), mod_idx, 0, j))],
        out_specs=pl.BlockSpec((tm, tn), lambda i, j: (i, j)),
        compiler_params=_cparams(("parallel", "arbitrary")),
    )(xin, w, resid, mod)


def _mix_kernel(oa_ref, ob_ref, oc_ref, od_ref, ga_ref, gb_ref, gc_ref, gd_ref, w_ref, o_ref):
    acc = None
    for n, (o_n, g_n) in enumerate(((oa_ref, ga_ref), (ob_ref, gb_ref), (oc_ref, gc_ref), (od_ref, gd_ref))):
        proj = jnp.dot(o_n[...], w_ref[n], preferred_element_type=F32)
        term = jax.nn.sigmoid(g_n[...].astype(F32)) * proj
        acc = term if acc is None else acc + term
    o_ref[...] = acc.astype(o_ref.dtype)


def _branch_mix(branches, proj_all, w_branch, col_gate, l, dims):
    m, bw = branches[0].shape
    d = w_branch.shape[3]
    tm, tn = min(TM, dims["lat_len"]), TN_OUT
    gate_specs = [pl.BlockSpec((tm, tn), functools.partial(lambda i, j, n: (i, (col_gate + n * d) // tn + j), n=n))
                  for n in range(N_BRANCH)]
    return pl.pallas_call(
        _mix_kernel,
        out_shape=jax.ShapeDtypeStruct((m, d), BF16),
        grid=(m // tm, d // tn),
        in_specs=[pl.BlockSpec((tm, bw), lambda i, j: (i, 0))] * N_BRANCH + gate_specs
                 + [_lspec(l, (N_BRANCH, bw, tn), lambda i, j: (0, 0, j))],
        out_specs=pl.BlockSpec((tm, tn), lambda i, j: (i, j)),
        compiler_params=_cparams(("parallel", "arbitrary")),
    )(*branches, proj_all, proj_all, proj_all, proj_all, w_branch)


def _final_norm_kernel(x_ref, g_ref, o_ref):
    o_ref[...] = _rms(x_ref[...], g_ref[...])


def _final_norm(x, g, row0, nrows):
    d = x.shape[1]
    tm = 512
    rb0 = row0 // tm
    return pl.pallas_call(
        _final_norm_kernel,
        out_shape=jax.ShapeDtypeStruct((nrows, d), F32),
        grid=(nrows // tm,),
        in_specs=[pl.BlockSpec((tm, d), lambda i: (rb0 + i, 0)), pl.BlockSpec((1, d), lambda i: (0, 0))],
        out_specs=pl.BlockSpec((tm, d), lambda i: (i, 0)),
        compiler_params=_cparams(("parallel",)),
    )(x, g.reshape(1, d))


def _rope_tables(seq_len, dim):
    nf = dim // 4
    t = jnp.arange(seq_len)
    row = (t // GRID_W).astype(F32)
    col = (t % GRID_W).astype(F32)
    inv = ROPE_THETA ** (-jnp.arange(nf, dtype=F32) / nf)
    ang_r, ang_c = row[:, None] * inv, col[:, None] * inv
    cos = jnp.concatenate([jnp.cos(ang_r), jnp.cos(ang_r), jnp.cos(ang_c), jnp.cos(ang_c)], axis=1)
    sin = jnp.concatenate([-jnp.sin(ang_r), jnp.sin(ang_r), -jnp.sin(ang_c), jnp.sin(ang_c)], axis=1)
    pad = LANE - dim
    if pad:
        cos = jnp.concatenate([cos, jnp.ones((seq_len, pad), F32)], axis=1)
        sin = jnp.concatenate([sin, jnp.zeros((seq_len, pad), F32)], axis=1)
    return cos, sin


def _rope(x, cos, sin, nf):
    lane = lax.broadcasted_iota(jnp.int32, x.shape, 1)
    swapped = jnp.where((lane % (2 * nf)) < nf, pltpu.roll(x, LANE - nf, 1), pltpu.roll(x, nf, 1))
    return x * cos + swapped * sin


def _attn_kernel(q_ref, k_ref, v_ref, o_ref, *, kv_per_step, n_rep, dk, dv):
    for a in range(kv_per_step):
        k = k_ref[0, :, a * dk:(a + 1) * dk]
        v = v_ref[0, :, a * dv:(a + 1) * dv]
        v_ones = jnp.concatenate([v, jnp.ones_like(v)], axis=1)
        for r in range(n_rep):
            h = a * n_rep + r
            q = q_ref[:, h * dk:(h + 1) * dk]
            s = lax.dot_general(q, k, (((1,), (1,)), ((), ())), preferred_element_type=F32)
            p = jnp.exp((s - jnp.max(s, axis=-1, keepdims=True)).astype(BF16))
            o = jnp.dot(p, v_ones, preferred_element_type=F32)
            o_ref[:, h * dv:(h + 1) * dv] = (o[:, :dv] / o[:, dv:]).astype(o_ref.dtype)


def _skip_inputs(kern, positions, *refs):
    return kern(*(r for i, r in enumerate(refs) if i not in positions))


def _in_place(kern, args, in_specs, out_bufs):
    aliases, skipped = {}, []
    for out_idx, buf in out_bufs.items():
        if buf is None or isinstance(buf, jax.ShapeDtypeStruct):
            continue
        in_specs.append(pl.BlockSpec(memory_space=pl.ANY))
        args.append(buf)
        aliases[len(args) - 1] = out_idx
        skipped.append(len(args) - 1)
    if skipped:
        kern = functools.partial(_skip_inputs, kern, tuple(skipped))
    return kern, aliases


def _attention(q, k, v, out_buf, *, nseq, seq_len, tq, n_kv, kv_per_step, n_rep, dk, dv, k_off, v_off, row0):
    lk = k.shape[1]
    nq = seq_len // tq
    rb0 = row0 // tq
    kw, vw = kv_per_step * dk, kv_per_step * dv
    assert k_off % kw == 0 and v_off % vw == 0 and n_kv % kv_per_step == 0
    kb0, vb0 = k_off // kw, v_off // vw
    kern = functools.partial(_attn_kernel, kv_per_step=kv_per_step, n_rep=n_rep, dk=dk, dv=dv)
    args, in_specs = [q, k, v], [pl.BlockSpec((tq, n_rep * kw), lambda b, g, i: (b * nq + i, g)),
                                 pl.BlockSpec((1, lk, kw), lambda b, g, i: (b, 0, kb0 + g)),
                                 pl.BlockSpec((1, lk, vw), lambda b, g, i: (b, 0, vb0 + g))]
    kern, aliases = _in_place(kern, args, in_specs, {0: out_buf})
    return pl.pallas_call(
        kern,
        out_shape=jax.ShapeDtypeStruct(out_buf.shape, out_buf.dtype),
        grid=(nseq, n_kv // kv_per_step, nq),
        in_specs=in_specs,
        out_specs=pl.BlockSpec((tq, n_rep * vw), lambda b, g, i: (rb0 + b * nq + i, g)),
        input_output_aliases=aliases,
        compiler_params=_cparams(("parallel", "parallel", "arbitrary")),
    )(*args)


def _gqa_prep_kernel(*refs, rope, scale, n_q, n_kv):
    if rope:
        q_ref, k_ref, qg_ref, kg_ref, cos_ref, sin_ref, qo_ref, ko_ref = refs
    else:
        q_ref, k_ref, qg_ref, kg_ref, qo_ref, ko_ref, k32_ref = refs
    hd = GQA_HEAD_DIM
    for h in range(n_q):
        y = _rms(q_ref[:, h * hd:(h + 1) * hd].astype(F32), qg_ref[...])
        if rope:
            y = _rope(y, cos_ref[...], sin_ref[...], hd // 4)
        qo_ref[:, h * hd:(h + 1) * hd] = (y * scale).astype(qo_ref.dtype)
    for h in range(n_kv):
        y = _rms(k_ref[:, h * hd:(h + 1) * hd].astype(F32), kg_ref[...])
        if rope:
            y = _rope(y, cos_ref[...], sin_ref[...], hd // 4)
        else:
            k32_ref[:, h * hd:(h + 1) * hd] = y
        ko_ref[:, h * hd:(h + 1) * hd] = y.astype(ko_ref.dtype)


def _gqa_prep(proj, qg, kg, tabs, *, l, row0, nrows, seq_len, cols):
    tr = TR_ROWS if tabs is None else min(TR_ROWS, seq_len)
    rb0 = row0 // tr
    qw, kw = cols["gq"][1], cols["gk"][1]
    n_q, n_kv = qw // GQA_HEAD_DIM, kw // GQA_HEAD_DIM
    rope = tabs is not None
    in_specs = [pl.BlockSpec((tr, qw), lambda i: (rb0 + i, cols["gq"][0] // qw)),
                pl.BlockSpec((tr, kw), lambda i: (rb0 + i, cols["gk"][0] // kw)),
                _lspec(l, (1, GQA_HEAD_DIM), lambda i: (0, 0)),
                _lspec(l, (1, GQA_HEAD_DIM), lambda i: (0, 0))]
    args = [proj, proj, qg, kg]
    out_shape = [jax.ShapeDtypeStruct((nrows, qw), BF16), jax.ShapeDtypeStruct((nrows, kw), BF16)]
    out_specs = [pl.BlockSpec((tr, qw), lambda i: (i, 0)), pl.BlockSpec((tr, kw), lambda i: (i, 0))]
    if rope:
        per = seq_len // tr
        in_specs += [pl.BlockSpec((tr, LANE), lambda i: (i % per, 0))] * 2
        args += list(tabs)
    else:
        out_shape.append(jax.ShapeDtypeStruct((nrows, kw), F32))
        out_specs.append(pl.BlockSpec((tr, kw), lambda i: (i, 0)))
    return pl.pallas_call(
        functools.partial(_gqa_prep_kernel, rope=rope, scale=GQA_HEAD_DIM ** -0.5, n_q=n_q, n_kv=n_kv),
        out_shape=tuple(out_shape),
        grid=(nrows // tr,),
        in_specs=in_specs,
        out_specs=tuple(out_specs),
        compiler_params=_cparams(("parallel",)),
    )(*args)


MLA_KV_IN = MLA_KV_LORA + LANE
MLA_QK = 2 * LANE


def _mla_prep_kernel(*refs, rope):
    if rope:
        qd_ref, ckv_ref, sm_ref, qg_ref, kvg_ref, cos_ref, sin_ref, qo_ref, kvo_ref = refs
    else:
        qd_ref, ckv_ref, sm_ref, qg_ref, kvg_ref, qo_ref, kvo_ref, ckv32_ref = refs
    qo_ref[...] = _rms(qd_ref[...].astype(F32), qg_ref[...]).astype(qo_ref.dtype)
    ckv = _rms(ckv_ref[...].astype(F32), kvg_ref[...])
    kvo_ref[:, :MLA_KV_LORA] = ckv.astype(kvo_ref.dtype)
    sm = sm_ref[...]
    if rope:
        sm = _rope(sm, cos_ref[...], sin_ref[...], MLA_ROPE // 4)
    else:
        ckv32_ref[...] = ckv
    lane = lax.broadcasted_iota(jnp.int32, sm.shape, 1)
    kvo_ref[:, MLA_KV_LORA:] = jnp.where(lane < MLA_ROPE, sm, 0.0).astype(kvo_ref.dtype)


def _mla_prep(proj, small, qg, kvg, tabs, *, l, row0, nrows, seq_len, cols):
    tr = TR_ROWS if tabs is None else min(TR_ROWS, seq_len)
    rb0 = row0 // tr
    qw, cw = cols["mqd"][1], cols["ckv"][1]
    rope = tabs is not None
    in_specs = [pl.BlockSpec((tr, qw), lambda i: (rb0 + i, cols["mqd"][0] // qw)),
                pl.BlockSpec((tr, cw), lambda i: (rb0 + i, cols["ckv"][0] // cw)),
                pl.BlockSpec((tr, LANE), lambda i: (rb0 + i, 0)),
                _lspec(l, (1, qw), lambda i: (0, 0)),
                _lspec(l, (1, cw), lambda i: (0, 0))]
    args = [proj, proj, small, qg, kvg]
    out_shape = [jax.ShapeDtypeStruct((nrows, qw), BF16), jax.ShapeDtypeStruct((nrows, MLA_KV_IN), BF16)]
    out_specs = [pl.BlockSpec((tr, qw), lambda i: (i, 0)), pl.BlockSpec((tr, MLA_KV_IN), lambda i: (i, 0))]
    if rope:
        per = seq_len // tr
        in_specs += [pl.BlockSpec((tr, LANE), lambda i: (i % per, 0))] * 2
        args += list(tabs)
    else:
        out_shape.append(jax.ShapeDtypeStruct((nrows, cw), F32))
        out_specs.append(pl.BlockSpec((tr, cw), lambda i: (i, 0)))
    return pl.pallas_call(
        functools.partial(_mla_prep_kernel, rope=rope),
        out_shape=tuple(out_shape),
        grid=(nrows // tr,),
        in_specs=in_specs,
        out_specs=tuple(out_specs),
        compiler_params=_cparams(("parallel",)),
    )(*args)


def _mla_q_kernel(*refs, rope, scale):
    if rope:
        x_ref, w_ref, cos_ref, sin_ref, o_ref = refs
    else:
        x_ref, w_ref, o_ref = refs
    acc = jnp.dot(x_ref[...], w_ref[...], preferred_element_type=F32) * scale
    if rope:
        o_ref[:, :LANE] = acc[:, :LANE].astype(o_ref.dtype)
        o_ref[:, LANE:] = _rope(acc[:, LANE:], cos_ref[...], sin_ref[...], MLA_ROPE // 4).astype(o_ref.dtype)
    else:
        o_ref[...] = acc.astype(o_ref.dtype)


def _mla_q(qdn, w_q, tabs, *, l, seq_len):
    nrows, k = qdn.shape
    rope = tabs is not None
    tm = min(TR_ROWS, seq_len) if rope else TR_ROWS
    in_specs = [pl.BlockSpec((tm, k), lambda i, h: (i, 0)), _lspec(l, (k, MLA_QK), lambda i, h: (0, h))]
    args = [qdn, w_q]
    if rope:
        per = seq_len // tm
        in_specs += [pl.BlockSpec((tm, LANE), lambda i, h: (i % per, 0))] * 2
        args += list(tabs)
    return pl.pallas_call(
        functools.partial(_mla_q_kernel, rope=rope, scale=(MLA_NOPE + MLA_ROPE) ** -0.5),
        out_shape=jax.ShapeDtypeStruct((nrows, MLA_HEADS * MLA_QK), BF16),
        grid=(nrows // tm, MLA_HEADS),
        in_specs=in_specs,
        out_specs=pl.BlockSpec((tm, MLA_QK), lambda i, h: (i, h)),
        compiler_params=_cparams(("parallel", "arbitrary")),
    )(*args)


def _matmul_kernel(x_ref, w_ref, o_ref):
    o_ref[...] = jnp.dot(x_ref[...], w_ref[...], preferred_element_type=F32).astype(o_ref.dtype)


def _mla_kv(kv_in, w_kv, l):
    nrows, k = kv_in.shape
    n = w_kv.shape[2]
    tm = 512 if nrows % 512 == 0 else 256
    return pl.pallas_call(
        _matmul_kernel,
        out_shape=jax.ShapeDtypeStruct((nrows, n), BF16),
        grid=(nrows // tm,),
        in_specs=[pl.BlockSpec((tm, k), lambda i: (i, 0)), _lspec(l, (k, n), lambda i: (0, 0))],
        out_specs=pl.BlockSpec((tm, n), lambda i: (i, 0)),
        compiler_params=_cparams(("parallel",)),
    )(kv_in, w_kv)


def _conv_kernel(x_ref, w_ref, b_ref, o_ref):
    x = x_ref[0].astype(F32)
    seq = x.shape[0]
    edge = 8
    row = lax.broadcasted_iota(jnp.int32, (edge, x.shape[1]), 0)
    pad = SSD_CONV // 2
    acc = x * w_ref[pad:pad + 1, :] + b_ref[...]
    for k in range(SSD_CONV):
        d = k - pad
        if d == 0:
            continue
        if d > 0:
            src = jnp.concatenate([jnp.where(row < d, 0.0, x[:edge]), x[edge:]], axis=0)
        else:
            src = jnp.concatenate([x[:seq - edge], jnp.where(row >= edge + d, 0.0, x[seq - edge:])], axis=0)
        acc = acc + pltpu.roll(src, (-d) % seq, 0) * w_ref[k:k + 1, :]
    o_ref[0] = _silu(acc).astype(o_ref.dtype)


def _ssd_conv(proj, conv_w, conv_b, *, l, seq0, nseq, seq_len, cols):
    npad = proj.shape[1]
    c0, cw = cols["sxbc"]
    tc = 256
    view = proj.reshape(-1, seq_len, npad)
    return pl.pallas_call(
        _conv_kernel,
        out_shape=jax.ShapeDtypeStruct((nseq, seq_len, cw), BF16),
        grid=(nseq, cw // tc),
        in_specs=[pl.BlockSpec((1, seq_len, tc), lambda s, c: (seq0 + s, 0, c0 // tc + c)),
                  _lspec(l, (SSD_CONV, tc), lambda s, c: (0, c)),
                  _lspec(l, (1, tc), lambda s, c: (0, c))],
        out_specs=pl.BlockSpec((1, seq_len, tc), lambda s, c: (s, 0, c)),
        compiler_params=_cparams(("parallel", "parallel")),
    )(view, conv_w, conv_b).reshape(nseq * seq_len, cw)


def _expand_heads(vals, expand):
    q = vals.shape[0]
    hi = vals.astype(BF16)
    lo = (vals - hi.astype(F32)).astype(BF16)
    out = jnp.dot(jnp.concatenate([hi, lo], axis=0), expand, preferred_element_type=F32)
    return out[:q] + out[q:]


def _ssd_kernel(*refs, reverse, zero_init, final, d):
    if final:
        (x_ref, b_ref, c_ref, dtc_ref, dtr_ref, biasc_ref, biasr_ref, alogc_ref, alogr_ref, h0_ref,
         yf_ref, z_ref, dskip_ref, ng_ref, y_ref, hout_ref, st_ref) = refs
    else:
        (x_ref, b_ref, c_ref, dtc_ref, dtr_ref, biasc_ref, biasr_ref, alogc_ref, alogr_ref, h0_ref,
         y_ref, hout_ref, st_ref) = refs
    q = SSD_CHUNK
    nseq = x_ref.shape[0]
    nh = dtc_ref.shape[2] // 2
    hpg = nh // SSD_GROUPS
    c = pl.program_id(1)

    @pl.when(c == 0)
    def _():
        for sb in range(nseq):
            for g in range(SSD_GROUPS):
                if zero_init:
                    st_ref[sb, g] = jnp.zeros(st_ref.shape[2:], F32)
                else:
                    st_ref[sb, g] = h0_ref[sb, g].T

    lo = d * nh
    ii = lax.broadcasted_iota(jnp.int32, (q, q), 0)
    jj = lax.broadcasted_iota(jnp.int32, (q, q), 1)
    causal = (jj >= ii) if reverse else (jj <= ii)
    tri = causal.astype(F32)
    tri_t = ((ii >= jj) if reverse else (ii <= jj)).astype(F32)
    lane = lax.broadcasted_iota(jnp.int32, (q, LANE), 1)
    lane_row = lax.broadcasted_iota(jnp.int32, (1, LANE), 1)
    neg_a_c = -jnp.exp(alogc_ref[:, lo:lo + nh])
    neg_a_r = -jnp.exp(alogr_ref[lo:lo + nh, :])
    head_of_lane = lax.broadcasted_iota(jnp.int32, (nh, nh * SSD_HEAD_DIM), 1) // SSD_HEAD_DIM
    expand = (lax.broadcasted_iota(jnp.int32, (nh, nh * SSD_HEAD_DIM), 0) == head_of_lane).astype(BF16)

    for sb in range(nseq):
        dt_c = _softplus(dtc_ref[sb, :, lo:lo + nh] + biasc_ref[:, lo:lo + nh])
        dt_r = _softplus(dtr_ref[sb, lo:lo + nh, :] + biasr_ref[lo:lo + nh, :])
        a_c = dt_c * neg_a_c
        a_r = dt_r * neg_a_r
        cum_c = jnp.dot(tri, a_c, precision=HIGHEST, preferred_element_type=F32)
        cum_r = jnp.dot(a_r, tri_t, precision=HIGHEST, preferred_element_type=F32)
        cum_dt_r = cum_r - jnp.log(dt_r)
        tot_c = jnp.sum(a_c, axis=0, keepdims=True)
        e_in = _expand_heads(jnp.exp(cum_c), expand)
        w_out = _expand_heads(dt_c * jnp.exp(tot_c - cum_c), expand)
        e_tot = jnp.exp(tot_c)

        x = x_ref[sb]
        y_groups = []
        for g in range(SSD_GROUPS):
            bg = b_ref[sb, :, g * SSD_STATE:(g + 1) * SSD_STATE]
            cg = c_ref[sb, :, g * SSD_STATE:(g + 1) * SSD_STATE]
            cb = lax.dot_general(cg, bg, (((1,), (1,)), ((), ())), preferred_element_type=F32)
            st = st_ref[sb, g]
            y_in = jnp.dot(cg, st.astype(BF16), preferred_element_type=F32)
            y_pairs, xs_pairs, dec_pairs = [], [], []
            for pr in range(hpg // 2):
                h0 = g * hpg + 2 * pr
                x_pair = x[:, h0 * SSD_HEAD_DIM:(h0 + 2) * SSD_HEAD_DIM]
                att = []
                for h in (h0, h0 + 1):
                    seg = cum_c[:, h:h + 1] - cum_dt_r[h:h + 1, :]
                    att.append((cb * jnp.exp(jnp.where(causal, seg, -jnp.inf))).astype(BF16))
                yd = jnp.dot(jnp.concatenate(att, axis=0), x_pair, preferred_element_type=F32)
                y_pair = jnp.where(lane < SSD_HEAD_DIM, yd[:q], yd[q:])
                pair = slice(h0 * SSD_HEAD_DIM, (h0 + 2) * SSD_HEAD_DIM)
                y_pair = y_pair + e_in[:, pair] * y_in[:, 2 * pr * SSD_HEAD_DIM:(2 * pr + 2) * SSD_HEAD_DIM]
                y_pairs.append(y_pair)
                xs_pairs.append((x_pair.astype(F32) * w_out[:, pair]).astype(BF16))
                dec_pairs.append(jnp.where(lane_row < SSD_HEAD_DIM, e_tot[:, h0:h0 + 1], e_tot[:, h0 + 1:h0 + 2]))
            xs_dec = jnp.concatenate(xs_pairs, axis=1)
            upd = lax.dot_general(bg, xs_dec, (((0,), (0,)), ((), ())), preferred_element_type=F32)
            st_ref[sb, g] = st * jnp.concatenate(dec_pairs, axis=1) + upd
            y_groups.append(jnp.concatenate(y_pairs, axis=1))
        y = jnp.concatenate(y_groups, axis=1)

        if final:
            y = y + yf_ref[sb] + dskip_ref[...] * x.astype(F32)
            y = y * _silu(z_ref[sb].astype(F32))
            y_ref[sb] = _rms(y, ng_ref[...]).astype(y_ref.dtype)
        else:
            y_ref[sb] = y

    @pl.when(c == pl.num_programs(1) - 1)
    def _():
        for sb in range(nseq):
            for g in range(SSD_GROUPS):
                hout_ref[sb, g] = st_ref[sb, g].T


def _ssd_pass(xbc, dt_col, dt_row, dt_bias, a_log, h0, extra, out_buf, state_buf, *, l, d, nseq, seq_len, row0,
              cols):
    q = SSD_CHUNK
    sb = SSD_SEQ_PER_STEP
    nc = seq_len // q
    nh2 = dt_col.shape[1]
    inner = cols["sz"][1]
    gw = inner // SSD_GROUPS
    bw = SSD_GROUPS * SSD_STATE
    assert nseq % sb == 0 and (row0 // seq_len) % sb == 0
    s0 = row0 // seq_len // sb
    reverse = d == 1
    final = extra is not None
    cidx = (lambda c: nc - 1 - c) if reverse else (lambda c: c)
    zero_init = h0 is None
    st_block = (sb, SSD_GROUPS, gw, SSD_STATE)
    if zero_init:
        h0 = jnp.zeros(st_block, F32)
        h0_spec = pl.BlockSpec(st_block, lambda s, c: (0, 0, 0, 0))
    else:
        h0_spec = pl.BlockSpec((None, None) + st_block, lambda s, c: (l, d, s, 0, 0, 0))
    per_seq = lambda a: a.reshape(-1, seq_len, a.shape[-1])
    xbc3 = per_seq(xbc)
    in_specs = [pl.BlockSpec((sb, q, inner), lambda s, c: (s, cidx(c), 0)),
                pl.BlockSpec((sb, q, bw), lambda s, c: (s, cidx(c), inner // bw)),
                pl.BlockSpec((sb, q, bw), lambda s, c: (s, cidx(c), inner // bw + 1)),
                pl.BlockSpec((sb, q, nh2), lambda s, c: (s0 + s, cidx(c), 0)),
                pl.BlockSpec((sb, nh2, q), lambda s, c: (s, 0, cidx(c))),
                _lspec(l, (1, nh2), lambda s, c: (0, 0)),
                _lspec(l, (nh2, 1), lambda s, c: (0, 0)),
                _lspec(l, (1, nh2), lambda s, c: (0, 0)),
                _lspec(l, (nh2, 1), lambda s, c: (0, 0)),
                h0_spec]
    args = [xbc3, xbc3, xbc3, per_seq(dt_col), dt_row, dt_bias[0], dt_bias[1], a_log[0], a_log[1], h0]
    kern = functools.partial(_ssd_kernel, reverse=reverse, zero_init=zero_init, final=final, d=d)
    if final:
        y_fwd, proj, d_skip, norm_g = extra
        in_specs += [pl.BlockSpec((sb, q, inner), lambda s, c: (s, cidx(c), 0)),
                     pl.BlockSpec((sb, q, inner), lambda s, c: (s0 + s, cidx(c), cols["sz"][0] // inner)),
                     _lspec(l, (1, inner), lambda s, c: (0, 0)),
                     _lspec(l, (1, inner), lambda s, c: (0, 0))]
        args += [y_fwd, per_seq(proj), d_skip, norm_g]
        out3 = (out_buf.shape[0] // seq_len, seq_len, inner)
        if isinstance(out_buf, jax.ShapeDtypeStruct):
            out_buf = jax.ShapeDtypeStruct(out3, out_buf.dtype)
        else:
            out_buf = out_buf.reshape(out3)
        y_shape = jax.ShapeDtypeStruct(out3, out_buf.dtype)
        y_spec = pl.BlockSpec((sb, q, inner), lambda s, c: (s0 + s, cidx(c), 0))
    else:
        y_shape = jax.ShapeDtypeStruct((nseq, seq_len, inner), F32)
        y_spec = pl.BlockSpec((sb, q, inner), lambda s, c: (s, cidx(c), 0))
    if state_buf is None:
        st_shape = jax.ShapeDtypeStruct((nseq,) + st_block[1:], F32)
        st_spec = pl.BlockSpec(st_block, lambda s, c: (s, 0, 0, 0))
    else:
        st_shape = jax.ShapeDtypeStruct(state_buf.shape, state_buf.dtype)
        st_spec = pl.BlockSpec((sb, None, None) + st_block[1:], lambda s, c: (s, l, d, 0, 0, 0))
    kern, aliases = _in_place(kern, args, in_specs, {0: out_buf if final else None, 1: state_buf})
    y, st = pl.pallas_call(
        kern,
        out_shape=(y_shape, st_shape),
        grid=(nseq // sb, nc),
        in_specs=in_specs,
        out_specs=(y_spec, st_spec),
        scratch_shapes=[pltpu.VMEM((sb, SSD_GROUPS, SSD_STATE, gw), F32)],
        input_output_aliases=aliases,
        compiler_params=_cparams(("parallel", "arbitrary")),
    )(*args)
    return (y.reshape(-1, inner) if final else y), st


def _s5_kernel(u_ref, wt_ref, ws_ref, wc_ref, a_ref, h0_ref, y_ref, hout_ref, s_scr, hin_scr, *, nseq, nc):
    gb = u_ref.shape[1]
    rows = u_ref.shape[3]
    depth = S5_CHUNK * S5_GROUP_CH
    npair = gb // 2
    w = npair * LANE
    contract_rows = (((0,), (0,)), ((), ()))
    contract_cols = (((1,), (1,)), ((), ()))

    def u_t(g):
        return u_ref[:, g].reshape(depth, rows)

    for p in range(npair):
        s = (lax.dot_general(u_t(2 * p), ws_ref[2 * p], contract_rows, preferred_element_type=F32)
             + lax.dot_general(u_t(2 * p + 1), ws_ref[2 * p + 1], contract_rows, preferred_element_type=F32))
        for comp in range(4):
            s_scr[:, comp * w + p * LANE:comp * w + (p + 1) * LANE] = s[:, comp * LANE:(comp + 1) * LANE]

    af_re, af_im, ab_re, ab_im = a_ref[0], a_ref[1], a_ref[2], a_ref[3]

    def one_sequence(sq, _):
        def step(i, carry):
            hf_re, hf_im, hb_re, hb_im = carry
            rf = sq * nc + i
            rb = sq * nc + nc - 1 - i
            hin_scr[pl.ds(rf, 1), 0:w] = hf_re
            hin_scr[pl.ds(rf, 1), w:2 * w] = hf_im
            hin_scr[pl.ds(rb, 1), 2 * w:3 * w] = hb_re
            hin_scr[pl.ds(rb, 1), 3 * w:4 * w] = hb_im
            sf_re = s_scr[pl.ds(rf, 1), 0:w]
            sf_im = s_scr[pl.ds(rf, 1), w:2 * w]
            sb_re = s_scr[pl.ds(rb, 1), 2 * w:3 * w]
            sb_im = s_scr[pl.ds(rb, 1), 3 * w:4 * w]
            return (af_re * hf_re - af_im * hf_im + sf_re,
                    af_re * hf_im + af_im * hf_re + sf_im,
                    ab_re * hb_re - ab_im * hb_im + sb_re,
                    ab_re * hb_im + ab_im * hb_re + sb_im)

        last = lax.fori_loop(0, nc, step, tuple(h0_ref[sq, comp] for comp in range(4)))
        for comp in range(4):
            hout_ref[sq, comp] = last[comp]
        return 0

    lax.fori_loop(0, nseq, one_sequence, 0)

    for p in range(npair):
        hin = jnp.concatenate([hin_scr[:, comp * w + p * LANE:comp * w + (p + 1) * LANE] for comp in range(4)],
                              axis=1).astype(BF16)
        for e in range(2):
            g = 2 * p + e
            y = (jnp.dot(wt_ref[g], u_t(g), preferred_element_type=F32)
                 + lax.dot_general(wc_ref[g], hin, contract_cols, preferred_element_type=F32))
            y_ref[:, g] = y.astype(y_ref.dtype).reshape(S5_CHUNK, S5_GROUP_CH, rows)


def _s5_scan(u_t, w_toep, w_state, w_carry, a_pow, h0, y_buf, *, l, nseq, seq_len, row0):
    nt, ng, nh, _ = u_t.shape
    nc = seq_len // S5_CHUNK
    rows = nseq * nc
    rblk = row0 // rows
    gb = S5_GB
    kw = nt * nh
    w = (gb // 2) * LANE
    kern = functools.partial(_s5_kernel, nseq=nseq, nc=nc)
    st_block = (nseq, 4, 1, w)
    if h0 is None:
        h0 = jnp.zeros((nseq, 4, 1, ng * S5_STATE), F32)
        h0_spec = pl.BlockSpec(st_block, lambda j: (0, 0, 0, j))
    else:
        h0_spec = _lspec(l, st_block, lambda j: (0, 0, 0, j))
    args = [u_t, w_toep, w_state, w_carry, a_pow, h0]
    in_specs = [pl.BlockSpec((nt, gb, nh, rows), lambda j: (0, j, 0, rblk)),
                _lspec(l, (gb, kw, kw), lambda j: (j, 0, 0)),
                _lspec(l, (gb, kw, 4 * LANE), lambda j: (j, 0, 0)),
                _lspec(l, (gb, kw, 4 * LANE), lambda j: (j, 0, 0)),
                _lspec(l, (4, 1, w), lambda j: (0, 0, j)),
                h0_spec]
    kern, aliases = _in_place(kern, args, in_specs, {0: y_buf})
    return pl.pallas_call(
        kern,
        out_shape=(jax.ShapeDtypeStruct(y_buf.shape, y_buf.dtype),
                   jax.ShapeDtypeStruct((nseq, 4, 1, ng * S5_STATE), F32)),
        grid=(ng // gb,),
        in_specs=in_specs,
        out_specs=(pl.BlockSpec((nt, gb, nh, rows), lambda j: (0, j, 0, rblk)),
                   pl.BlockSpec((nseq, 4, 1, w), lambda j: (0, 0, 0, j))),
        scratch_shapes=[pltpu.VMEM((rows, 4 * w), F32), pltpu.VMEM((rows, 4 * w), F32)],
        input_output_aliases=aliases,
        compiler_params=_cparams(("parallel",)),
    )(*args)


def _s5_weights(lam_re, lam_im, log_step, b_re, b_im, c_re, c_im):
    t = S5_CHUNK
    ng, ns, nh = b_re.shape
    step = jnp.exp(log_step)[..., None]
    lr, li = lam_re * step, lam_im * step
    n = jnp.arange(t + 1, dtype=F32)[:, None, None, None]
    mag = jnp.exp(lr[None] * n)
    pw_re, pw_im = mag * jnp.cos(li[None] * n), mag * jnp.sin(li[None] * n)
    a_re, a_im = pw_re[1], pw_im[1]
    den = lam_re * lam_re + lam_im * lam_im
    k_re = ((a_re - 1.0) * lam_re + a_im * lam_im) / den
    k_im = (a_im * lam_re - (a_re - 1.0) * lam_im) / den
    bt_re, bt_im = b_re.transpose(0, 2, 1), b_im.transpose(0, 2, 1)
    w_re = k_re[:, :, None, :] * bt_re[None] - k_im[:, :, None, :] * bt_im[None]
    w_im = k_re[:, :, None, :] * bt_im[None] + k_im[:, :, None, :] * bt_re[None]

    def times_pow(idx, d, x_re, x_im):
        p_re, p_im = pw_re[idx, d][:, :, None, :], pw_im[idx, d][:, :, None, :]
        return p_re * x_re[None] - p_im * x_im[None], p_re * x_im[None] + p_im * x_re[None]

    ti = jnp.arange(t)
    kern = []
    for d in range(2):
        aw_re, aw_im = times_pow(ti, d, w_re[d], w_im[d])
        kern.append(jnp.einsum("gop,tgip->tgoi", c_re, aw_re, precision=HIGHEST)
                    - jnp.einsum("gop,tgip->tgoi", c_im, aw_im, precision=HIGHEST))
    k_all = jnp.concatenate([kern[0][:0:-1], (kern[0][0] + kern[1][0])[None], kern[1][1:]], axis=0)
    k_flat = k_all.transpose(1, 2, 0, 3).reshape(ng, nh, (2 * t - 1) * nh)
    toep_t = jnp.stack([k_flat[:, :, (t - 1 - to) * nh:(2 * t - 1 - to) * nh] for to in range(t)], axis=1)
    toep_t = toep_t.reshape(ng, t * nh, t * nh)

    side = jax.nn.one_hot(jnp.arange(ng) % 2, 2, dtype=F32)
    width = 4 * 2 * ns

    def lane_pow(p, idx_f, idx_b):
        x = jnp.stack([p[idx_f, 0], p[idx_f, 0], p[idx_b, 1], p[idx_b, 1]], axis=2)
        x = x[:, :, :, None, :] * side[None, :, None, :, None]
        return x.reshape(t, ng, 1, width).transpose(1, 0, 2, 3)

    def lane_coef(comps):
        x = jnp.stack(comps, axis=2)[:, :, :, None, :]
        return jnp.broadcast_to(x, (ng, nh, 4, 2, ns)).reshape(ng, 1, nh, width)

    def state_matrix(idx_f, idx_b, re_coef, im_coef):
        mat = (lane_pow(pw_re, idx_f, idx_b) * lane_coef(re_coef)
               + lane_pow(pw_im, idx_f, idx_b) * lane_coef(im_coef))
        return mat.reshape(ng, t * nh, width).astype(BF16)

    w_state = state_matrix(t - 1 - ti, ti, (w_re[0], w_im[0], w_re[1], w_im[1]),
                           (-w_im[0], w_re[0], -w_im[1], w_re[1]))
    carry_t = state_matrix(ti + 1, t - ti, (c_re, -c_im, c_re, -c_im), (-c_im, -c_re, -c_im, -c_re))
    a_pow = jnp.stack([pw_re[t, 0], pw_im[t, 0], pw_re[t, 1], pw_im[t, 1]], axis=0).reshape(4, 1, ng * ns)
    return toep_t.astype(BF16), w_state, carry_t, a_pow


def _glu_kernel(y_ref, u_ref, d_ref, wv_ref, wg_ref, o_ref, vs_ref):
    @pl.when(pl.program_id(1) == 0)
    def _():
        v = y_ref[...].astype(F32) + d_ref[...] * u_ref[...].astype(F32)
        v = 0.5 * v * (1.0 + jnp.tanh(math.sqrt(2.0 / math.pi) * (v + 0.044715 * (v * v * v))))
        vs_ref[...] = v.astype(BF16)

    v = vs_ref[...]
    a = jnp.dot(v, wv_ref[...], preferred_element_type=F32)
    b = jnp.dot(v, wg_ref[...], preferred_element_type=F32)
    o_ref[...] = (a * jax.nn.sigmoid(b)).astype(o_ref.dtype)


def _s5_glu(y, proj, d_skip, w_glu, l, dims, cols):
    m, wd = y.shape
    tm, tn = min(TM, dims["lat_len"]), 512
    nj = wd // tn
    return pl.pallas_call(
        _glu_kernel,
        out_shape=jax.ShapeDtypeStruct((m, wd), BF16),
        grid=(m // tm, nj),
        in_specs=[pl.BlockSpec((tm, wd), lambda i, j: (i, 0)),
                  pl.BlockSpec((tm, wd), lambda i, j: (i, cols["s5u"][0] // wd)),
                  _lspec(l, (1, wd), lambda i, j: (0, 0)),
                  _lspec(l, (wd, tn), lambda i, j: (0, j)),
                  _lspec(l, (wd, tn), lambda i, j: (0, j + nj))],
        out_specs=pl.BlockSpec((tm, tn), lambda i, j: (i, j)),
        scratch_shapes=[pltpu.VMEM((tm, wd), BF16)],
        compiler_params=_cparams(("parallel", "arbitrary")),
    )(y, proj, d_skip, w_glu, w_glu)


def _columns(d_model, branch_w):
    kvw = GQA_KV_HEADS * GQA_HEAD_DIM
    xbc = branch_w + 2 * SSD_GROUPS * SSD_STATE
    order = (("gate", N_BRANCH * d_model), ("gq", branch_w), ("sz", branch_w), ("s5u", branch_w),
             ("sxbc", xbc), ("mqd", MLA_Q_LORA), ("ckv", MLA_KV_LORA), ("gk", kvw), ("gv", kvw))
    cols, off = {}, 0
    for name, width in order:
        cols[name] = (off, width)
        off += width
    return cols, off


def _prep_w_in(w_in, d_model, branch_w, n_dt):
    kvw = GQA_KV_HEADS * GQA_HEAD_DIM
    xbc = branch_w + 2 * SSD_GROUPS * SSD_STATE
    splits = (N_BRANCH * d_model, branch_w, kvw, kvw, branch_w, xbc, n_dt, MLA_Q_LORA, MLA_KV_LORA + MLA_ROPE, branch_w)
    bounds, acc = [], 0
    for wd in splits[:-1]:
        acc += wd
        bounds.append(acc)
    gate, gq, gk, gv, sz, sxbc, sdt, mqd, mkvd, s5u = jnp.split(w_in, bounds, axis=-1)
    ckv, kpe = mkvd[..., :MLA_KV_LORA], mkvd[..., MLA_KV_LORA:]
    main = jnp.concatenate([gate, gq, sz, s5u, sxbc, mqd, ckv, gk, gv], axis=-1).astype(BF16)
    pad = jnp.zeros(w_in.shape[:-1] + (LANE - MLA_ROPE - n_dt,), w_in.dtype)
    small = jnp.concatenate([kpe, sdt, pad], axis=-1).astype(BF16)
    return main, small


def _prep_mla(w_uq, w_ukv):
    depth = w_uq.shape[0]
    qk = MLA_NOPE + MLA_ROPE
    wq = w_uq.reshape(depth, MLA_Q_LORA, MLA_HEADS, qk)
    wq = jnp.pad(wq, ((0, 0), (0, 0), (0, 0), (0, MLA_QK - qk))).reshape(depth, MLA_Q_LORA, MLA_HEADS * MLA_QK)
    wkv = w_ukv.reshape(depth, MLA_KV_LORA, MLA_HEADS, MLA_NOPE + LANE)
    k_nope, v = wkv[..., :MLA_NOPE], wkv[..., MLA_NOPE:]
    k_top = jnp.pad(k_nope, ((0, 0), (0, 0), (0, 0), (0, MLA_QK - MLA_NOPE)))
    eye = jnp.eye(LANE, MLA_QK, k=MLA_NOPE, dtype=w_ukv.dtype) * (jnp.arange(LANE) < MLA_ROPE)[:, None]
    k_bot = jnp.broadcast_to(eye[None, :, None, :], (depth, LANE, MLA_HEADS, MLA_QK))
    k_all = jnp.concatenate([k_top, k_bot], axis=1).reshape(depth, MLA_KV_IN, MLA_HEADS * MLA_QK)
    v_all = jnp.pad(v, ((0, 0), (0, LANE), (0, 0), (0, 0))).reshape(depth, MLA_KV_IN, MLA_HEADS * LANE)
    return wq.astype(BF16), jnp.concatenate([k_all, v_all], axis=-1).astype(BF16)


def kernel(x_prompt, x_sample, cache_gqa_k, cache_gqa_v, cache_mla_ckv, cache_mla_kpe, state_ssd, state_s5, c, c_ctx, norm1_g, norm2_g, w_mod, b_mod, w_in, gqa_qn_g, gqa_kn_g, ssd_conv_w, ssd_conv_b, ssd_a_log, ssd_dt_bias, ssd_d, ssd_norm_g, mla_qn_g, mla_w_uq, mla_kvn_g, mla_w_ukv, s5_lam_re, s5_lam_im, s5_log_step, s5_b_re, s5_b_im, s5_c_re, s5_c_im, s5_d, s5_w_glu, w_branch, w_out, w_ffn_in, w_ffn_out, final_g):
    nb_ctx, len_ctx, d_model = x_prompt.shape
    nb_lat, len_lat, _ = x_sample.shape
    depth = w_in.shape[0]
    past = cache_gqa_k.shape[2]
    branch_w = w_branch.shape[2]
    n_heads = ssd_d.shape[1]
    ctx_rows, lat_rows = nb_ctx * len_ctx, nb_lat * len_lat
    m = ctx_rows + lat_rows
    dims = {"ctx_rows": ctx_rows, "lat_len": len_lat}
    cols, _ = _columns(d_model, branch_w)
    kvw = GQA_KV_HEADS * GQA_HEAD_DIM
    n_s5 = branch_w // S5_GROUP_CH
    gw = branch_w // SSD_GROUPS

    row = lambda a: a.reshape(depth, 1, -1)
    col = lambda a: a.reshape(depth, -1, 1)
    w_main, w_small = _prep_w_in(w_in, d_model, branch_w, 2 * n_heads)
    w_q, w_kv = _prep_mla(mla_w_uq, mla_w_ukv)
    s5_toep, s5_state, s5_carry, s5_apow = jax.vmap(_s5_weights)(
        s5_lam_re, s5_lam_im, s5_log_step, s5_b_re, s5_b_im, s5_c_re, s5_c_im)
    tabs_a = _rope_tables(len_lat, GQA_HEAD_DIM)
    tabs_c = _rope_tables(len_lat, MLA_ROPE)
    n_mod = -(-(nb_lat + 1) // 8) * 8
    cvec = jnp.concatenate([c_ctx[None], c, jnp.zeros((n_mod - nb_lat - 1, d_model), F32)], axis=0)
    mod = _modulation(cvec, w_mod, b_mod).reshape(depth, n_mod, 6, 1, d_model)
    norm1_g, norm2_g = row(norm1_g), row(norm2_g)
    gqa_qn_g, gqa_kn_g, mla_qn_g, mla_kvn_g = row(gqa_qn_g), row(gqa_kn_g), row(mla_qn_g), row(mla_kvn_g)
    conv_b, ssd_norm_g, s5_d = row(ssd_conv_b), row(ssd_norm_g), row(s5_d)
    d_skip = row(jnp.repeat(ssd_d, SSD_HEAD_DIM, axis=-1))
    dt_bias, a_log = (row(ssd_dt_bias), col(ssd_dt_bias)), (row(ssd_a_log), col(ssd_a_log))
    w_glu, w_branch, w_out = s5_w_glu.astype(BF16), w_branch.astype(BF16), w_out.astype(BF16)
    w_ffn_in, w_ffn_out = w_ffn_in.astype(BF16), w_ffn_out.astype(BF16)
    cache_k = cache_gqa_k.reshape(nb_lat, depth, past, kvw).astype(BF16)
    cache_v = cache_gqa_v.reshape(nb_lat, depth, past, kvw).astype(BF16)
    zpad = jnp.zeros(cache_mla_kpe.shape[:-1] + (LANE - MLA_ROPE,), F32)
    cache_kv = jnp.concatenate([cache_mla_ckv, cache_mla_kpe, zpad], axis=-1).astype(BF16)
    h0_ssd = jnp.moveaxis(state_ssd, (1, 2), (0, 1)).reshape(depth, 2, nb_lat, SSD_GROUPS, gw, SSD_STATE)
    h0_s5 = jnp.moveaxis(state_s5, 1, 0).reshape(depth, nb_lat, 4, 1, n_s5 * S5_STATE)
    ssd_new = jax.ShapeDtypeStruct((nb_ctx, depth, 2, SSD_GROUPS, gw, SSD_STATE), F32)
    new_buf = lambda: jax.ShapeDtypeStruct((m, branch_w), BF16)
    gqa = dict(n_kv=GQA_KV_HEADS, n_rep=cols["gq"][1] // kvw, dk=GQA_HEAD_DIM, dv=GQA_HEAD_DIM)
    mla = dict(n_kv=MLA_HEADS, n_rep=1, dk=MLA_QK, dv=LANE, k_off=0, v_off=MLA_HEADS * MLA_QK)
    nchunk = m // S5_CHUNK

    x = jnp.concatenate([x_prompt.reshape(ctx_rows, d_model), x_sample.reshape(lat_rows, d_model)], axis=0)
    new = []
    for l in range(depth):
        proj, small = _in_proj(x, norm1_g, mod, w_main, w_small, l, dims)

        qn, kn, k_own = _gqa_prep(proj, gqa_qn_g, gqa_kn_g, None, l=l, row0=0, nrows=ctx_rows, seq_len=len_ctx,
                                  cols=cols)
        o_a = _attention(qn, kn.reshape(nb_ctx, len_ctx, kvw), proj.reshape(-1, len_ctx, proj.shape[1]), new_buf(),
                         nseq=nb_ctx, seq_len=len_ctx, tq=len_ctx, kv_per_step=GQA_KV_HEADS, k_off=0,
                         v_off=cols["gv"][0], row0=0, **gqa)
        v_own = proj[:ctx_rows, cols["gv"][0]:cols["gv"][0] + kvw]
        qn, kn = _gqa_prep(proj, gqa_qn_g, gqa_kn_g, tabs_a, l=l, row0=ctx_rows, nrows=lat_rows, seq_len=len_lat,
                           cols=cols)
        k_lat = jnp.concatenate([kn.reshape(nb_lat, len_lat, kvw), cache_k[:, l]], axis=1)
        v_lat = jnp.concatenate([proj[ctx_rows:, cols["gv"][0]:cols["gv"][0] + kvw].reshape(nb_lat, len_lat, kvw),
                                 cache_v[:, l]], axis=1)
        o_a = _attention(qn, k_lat, v_lat, o_a, nseq=nb_lat, seq_len=len_lat, tq=min(TQ_LAT, len_lat),
                         kv_per_step=1, k_off=0, v_off=0, row0=ctx_rows, **gqa)

        qdn, kv_in, ckv_own = _mla_prep(proj, small, mla_qn_g, mla_kvn_g, None, l=l, row0=0, nrows=ctx_rows,
                                        seq_len=len_ctx, cols=cols)
        kpe_own = small[:ctx_rows, :MLA_ROPE]
        kv = _mla_kv(kv_in, w_kv, l).reshape(nb_ctx, len_ctx, -1)
        o_c = _attention(_mla_q(qdn, w_q, None, l=l, seq_len=len_ctx), kv, kv, new_buf(),
                         nseq=nb_ctx, seq_len=len_ctx, tq=len_ctx, kv_per_step=MLA_HEADS, row0=0, **mla)
        qdn, kv_in = _mla_prep(proj, small, mla_qn_g, mla_kvn_g, tabs_c, l=l, row0=ctx_rows, nrows=lat_rows,
                               seq_len=len_lat, cols=cols)
        kv_in = jnp.concatenate([kv_in.reshape(nb_lat, len_lat, MLA_KV_IN), cache_kv[:, l]], axis=1)
        kv = _mla_kv(kv_in.reshape(-1, MLA_KV_IN), w_kv, l).reshape(nb_lat, len_lat + past, -1)
        o_c = _attention(_mla_q(qdn, w_q, tabs_c, l=l, seq_len=len_lat), kv, kv, o_c,
                         nseq=nb_lat, seq_len=len_lat, tq=min(TQ_LAT, len_lat), kv_per_step=MLA_LAT_HEADS_PER_STEP,
                         row0=ctx_rows, **mla)

        dt_col = small[:, MLA_ROPE:MLA_ROPE + 2 * n_heads]
        o_b = new_buf()
        for nseq, seq_len, row0, h0 in ((nb_ctx, len_ctx, 0, None), (nb_lat, len_lat, ctx_rows, h0_ssd)):
            xbc = _ssd_conv(proj, ssd_conv_w, conv_b, l=l, seq0=row0 // seq_len, nseq=nseq, seq_len=seq_len,
                            cols=cols)
            dt_row = dt_col[row0:row0 + nseq * seq_len].reshape(nseq, seq_len, -1).transpose(0, 2, 1)
            common = dict(l=l, nseq=nseq, seq_len=seq_len, row0=row0, cols=cols)
            keep = h0 is None
            y_f, st = _ssd_pass(xbc, dt_col, dt_row, dt_bias, a_log, h0, None, None, ssd_new if keep else None,
                                d=0, **common)
            ssd_new = st if keep else ssd_new
            o_b, st = _ssd_pass(xbc, dt_col, dt_row, dt_bias, a_log, h0, (y_f, proj, d_skip, ssd_norm_g), o_b,
                                ssd_new if keep else None, d=1, **common)
            ssd_new = st if keep else ssd_new

        u = proj[:, cols["s5u"][0]:cols["s5u"][0] + branch_w]
        u_t = u.reshape(nchunk, S5_CHUNK * branch_w).T.reshape(S5_CHUNK, n_s5, S5_GROUP_CH, nchunk)
        s5_args = (u_t, s5_toep, s5_state, s5_carry, s5_apow)
        y_t, s5_new = _s5_scan(*s5_args, None, jax.ShapeDtypeStruct(u_t.shape, BF16), l=l, nseq=nb_ctx,
                               seq_len=len_ctx, row0=0)
        y_t, _ = _s5_scan(*s5_args, h0_s5, y_t, l=l, nseq=nb_lat, seq_len=len_lat, row0=ctx_rows // S5_CHUNK)
        y = y_t.reshape(S5_CHUNK * branch_w, nchunk).T.reshape(m, branch_w)
        o_d = _s5_glu(y, proj, s5_d, w_glu, l, dims, cols)

        mixed = _branch_mix([o_a, o_b, o_c, o_d], proj, w_branch, cols["gate"][0], l, dims)
        x = _resid_proj(mixed, w_out, x, mod, 2, 2 * TM, l, dims)
        hidden = _ffn_in(x, norm2_g, mod, w_ffn_in, l, dims)
        x = _resid_proj(hidden, w_ffn_out, x, mod, 5, TM, l, dims)
        new.append((k_own.reshape(nb_ctx, len_ctx, GQA_KV_HEADS, GQA_HEAD_DIM),
                    v_own.astype(F32).reshape(nb_ctx, len_ctx, GQA_KV_HEADS, GQA_HEAD_DIM),
                    ckv_own.reshape(nb_ctx, len_ctx, MLA_KV_LORA),
                    kpe_own.reshape(nb_ctx, len_ctx, MLA_ROPE),
                    s5_new.reshape(nb_ctx, 2, 2, n_s5, S5_STATE)))

    stacked = [jnp.stack([layer_out[i] for layer_out in new], axis=1) for i in range(5)]
    return (_final_norm(x, final_g, 0, ctx_rows).reshape(x_prompt.shape),
            _final_norm(x, final_g, ctx_rows, lat_rows).reshape(x_sample.shape),
            *stacked[:4],
            ssd_new.reshape(nb_ctx, depth, 2, n_heads, SSD_HEAD_DIM, SSD_STATE),
            stacked[4])
```

```python
import functools
import math

import jax
import jax.numpy as jnp
from jax import lax
from jax.experimental import pallas as pl
from jax.experimental.pallas import tpu as pltpu

F32 = jnp.float32
BF16 = jnp.bfloat16
HIGHEST = lax.Precision.HIGHEST

GRID_W = 64
N_BRANCH = 4
GQA_HEAD_DIM = 128
GQA_KV_HEADS = 2
SSD_HEAD_DIM = 64
SSD_GROUPS = 2
SSD_STATE = 128
SSD_CONV = 5
SSD_CHUNK = 128
MLA_HEADS = 8
MLA_NOPE = 128
MLA_ROPE = 64
MLA_Q_LORA = 512
MLA_KV_LORA = 256
S5_GROUP_CH = 16
S5_STATE = 64
S5_CHUNK = 16
ROPE_THETA = 10000.0
NORM_EPS = 1e-6
LANE = 128
VMEM_LIMIT = 56 * 1024 * 1024

TM = 1024
TN_IN = 1280
TN_FFN = 512
TN_OUT = 512
TR_ROWS = 1024
TQ_LAT = 512
MLA_LAT_HEADS_PER_STEP = 4
S5_GB = 8
SSD_SEQ_PER_STEP = 2


def _cparams(sem):
    return pltpu.CompilerParams(dimension_semantics=sem, vmem_limit_bytes=VMEM_LIMIT)


def _row_group(i, tm, ctx_rows, lat_len):
    nct = ctx_rows // tm
    per = lat_len // tm
    return jnp.where(i < nct, 0, 1 + (i - nct) // per)


def _lspec(l, block, index_map):
    return pl.BlockSpec((None,) + tuple(block), lambda *ids: (l,) + tuple(index_map(*ids)))


def _silu(x):
    return x * jax.nn.sigmoid(x)


def _softplus(x):
    return jnp.maximum(x, 0.0) + jnp.log(1.0 + jnp.exp(-jnp.abs(x)))


def _rms(x, g):
    ms = jnp.mean(x * x, axis=-1, keepdims=True)
    return x * lax.rsqrt(ms + NORM_EPS) * g


def _mod_kernel(c_ref, w_ref, b_ref, o_ref):
    c = c_ref[...]
    s = _silu(c).astype(BF16)
    o_ref[0] = jnp.dot(s, w_ref[0].astype(BF16), preferred_element_type=F32) + b_ref[0]


def _modulation(cvec, w_mod, b_mod):
    depth, d, n = w_mod.shape
    tn = 1024
    return pl.pallas_call(
        _mod_kernel,
        out_shape=jax.ShapeDtypeStruct((depth, 8, n), F32),
        grid=(depth, n // tn),
        in_specs=[pl.BlockSpec((8, d), lambda l, j: (0, 0)),
                  pl.BlockSpec((1, d, tn), lambda l, j: (l, 0, j)),
                  pl.BlockSpec((1, 1, tn), lambda l, j: (l, 0, j))],
        out_specs=pl.BlockSpec((1, 8, tn), lambda l, j: (l, 0, j)),
        compiler_params=_cparams(("parallel", "parallel")),
    )(cvec, w_mod, b_mod.reshape(depth, 1, n))


def _norm_mod(x_ref, g_ref, sc_ref, sh_ref, hs_ref):
    y = _rms(x_ref[...], g_ref[...])
    hs_ref[...] = (y * (1.0 + sc_ref[...]) + sh_ref[...]).astype(BF16)


def _in_proj_kernel(x_ref, g_ref, sc_ref, sh_ref, w_ref, ws_ref, o_ref, os_ref, hs_ref):
    @pl.when(pl.program_id(1) == 0)
    def _():
        _norm_mod(x_ref, g_ref, sc_ref, sh_ref, hs_ref)
        os_ref[...] = jnp.dot(hs_ref[...], ws_ref[...], preferred_element_type=F32)

    o_ref[...] = jnp.dot(hs_ref[...], w_ref[...], preferred_element_type=F32).astype(o_ref.dtype)


def _in_proj(x, norm_g, mod, w_main, w_small, l, dims):
    m, d = x.shape
    npad = w_main.shape[2]
    tm, tn = min(TM, dims["lat_len"]), TN_IN
    grp = lambda i: _row_group(i, tm, dims["ctx_rows"], dims["lat_len"])
    return pl.pallas_call(
        _in_proj_kernel,
        out_shape=(jax.ShapeDtypeStruct((m, npad), BF16), jax.ShapeDtypeStruct((m, LANE), F32)),
        grid=(m // tm, npad // tn),
        in_specs=[pl.BlockSpec((tm, d), lambda i, j: (i, 0)),
                  _lspec(l, (1, d), lambda i, j: (0, 0)),
                  _lspec(l, (None, None, 1, d), lambda i, j: (grp(i), 1, 0, 0)),
                  _lspec(l, (None, None, 1, d), lambda i, j: (grp(i), 0, 0, 0)),
                  _lspec(l, (d, tn), lambda i, j: (0, j)),
                  _lspec(l, (d, LANE), lambda i, j: (0, 0))],
        out_specs=(pl.BlockSpec((tm, tn), lambda i, j: (i, j)),
                   pl.BlockSpec((tm, LANE), lambda i, j: (i, 0))),
        scratch_shapes=[pltpu.VMEM((tm, d), BF16)],
        compiler_params=_cparams(("parallel", "arbitrary")),
    )(x, norm_g, mod, mod, w_main, w_small)


def _ffn_in_kernel(x_ref, g_ref, sc_ref, sh_ref, wg_ref, wu_ref, o_ref, hs_ref):
    @pl.when(pl.program_id(1) == 0)
    def _():
        _norm_mod(x_ref, g_ref, sc_ref, sh_ref, hs_ref)

    h = hs_ref[...]
    a = jnp.dot(h, wg_ref[...], preferred_element_type=F32)
    b = jnp.dot(h, wu_ref[...], preferred_element_type=F32)
    o_ref[...] = (_silu(a) * b).astype(o_ref.dtype)


def _ffn_in(x, norm_g, mod, w, l, dims):
    m, d = x.shape
    hid = w.shape[2] // 2
    tm, tn = min(TM, dims["lat_len"]), TN_FFN
    nj = hid // tn
    grp = lambda i: _row_group(i, tm, dims["ctx_rows"], dims["lat_len"])
    return pl.pallas_call(
        _ffn_in_kernel,
        out_shape=jax.ShapeDtypeStruct((m, hid), BF16),
        grid=(m // tm, nj),
        in_specs=[pl.BlockSpec((tm, d), lambda i, j: (i, 0)),
                  _lspec(l, (1, d), lambda i, j: (0, 0)),
                  _lspec(l, (None, None, 1, d), lambda i, j: (grp(i), 4, 0, 0)),
                  _lspec(l, (None, None, 1, d), lambda i, j: (grp(i), 3, 0, 0)),
                  _lspec(l, (d, tn), lambda i, j: (0, j)),
                  _lspec(l, (d, tn), lambda i, j: (0, j + nj))],
        out_specs=pl.BlockSpec((tm, tn), lambda i, j: (i, j)),
        scratch_shapes=[pltpu.VMEM((tm, d), BF16)],
        compiler_params=_cparams(("parallel", "arbitrary")),
    )(x, norm_g, mod, mod, w, w)


def _resid_kernel(x_ref, w_ref, r_ref, g_ref, o_ref):
    o_ref[...] = r_ref[...] + g_ref[...] * jnp.dot(x_ref[...], w_ref[...], preferred_element_type=F32)


def _resid_proj(xin, w, resid, mod, mod_idx, tm, l, dims):
    m, k = xin.shape
    n = w.shape[2]
    tm, tn = min(tm, dims["lat_len"]), TN_OUT
    grp = lambda i: _row_group(i, tm, dims["ctx_rows"], dims["lat_len"])
    return pl.pallas_call(
        _resid_kernel,
        out_shape=jax.ShapeDtypeStruct((m, n), F32),
        grid=(m // tm, n // tn),
        in_specs=[pl.BlockSpec((tm, k), lambda i, j: (i, 0)),
                  _lspec(l, (k, tn), lambda i, j: (0, j)),
                  pl.BlockSpec((tm, tn), lambda i, j: (i, j)),
                  _lspec(l, (None, None, 1, tn), lambda i, j: (grp(i), mod_idx, 0, j))],
        out_specs=pl.BlockSpec((tm, tn), lambda i, j: (i, j)),
        compiler_params=_cparams(("parallel", "arbitrary")),
    )(xin, w, resid, mod)


def _mix_kernel(oa_ref, ob_ref, oc_ref, od_ref, ga_ref, gb_ref, gc_ref, gd_ref, w_ref, o_ref):
    acc = None
    for n, (o_n, g_n) in enumerate(((oa_ref, ga_ref), (ob_ref, gb_ref), (oc_ref, gc_ref), (od_ref, gd_ref))):
        proj = jnp.dot(o_n[...], w_ref[n], preferred_element_type=F32)
        term = jax.nn.sigmoid(g_n[...].astype(F32)) * proj
        acc = term if acc is None else acc + term
    o_ref[...] = acc.astype(o_ref.dtype)


def _branch_mix(branches, proj_all, w_branch, col_gate, l, dims):
    m, bw = branches[0].shape
    d = w_branch.shape[3]
    tm, tn = min(TM, dims["lat_len"]), TN_OUT
    gate_specs = [pl.BlockSpec((tm, tn), functools.partial(lambda i, j, n: (i, (col_gate + n * d) // tn + j), n=n))
                  for n in range(N_BRANCH)]
    return pl.pallas_call(
        _mix_kernel,
        out_shape=jax.ShapeDtypeStruct((m, d), BF16),
        grid=(m // tm, d // tn),
        in_specs=[pl.BlockSpec((tm, bw), lambda i, j: (i, 0))] * N_BRANCH + gate_specs
                 + [_lspec(l, (N_BRANCH, bw, tn), lambda i, j: (0, 0, j))],
        out_specs=pl.BlockSpec((tm, tn), lambda i, j: (i, j)),
        compiler_params=_cparams(("parallel", "arbitrary")),
    )(*branches, proj_all, proj_all, proj_all, proj_all, w_branch)


def _final_norm_kernel(x_ref, g_ref, o_ref):
    o_ref[...] = _rms(x_ref[...], g_ref[...])


def _final_norm(x, g, row0, nrows):
    d = x.shape[1]
    tm = 512
    rb0 = row0 // tm
    return pl.pallas_call(
        _final_norm_kernel,
        out_shape=jax.ShapeDtypeStruct((nrows, d), F32),
        grid=(nrows // tm,),
        in_specs=[pl.BlockSpec((tm, d), lambda i: (rb0 + i, 0)), pl.BlockSpec((1, d), lambda i: (0, 0))],
        out_specs=pl.BlockSpec((tm, d), lambda i: (i, 0)),
        compiler_params=_cparams(("parallel",)),
    )(x, g.reshape(1, d))


def _rope_tables(seq_len, dim):
    nf = dim // 4
    t = jnp.arange(seq_len)
    row = (t // GRID_W).astype(F32)
    col = (t % GRID_W).astype(F32)
    inv = ROPE_THETA ** (-jnp.arange(nf, dtype=F32) / nf)
    ang_r, ang_c = row[:, None] * inv, col[:, None] * inv
    cos = jnp.concatenate([jnp.cos(ang_r), jnp.cos(ang_r), jnp.cos(ang_c), jnp.cos(ang_c)], axis=1)
    sin = jnp.concatenate([-jnp.sin(ang_r), jnp.sin(ang_r), -jnp.sin(ang_c), jnp.sin(ang_c)], axis=1)
    pad = LANE - dim
    if pad:
        cos = jnp.concatenate([cos, jnp.ones((seq_len, pad), F32)], axis=1)
        sin = jnp.concatenate([sin, jnp.zeros((seq_len, pad), F32)], axis=1)
    return cos, sin


def _rope(x, cos, sin, nf):
    lane = lax.broadcasted_iota(jnp.int32, x.shape, 1)
    swapped = jnp.where((lane % (2 * nf)) < nf, pltpu.roll(x, LANE - nf, 1), pltpu.roll(x, nf, 1))
    return x * cos + swapped * sin


def _attn_kernel(q_ref, k_ref, v_ref, o_ref, *, kv_per_step, n_rep, dk, dv):
    for a in range(kv_per_step):
        k = k_ref[0, :, a * dk:(a + 1) * dk]
        v = v_ref[0, :, a * dv:(a + 1) * dv]
        v_ones = jnp.concatenate([v, jnp.ones_like(v)], axis=1)
        for r in range(n_rep):
            h = a * n_rep + r
            q = q_ref[:, h * dk:(h + 1) * dk]
            s = lax.dot_general(q, k, (((1,), (1,)), ((), ())), preferred_element_type=F32)
            p = jnp.exp((s - jnp.max(s, axis=-1, keepdims=True)).astype(BF16))
            o = jnp.dot(p, v_ones, preferred_element_type=F32)
            o_ref[:, h * dv:(h + 1) * dv] = (o[:, :dv] / o[:, dv:]).astype(o_ref.dtype)


def _skip_inputs(kern, positions, *refs):
    return kern(*(r for i, r in enumerate(refs) if i not in positions))


def _in_place(kern, args, in_specs, out_bufs):
    aliases, skipped = {}, []
    for out_idx, buf in out_bufs.items():
        if buf is None or isinstance(buf, jax.ShapeDtypeStruct):
            continue
        in_specs.append(pl.BlockSpec(memory_space=pl.ANY))
        args.append(buf)
        aliases[len(args) - 1] = out_idx
        skipped.append(len(args) - 1)
    if skipped:
        kern = functools.partial(_skip_inputs, kern, tuple(skipped))
    return kern, aliases


def _attention(q, k, v, out_buf, *, nseq, seq_len, tq, n_kv, kv_per_step, n_rep, dk, dv, k_off, v_off, row0):
    lk = k.shape[1]
    nq = seq_len // tq
    rb0 = row0 // tq
    kw, vw = kv_per_step * dk, kv_per_step * dv
    assert k_off % kw == 0 and v_off % vw == 0 and n_kv % kv_per_step == 0
    kb0, vb0 = k_off // kw, v_off // vw
    kern = functools.partial(_attn_kernel, kv_per_step=kv_per_step, n_rep=n_rep, dk=dk, dv=dv)
    args, in_specs = [q, k, v], [pl.BlockSpec((tq, n_rep * kw), lambda b, g, i: (b * nq + i, g)),
                                 pl.BlockSpec((1, lk, kw), lambda b, g, i: (b, 0, kb0 + g)),
                                 pl.BlockSpec((1, lk, vw), lambda b, g, i: (b, 0, vb0 + g))]
    kern, aliases = _in_place(kern, args, in_specs, {0: out_buf})
    return pl.pallas_call(
        kern,
        out_shape=jax.ShapeDtypeStruct(out_buf.shape, out_buf.dtype),
        grid=(nseq, n_kv // kv_per_step, nq),
        in_specs=in_specs,
        out_specs=pl.BlockSpec((tq, n_rep * vw), lambda b, g, i: (rb0 + b * nq + i, g)),
        input_output_aliases=aliases,
        compiler_params=_cparams(("parallel", "parallel", "arbitrary")),
    )(*args)


def _gqa_prep_kernel(*refs, rope, scale, n_q, n_kv):
    if rope:
        q_ref, k_ref, qg_ref, kg_ref, cos_ref, sin_ref, qo_ref, ko_ref = refs
    else:
        q_ref, k_ref, qg_ref, kg_ref, qo_ref, ko_ref, k32_ref = refs
    hd = GQA_HEAD_DIM
    for h in range(n_q):
        y = _rms(q_ref[:, h * hd:(h + 1) * hd].astype(F32), qg_ref[...])
        if rope:
            y = _rope(y, cos_ref[...], sin_ref[...], hd // 4)
        qo_ref[:, h * hd:(h + 1) * hd] = (y * scale).astype(qo_ref.dtype)
    for h in range(n_kv):
        y = _rms(k_ref[:, h * hd:(h + 1) * hd].astype(F32), kg_ref[...])
        if rope:
            y = _rope(y, cos_ref[...], sin_ref[...], hd // 4)
        else:
            k32_ref[:, h * hd:(h + 1) * hd] = y
        ko_ref[:, h * hd:(h + 1) * hd] = y.astype(ko_ref.dtype)


def _gqa_prep(proj, qg, kg, tabs, *, l, row0, nrows, seq_len, cols):
    tr = TR_ROWS if tabs is None else min(TR_ROWS, seq_len)
    rb0 = row0 // tr
    qw, kw = cols["gq"][1], cols["gk"][1]
    n_q, n_kv = qw // GQA_HEAD_DIM, kw // GQA_HEAD_DIM
    rope = tabs is not None
    in_specs = [pl.BlockSpec((tr, qw), lambda i: (rb0 + i, cols["gq"][0] // qw)),
                pl.BlockSpec((tr, kw), lambda i: (rb0 + i, cols["gk"][0] // kw)),
                _lspec(l, (1, GQA_HEAD_DIM), lambda i: (0, 0)),
                _lspec(l, (1, GQA_HEAD_DIM), lambda i: (0, 0))]
    args = [proj, proj, qg, kg]
    out_shape = [jax.ShapeDtypeStruct((nrows, qw), BF16), jax.ShapeDtypeStruct((nrows, kw), BF16)]
    out_specs = [pl.BlockSpec((tr, qw), lambda i: (i, 0)), pl.BlockSpec((tr, kw), lambda i: (i, 0))]
    if rope:
        per = seq_len // tr
        in_specs += [pl.BlockSpec((tr, LANE), lambda i: (i % per, 0))] * 2
        args += list(tabs)
    else:
        out_shape.append(jax.ShapeDtypeStruct((nrows, kw), F32))
        out_specs.append(pl.BlockSpec((tr, kw), lambda i: (i, 0)))
    return pl.pallas_call(
        functools.partial(_gqa_prep_kernel, rope=rope, scale=GQA_HEAD_DIM ** -0.5, n_q=n_q, n_kv=n_kv),
        out_shape=tuple(out_shape),
        grid=(nrows // tr,),
        in_specs=in_specs,
        out_specs=tuple(out_specs),
        compiler_params=_cparams(("parallel",)),
    )(*args)


MLA_KV_IN = MLA_KV_LORA + LANE
MLA_QK = 2 * LANE


def _mla_prep_kernel(*refs, rope):
    if rope:
        qd_ref, ckv_ref, sm_ref, qg_ref, kvg_ref, cos_ref, sin_ref, qo_ref, kvo_ref = refs
    else:
        qd_ref, ckv_ref, sm_ref, qg_ref, kvg_ref, qo_ref, kvo_ref, ckv32_ref = refs
    qo_ref[...] = _rms(qd_ref[...].astype(F32), qg_ref[...]).astype(qo_ref.dtype)
    ckv = _rms(ckv_ref[...].astype(F32), kvg_ref[...])
    kvo_ref[:, :MLA_KV_LORA] = ckv.astype(kvo_ref.dtype)
    sm = sm_ref[...]
    if rope:
        sm = _rope(sm, cos_ref[...], sin_ref[...], MLA_ROPE // 4)
    else:
        ckv32_ref[...] = ckv
    lane = lax.broadcasted_iota(jnp.int32, sm.shape, 1)
    kvo_ref[:, MLA_KV_LORA:] = jnp.where(lane < MLA_ROPE, sm, 0.0).astype(kvo_ref.dtype)


def _mla_prep(proj, small, qg, kvg, tabs, *, l, row0, nrows, seq_len, cols):
    tr = TR_ROWS if tabs is None else min(TR_ROWS, seq_len)
    rb0 = row0 // tr
    qw, cw = cols["mqd"][1], cols["ckv"][1]
    rope = tabs is not None
    in_specs = [pl.BlockSpec((tr, qw), lambda i: (rb0 + i, cols["mqd"][0] // qw)),
                pl.BlockSpec((tr, cw), lambda i: (rb0 + i, cols["ckv"][0] // cw)),
                pl.BlockSpec((tr, LANE), lambda i: (rb0 + i, 0)),
                _lspec(l, (1, qw), lambda i: (0, 0)),
                _lspec(l, (1, cw), lambda i: (0, 0))]
    args = [proj, proj, small, qg, kvg]
    out_shape = [jax.ShapeDtypeStruct((nrows, qw), BF16), jax.ShapeDtypeStruct((nrows, MLA_KV_IN), BF16)]
    out_specs = [pl.BlockSpec((tr, qw), lambda i: (i, 0)), pl.BlockSpec((tr, MLA_KV_IN), lambda i: (i, 0))]
    if rope:
        per = seq_len // tr
        in_specs += [pl.BlockSpec((tr, LANE), lambda i: (i % per, 0))] * 2
        args += list(tabs)
    else:
        out_shape.append(jax.ShapeDtypeStruct((nrows, cw), F32))
        out_specs.append(pl.BlockSpec((tr, cw), lambda i: (i, 0)))
    return pl.pallas_call(
        functools.partial(_mla_prep_kernel, rope=rope),
        out_shape=tuple(out_shape),
        grid=(nrows // tr,),
        in_specs=in_specs,
        out_specs=tuple(out_specs),
        compiler_params=_cparams(("parallel",)),
    )(*args)


def _mla_q_kernel(*refs, rope, scale):
    if rope:
        x_ref, w_ref, cos_ref, sin_ref, o_ref = refs
    else:
        x_ref, w_ref, o_ref = refs
    acc = jnp.dot(x_ref[...], w_ref[...], preferred_element_type=F32) * scale
    if rope:
        o_ref[:, :LANE] = acc[:, :LANE].astype(o_ref.dtype)
        o_ref[:, LANE:] = _rope(acc[:, LANE:], cos_ref[...], sin_ref[...], MLA_ROPE // 4).astype(o_ref.dtype)
    else:
        o_ref[...] = acc.astype(o_ref.dtype)


def _mla_q(qdn, w_q, tabs, *, l, seq_len):
    nrows, k = qdn.shape
    rope = tabs is not None
    tm = min(TR_ROWS, seq_len) if rope else TR_ROWS
    in_specs = [pl.BlockSpec((tm, k), lambda i, h: (i, 0)), _lspec(l, (k, MLA_QK), lambda i, h: (0, h))]
    args = [qdn, w_q]
    if rope:
        per = seq_len // tm
        in_specs += [pl.BlockSpec((tm, LANE), lambda i, h: (i % per, 0))] * 2
        args += list(tabs)
    return pl.pallas_call(
        functools.partial(_mla_q_kernel, rope=rope, scale=(MLA_NOPE + MLA_ROPE) ** -0.5),
        out_shape=jax.ShapeDtypeStruct((nrows, MLA_HEADS * MLA_QK), BF16),
        grid=(nrows // tm, MLA_HEADS),
        in_specs=in_specs,
        out_specs=pl.BlockSpec((tm, MLA_QK), lambda i, h: (i, h)),
        compiler_params=_cparams(("parallel", "arbitrary")),
    )(*args)


def _matmul_kernel(x_ref, w_ref, o_ref):
    o_ref[...] = jnp.dot(x_ref[...], w_ref[...], preferred_element_type=F32).astype(o_ref.dtype)


def _mla_kv(kv_in, w_kv, l):
    nrows, k = kv_in.shape
    n = w_kv.shape[2]
    tm = 512 if nrows % 512 == 0 else 256
    return pl.pallas_call(
        _matmul_kernel,
        out_shape=jax.ShapeDtypeStruct((nrows, n), BF16),
        grid=(nrows // tm,),
        in_specs=[pl.BlockSpec((tm, k), lambda i: (i, 0)), _lspec(l, (k, n), lambda i: (0, 0))],
        out_specs=pl.BlockSpec((tm, n), lambda i: (i, 0)),
        compiler_params=_cparams(("parallel",)),
    )(kv_in, w_kv)


def _conv_kernel(x_ref, w_ref, b_ref, o_ref):
    x = x_ref[0].astype(F32)
    seq = x.shape[0]
    edge = 8
    row = lax.broadcasted_iota(jnp.int32, (edge, x.shape[1]), 0)
    pad = SSD_CONV // 2
    acc = x * w_ref[pad:pad + 1, :] + b_ref[...]
    for k in range(SSD_CONV):
        d = k - pad
        if d == 0:
            continue
        if d > 0:
            src = jnp.concatenate([jnp.where(row < d, 0.0, x[:edge]), x[edge:]], axis=0)
        else:
            src = jnp.concatenate([x[:seq - edge], jnp.where(row >= edge + d, 0.0, x[seq - edge:])], axis=0)
        acc = acc + pltpu.roll(src, (-d) % seq, 0) * w_ref[k:k + 1, :]
    o_ref[0] = _silu(acc).astype(o_ref.dtype)


def _ssd_conv(proj, conv_w, conv_b, *, l, seq0, nseq, seq_len, cols):
    npad = proj.shape[1]
    c0, cw = cols["sxbc"]
    tc = 256
    view = proj.reshape(-1, seq_len, npad)
    return pl.pallas_call(
        _conv_kernel,
        out_shape=jax.ShapeDtypeStruct((nseq, seq_len, cw), BF16),
        grid=(nseq, cw // tc),
        in_specs=[pl.BlockSpec((1, seq_len, tc), lambda s, c: (seq0 + s, 0, c0 // tc + c)),
                  _lspec(l, (SSD_CONV, tc), lambda s, c: (0, c)),
                  _lspec(l, (1, tc), lambda s, c: (0, c))],
        out_specs=pl.BlockSpec((1, seq_len, tc), lambda s, c: (s, 0, c)),
        compiler_params=_cparams(("parallel", "parallel")),
    )(view, conv_w, conv_b).reshape(nseq * seq_len, cw)


def _expand_heads(vals, expand):
    q = vals.shape[0]
    hi = vals.astype(BF16)
    lo = (vals - hi.astype(F32)).astype(BF16)
    out = jnp.dot(jnp.concatenate([hi, lo], axis=0), expand, preferred_element_type=F32)
    return out[:q] + out[q:]


def _ssd_kernel(*refs, reverse, zero_init, final, d):
    if final:
        (x_ref, b_ref, c_ref, dtc_ref, dtr_ref, biasc_ref, biasr_ref, alogc_ref, alogr_ref, h0_ref,
         yf_ref, z_ref, dskip_ref, ng_ref, y_ref, hout_ref, st_ref) = refs
    else:
        (x_ref, b_ref, c_ref, dtc_ref, dtr_ref, biasc_ref, biasr_ref, alogc_ref, alogr_ref, h0_ref,
         y_ref, hout_ref, st_ref) = refs
    q = SSD_CHUNK
    nseq = x_ref.shape[0]
    nh = dtc_ref.shape[2] // 2
    hpg = nh // SSD_GROUPS
    c = pl.program_id(1)

    @pl.when(c == 0)
    def _():
        for sb in range(nseq):
            for g in range(SSD_GROUPS):
                if zero_init:
                    st_ref[sb, g] = jnp.zeros(st_ref.shape[2:], F32)
                else:
                    st_ref[sb, g] = h0_ref[sb, g].T

    lo = d * nh
    ii = lax.broadcasted_iota(jnp.int32, (q, q), 0)
    jj = lax.broadcasted_iota(jnp.int32, (q, q), 1)
    causal = (jj >= ii) if reverse else (jj <= ii)
    tri = causal.astype(F32)
    tri_t = ((ii >= jj) if reverse else (ii <= jj)).astype(F32)
    lane = lax.broadcasted_iota(jnp.int32, (q, LANE), 1)
    lane_row = lax.broadcasted_iota(jnp.int32, (1, LANE), 1)
    neg_a_c = -jnp.exp(alogc_ref[:, lo:lo + nh])
    neg_a_r = -jnp.exp(alogr_ref[lo:lo + nh, :])
    head_of_lane = lax.broadcasted_iota(jnp.int32, (nh, nh * SSD_HEAD_DIM), 1) // SSD_HEAD_DIM
    expand = (lax.broadcasted_iota(jnp.int32, (nh, nh * SSD_HEAD_DIM), 0) == head_of_lane).astype(BF16)

    for sb in range(nseq):
        dt_c = _softplus(dtc_ref[sb, :, lo:lo + nh] + biasc_ref[:, lo:lo + nh])
        dt_r = _softplus(dtr_ref[sb, lo:lo + nh, :] + biasr_ref[lo:lo + nh, :])
        a_c = dt_c * neg_a_c
        a_r = dt_r * neg_a_r
        cum_c = jnp.dot(tri, a_c, precision=HIGHEST, preferred_element_type=F32)
        cum_r = jnp.dot(a_r, tri_t, precision=HIGHEST, preferred_element_type=F32)
        cum_dt_r = cum_r - jnp.log(dt_r)
        tot_c = jnp.sum(a_c, axis=0, keepdims=True)
        e_in = _expand_heads(jnp.exp(cum_c), expand)
        w_out = _expand_heads(dt_c * jnp.exp(tot_c - cum_c), expand)
        e_tot = jnp.exp(tot_c)

        x = x_ref[sb]
        y_groups = []
        for g in range(SSD_GROUPS):
            bg = b_ref[sb, :, g * SSD_STATE:(g + 1) * SSD_STATE]
            cg = c_ref[sb, :, g * SSD_STATE:(g + 1) * SSD_STATE]
            cb = lax.dot_general(cg, bg, (((1,), (1,)), ((), ())), preferred_element_type=F32)
            st = st_ref[sb, g]
            y_in = jnp.dot(cg, st.astype(BF16), preferred_element_type=F32)
            y_pairs, xs_pairs, dec_pairs = [], [], []
            for pr in range(hpg // 2):
                h0 = g * hpg + 2 * pr
                x_pair = x[:, h0 * SSD_HEAD_DIM:(h0 + 2) * SSD_HEAD_DIM]
                att = []
                for h in (h0, h0 + 1):
                    seg = cum_c[:, h:h + 1] - cum_dt_r[h:h + 1, :]
                    att.append((cb * jnp.exp(jnp.where(causal, seg, -jnp.inf))).astype(BF16))
                yd = jnp.dot(jnp.concatenate(att, axis=0), x_pair, preferred_element_type=F32)
                y_pair = jnp.where(lane < SSD_HEAD_DIM, yd[:q], yd[q:])
                pair = slice(h0 * SSD_HEAD_DIM, (h0 + 2) * SSD_HEAD_DIM)
                y_pair = y_pair + e_in[:, pair] * y_in[:, 2 * pr * SSD_HEAD_DIM:(2 * pr + 2) * SSD_HEAD_DIM]
                y_pairs.append(y_pair)
                xs_pairs.append((x_pair.astype(F32) * w_out[:, pair]).astype(BF16))
                dec_pairs.append(jnp.where(lane_row < SSD_HEAD_DIM, e_tot[:, h0:h0 + 1], e_tot[:, h0 + 1:h0 + 2]))
            xs_dec = jnp.concatenate(xs_pairs, axis=1)
            upd = lax.dot_general(bg, xs_dec, (((0,), (0,)), ((), ())), preferred_element_type=F32)
            st_ref[sb, g] = st * jnp.concatenate(dec_pairs, axis=1) + upd
            y_groups.append(jnp.concatenate(y_pairs, axis=1))
        y = jnp.concatenate(y_groups, axis=1)

        if final:
            y = y + yf_ref[sb] + dskip_ref[...] * x.astype(F32)
            y = y * _silu(z_ref[sb].astype(F32))
            y_ref[sb] = _rms(y, ng_ref[...]).astype(y_ref.dtype)
        else:
            y_ref[sb] = y

    @pl.when(c == pl.num_programs(1) - 1)
    def _():
        for sb in range(nseq):
            for g in range(SSD_GROUPS):
                hout_ref[sb, g] = st_ref[sb, g].T


def _ssd_pass(xbc, dt_col, dt_row, dt_bias, a_log, h0, extra, out_buf, state_buf, *, l, d, nseq, seq_len, row0,
              cols):
    q = SSD_CHUNK
    sb = SSD_SEQ_PER_STEP
    nc = seq_len // q
    nh2 = dt_col.shape[1]
    inner = cols["sz"][1]
    gw = inner // SSD_GROUPS
    bw = SSD_GROUPS * SSD_STATE
    assert nseq % sb == 0 and (row0 // seq_len) % sb == 0
    s0 = row0 // seq_len // sb
    reverse = d == 1
    final = extra is not None
    cidx = (lambda c: nc - 1 - c) if reverse else (lambda c: c)
    zero_init = h0 is None
    st_block = (sb, SSD_GROUPS, gw, SSD_STATE)
    if zero_init:
        h0 = jnp.zeros(st_block, F32)
        h0_spec = pl.BlockSpec(st_block, lambda s, c: (0, 0, 0, 0))
    else:
        h0_spec = pl.BlockSpec((None, None) + st_block, lambda s, c: (l, d, s, 0, 0, 0))
    per_seq = lambda a: a.reshape(-1, seq_len, a.shape[-1])
    xbc3 = per_seq(xbc)
    in_specs = [pl.BlockSpec((sb, q, inner), lambda s, c: (s, cidx(c), 0)),
                pl.BlockSpec((sb, q, bw), lambda s, c: (s, cidx(c), inner // bw)),
                pl.BlockSpec((sb, q, bw), lambda s, c: (s, cidx(c), inner // bw + 1)),
                pl.BlockSpec((sb, q, nh2), lambda s, c: (s0 + s, cidx(c), 0)),
                pl.BlockSpec((sb, nh2, q), lambda s, c: (s, 0, cidx(c))),
                _lspec(l, (1, nh2), lambda s, c: (0, 0)),
                _lspec(l, (nh2, 1), lambda s, c: (0, 0)),
                _lspec(l, (1, nh2), lambda s, c: (0, 0)),
                _lspec(l, (nh2, 1), lambda s, c: (0, 0)),
                h0_spec]
    args = [xbc3, xbc3, xbc3, per_seq(dt_col), dt_row, dt_bias[0], dt_bias[1], a_log[0], a_log[1], h0]
    kern = functools.partial(_ssd_kernel, reverse=reverse, zero_init=zero_init, final=final, d=d)
    if final:
        y_fwd, proj, d_skip, norm_g = extra
        in_specs += [pl.BlockSpec((sb, q, inner), lambda s, c: (s, cidx(c), 0)),
                     pl.BlockSpec((sb, q, inner), lambda s, c: (s0 + s, cidx(c), cols["sz"][0] // inner)),
                     _lspec(l, (1, inner), lambda s, c: (0, 0)),
                     _lspec(l, (1, inner), lambda s, c: (0, 0))]
        args += [y_fwd, per_seq(proj), d_skip, norm_g]
        out3 = (out_buf.shape[0] // seq_len, seq_len, inner)
        if isinstance(out_buf, jax.ShapeDtypeStruct):
            out_buf = jax.ShapeDtypeStruct(out3, out_buf.dtype)
        else:
            out_buf = out_buf.reshape(out3)
        y_shape = jax.ShapeDtypeStruct(out3, out_buf.dtype)
        y_spec = pl.BlockSpec((sb, q, inner), lambda s, c: (s0 + s, cidx(c), 0))
    else:
        y_shape = jax.ShapeDtypeStruct((nseq, seq_len, inner), F32)
        y_spec = pl.BlockSpec((sb, q, inner), lambda s, c: (s, cidx(c), 0))
    if state_buf is None:
        st_shape = jax.ShapeDtypeStruct((nseq,) + st_block[1:], F32)
        st_spec = pl.BlockSpec(st_block, lambda s, c: (s, 0, 0, 0))
    else:
        st_shape = jax.ShapeDtypeStruct(state_buf.shape, state_buf.dtype)
        st_spec = pl.BlockSpec((sb, None, None) + st_block[1:], lambda s, c: (s, l, d, 0, 0, 0))
    kern, aliases = _in_place(kern, args, in_specs, {0: out_buf if final else None, 1: state_buf})
    y, st = pl.pallas_call(
        kern,
        out_shape=(y_shape, st_shape),
        grid=(nseq // sb, nc),
        in_specs=in_specs,
        out_specs=(y_spec, st_spec),
        scratch_shapes=[pltpu.VMEM((sb, SSD_GROUPS, SSD_STATE, gw), F32)],
        input_output_aliases=aliases,
        compiler_params=_cparams(("parallel", "arbitrary")),
    )(*args)
    return (y.reshape(-1, inner) if final else y), st


def _s5_kernel(u_ref, wt_ref, ws_ref, wc_ref, a_ref, h0_ref, y_ref, hout_ref, s_scr, hin_scr, *, nseq, nc):
    gb = u_ref.shape[1]
    rows = u_ref.shape[3]
    depth = S5_CHUNK * S5_GROUP_CH
    npair = gb // 2
    w = npair * LANE
    contract_rows = (((0,), (0,)), ((), ()))
    contract_cols = (((1,), (1,)), ((), ()))

    def u_t(g):
        return u_ref[:, g].reshape(depth, rows)

    for p in range(npair):
        s = (lax.dot_general(u_t(2 * p), ws_ref[2 * p], contract_rows, preferred_element_type=F32)
             + lax.dot_general(u_t(2 * p + 1), ws_ref[2 * p + 1], contract_rows, preferred_element_type=F32))
        for comp in range(4):
            s_scr[:, comp * w + p * LANE:comp * w + (p + 1) * LANE] = s[:, comp * LANE:(comp + 1) * LANE]

    af_re, af_im, ab_re, ab_im = a_ref[0], a_ref[1], a_ref[2], a_ref[3]

    def one_sequence(sq, _):
        def step(i, carry):
            hf_re, hf_im, hb_re, hb_im = carry
            rf = sq * nc + i
            rb = sq * nc + nc - 1 - i
            hin_scr[pl.ds(rf, 1), 0:w] = hf_re
            hin_scr[pl.ds(rf, 1), w:2 * w] = hf_im
            hin_scr[pl.ds(rb, 1), 2 * w:3 * w] = hb_re
            hin_scr[pl.ds(rb, 1), 3 * w:4 * w] = hb_im
            sf_re = s_scr[pl.ds(rf, 1), 0:w]
            sf_im = s_scr[pl.ds(rf, 1), w:2 * w]
            sb_re = s_scr[pl.ds(rb, 1), 2 * w:3 * w]
            sb_im = s_scr[pl.ds(rb, 1), 3 * w:4 * w]
            return (af_re * hf_re - af_im * hf_im + sf_re,
                    af_re * hf_im + af_im * hf_re + sf_im,
                    ab_re * hb_re - ab_im * hb_im + sb_re,
                    ab_re * hb_im + ab_im * hb_re + sb_im)

        last = lax.fori_loop(0, nc, step, tuple(h0_ref[sq, comp] for comp in range(4)))
        for comp in range(4):
            hout_ref[sq, comp] = last[comp]
        return 0

    lax.fori_loop(0, nseq, one_sequence, 0)

    for p in range(npair):
        hin = jnp.concatenate([hin_scr[:, comp * w + p * LANE:comp * w + (p + 1) * LANE] for comp in range(4)],
                              axis=1).astype(BF16)
        for e in range(2):
            g = 2 * p + e
            y = (jnp.dot(wt_ref[g], u_t(g), preferred_element_type=F32)
                 + lax.dot_general(wc_ref[g], hin, contract_cols, preferred_element_type=F32))
            y_ref[:, g] = y.astype(y_ref.dtype).reshape(S5_CHUNK, S5_GROUP_CH, rows)


def _s5_scan(u_t, w_toep, w_state, w_carry, a_pow, h0, y_buf, *, l, nseq, seq_len, row0):
    nt, ng, nh, _ = u_t.shape
    nc = seq_len // S5_CHUNK
    rows = nseq * nc
    rblk = row0 // rows
    gb = S5_GB
    kw = nt * nh
    w = (gb // 2) * LANE
    kern = functools.partial(_s5_kernel, nseq=nseq, nc=nc)
    st_block = (nseq, 4, 1, w)
    if h0 is None:
        h0 = jnp.zeros((nseq, 4, 1, ng * S5_STATE), F32)
        h0_spec = pl.BlockSpec(st_block, lambda j: (0, 0, 0, j))
    else:
        h0_spec = _lspec(l, st_block, lambda j: (0, 0, 0, j))
    args = [u_t, w_toep, w_state, w_carry, a_pow, h0]
    in_specs = [pl.BlockSpec((nt, gb, nh, rows), lambda j: (0, j, 0, rblk)),
                _lspec(l, (gb, kw, kw), lambda j: (j, 0, 0)),
                _lspec(l, (gb, kw, 4 * LANE), lambda j: (j, 0, 0)),
                _lspec(l, (gb, kw, 4 * LANE), lambda j: (j, 0, 0)),
                _lspec(l, (4, 1, w), lambda j: (0, 0, j)),
                h0_spec]
    kern, aliases = _in_place(kern, args, in_specs, {0: y_buf})
    return pl.pallas_call(
        kern,
        out_shape=(jax.ShapeDtypeStruct(y_buf.shape, y_buf.dtype),
                   jax.ShapeDtypeStruct((nseq, 4, 1, ng * S5_STATE), F32)),
        grid=(ng // gb,),
        in_specs=in_specs,
        out_specs=(pl.BlockSpec((nt, gb, nh, rows), lambda j: (0, j, 0, rblk)),
                   pl.BlockSpec((nseq, 4, 1, w), lambda j: (0, 0, 0, j))),
        scratch_shapes=[pltpu.VMEM((rows, 4 * w), F32), pltpu.VMEM((rows, 4 * w), F32)],
        input_output_aliases=aliases,
        compiler_params=_cparams(("parallel",)),
    )(*args)


def _s5_weights(lam_re, lam_im, log_step, b_re, b_im, c_re, c_im):
    t = S5_CHUNK
    ng, ns, nh = b_re.shape
    step = jnp.exp(log_step)[..., None]
    lr, li = lam_re * step, lam_im * step
    n = jnp.arange(t + 1, dtype=F32)[:, None, None, None]
    mag = jnp.exp(lr[None] * n)
    pw_re, pw_im = mag * jnp.cos(li[None] * n), mag * jnp.sin(li[None] * n)
    a_re, a_im = pw_re[1], pw_im[1]
    den = lam_re * lam_re + lam_im * lam_im
    k_re = ((a_re - 1.0) * lam_re + a_im * lam_im) / den
    k_im = (a_im * lam_re - (a_re - 1.0) * lam_im) / den
    bt_re, bt_im = b_re.transpose(0, 2, 1), b_im.transpose(0, 2, 1)
    w_re = k_re[:, :, None, :] * bt_re[None] - k_im[:, :, None, :] * bt_im[None]
    w_im = k_re[:, :, None, :] * bt_im[None] + k_im[:, :, None, :] * bt_re[None]

    def times_pow(idx, d, x_re, x_im):
        p_re, p_im = pw_re[idx, d][:, :, None, :], pw_im[idx, d][:, :, None, :]
        return p_re * x_re[None] - p_im * x_im[None], p_re * x_im[None] + p_im * x_re[None]

    ti = jnp.arange(t)
    kern = []
    for d in range(2):
        aw_re, aw_im = times_pow(ti, d, w_re[d], w_im[d])
        kern.append(jnp.einsum("gop,tgip->tgoi", c_re, aw_re, precision=HIGHEST)
                    - jnp.einsum("gop,tgip->tgoi", c_im, aw_im, precision=HIGHEST))
    k_all = jnp.concatenate([kern[0][:0:-1], (kern[0][0] + kern[1][0])[None], kern[1][1:]], axis=0)
    k_flat = k_all.transpose(1, 2, 0, 3).reshape(ng, nh, (2 * t - 1) * nh)
    toep_t = jnp.stack([k_flat[:, :, (t - 1 - to) * nh:(2 * t - 1 - to) * nh] for to in range(t)], axis=1)
    toep_t = toep_t.reshape(ng, t * nh, t * nh)

    side = jax.nn.one_hot(jnp.arange(ng) % 2, 2, dtype=F32)
    width = 4 * 2 * ns

    def lane_pow(p, idx_f, idx_b):
        x = jnp.stack([p[idx_f, 0], p[idx_f, 0], p[idx_b, 1], p[idx_b, 1]], axis=2)
        x = x[:, :, :, None, :] * side[None, :, None, :, None]
        return x.reshape(t, ng, 1, width).transpose(1, 0, 2, 3)

    def lane_coef(comps):
        x = jnp.stack(comps, axis=2)[:, :, :, None, :]
        return jnp.broadcast_to(x, (ng, nh, 4, 2, ns)).reshape(ng, 1, nh, width)

    def state_matrix(idx_f, idx_b, re_coef, im_coef):
        mat = (lane_pow(pw_re, idx_f, idx_b) * lane_coef(re_coef)
               + lane_pow(pw_im, idx_f, idx_b) * lane_coef(im_coef))
        return mat.reshape(ng, t * nh, width).astype(BF16)

    w_state = state_matrix(t - 1 - ti, ti, (w_re[0], w_im[0], w_re[1], w_im[1]),
                           (-w_im[0], w_re[0], -w_im[1], w_re[1]))
    carry_t = state_matrix(ti + 1, t - ti, (c_re, -c_im, c_re, -c_im), (-c_im, -c_re, -c_im, -c_re))
    a_pow = jnp.stack([pw_re[t, 0], pw_im[t, 0], pw_re[t, 1], pw_im[t, 1]], axis=0).reshape(4, 1, ng * ns)
    return toep_t.astype(BF16), w_state, carry_t, a_pow


def _glu_kernel(y_ref, u_ref, d_ref, wv_ref, wg_ref, o_ref, vs_ref):
    @pl.when(pl.program_id(1) == 0)
    def _():
        v = y_ref[...].astype(F32) + d_ref[...] * u_ref[...].astype(F32)
        v = 0.5 * v * (1.0 + jnp.tanh(math.sqrt(2.0 / math.pi) * (v + 0.044715 * (v * v * v))))
        vs_ref[...] = v.astype(BF16)

    v = vs_ref[...]
    a = jnp.dot(v, wv_ref[...], preferred_element_type=F32)
    b = jnp.dot(v, wg_ref[...], preferred_element_type=F32)
    o_ref[...] = (a * jax.nn.sigmoid(b)).astype(o_ref.dtype)


def _s5_glu(y, proj, d_skip, w_glu, l, dims, cols):
    m, wd = y.shape
    tm, tn = min(TM, dims["lat_len"]), 512
    nj = wd // tn
    return pl.pallas_call(
        _glu_kernel,
        out_shape=jax.ShapeDtypeStruct((m, wd), BF16),
        grid=(m // tm, nj),
        in_specs=[pl.BlockSpec((tm, wd), lambda i, j: (i, 0)),
                  pl.BlockSpec((tm, wd), lambda i, j: (i, cols["s5u"][0] // wd)),
                  _lspec(l, (1, wd), lambda i, j: (0, 0)),
                  _lspec(l, (wd, tn), lambda i, j: (0, j)),
                  _lspec(l, (wd, tn), lambda i, j: (0, j + nj))],
        out_specs=pl.BlockSpec((tm, tn), lambda i, j: (i, j)),
        scratch_shapes=[pltpu.VMEM((tm, wd), BF16)],
        compiler_params=_cparams(("parallel", "arbitrary")),
    )(y, proj, d_skip, w_glu, w_glu)


def _columns(d_model, branch_w):
    kvw = GQA_KV_HEADS * GQA_HEAD_DIM
    xbc = branch_w + 2 * SSD_GROUPS * SSD_STATE
    order = (("gate", N_BRANCH * d_model), ("gq", branch_w), ("sz", branch_w), ("s5u", branch_w),
             ("sxbc", xbc), ("mqd", MLA_Q_LORA), ("ckv", MLA_KV_LORA), ("gk", kvw), ("gv", kvw))
    cols, off = {}, 0
    for name, width in order:
        cols[name] = (off, width)
        off += width
    return cols, off


def _prep_w_in(w_in, d_model, branch_w, n_dt):
    kvw = GQA_KV_HEADS * GQA_HEAD_DIM
    xbc = branch_w + 2 * SSD_GROUPS * SSD_STATE
    splits = (N_BRANCH * d_model, branch_w, kvw, kvw, branch_w, xbc, n_dt, MLA_Q_LORA, MLA_KV_LORA + MLA_ROPE, branch_w)
    bounds, acc = [], 0
    for wd in splits[:-1]:
        acc += wd
        bounds.append(acc)
    gate, gq, gk, gv, sz, sxbc, sdt, mqd, mkvd, s5u = jnp.split(w_in, bounds, axis=-1)
    ckv, kpe = mkvd[..., :MLA_KV_LORA], mkvd[..., MLA_KV_LORA:]
    main = jnp.concatenate([gate, gq, sz, s5u, sxbc, mqd, ckv, gk, gv], axis=-1).astype(BF16)
    pad = jnp.zeros(w_in.shape[:-1] + (LANE - MLA_ROPE - n_dt,), w_in.dtype)
    small = jnp.concatenate([kpe, sdt, pad], axis=-1).astype(BF16)
    return main, small


def _prep_mla(w_uq, w_ukv):
    depth = w_uq.shape[0]
    qk = MLA_NOPE + MLA_ROPE
    wq = w_uq.reshape(depth, MLA_Q_LORA, MLA_HEADS, qk)
    wq = jnp.pad(wq, ((0, 0), (0, 0), (0, 0), (0, MLA_QK - qk))).reshape(depth, MLA_Q_LORA, MLA_HEADS * MLA_QK)
    wkv = w_ukv.reshape(depth, MLA_KV_LORA, MLA_HEADS, MLA_NOPE + LANE)
    k_nope, v = wkv[..., :MLA_NOPE], wkv[..., MLA_NOPE:]
    k_top = jnp.pad(k_nope, ((0, 0), (0, 0), (0, 0), (0, MLA_QK - MLA_NOPE)))
    eye = jnp.eye(LANE, MLA_QK, k=MLA_NOPE, dtype=w_ukv.dtype) * (jnp.arange(LANE) < MLA_ROPE)[:, None]
    k_bot = jnp.broadcast_to(eye[None, :, None, :], (depth, LANE, MLA_HEADS, MLA_QK))
    k_all = jnp.concatenate([k_top, k_bot], axis=1).reshape(depth, MLA_KV_IN, MLA_HEADS * MLA_QK)
    v_all = jnp.pad(v, ((0, 0), (0, LANE), (0, 0), (0, 0))).reshape(depth, MLA_KV_IN, MLA_HEADS * LANE)
    return wq.astype(BF16), jnp.concatenate([k_all, v_all], axis=-1).astype(BF16)


def kernel(x_prompt, x_sample, cache_gqa_k, cache_gqa_v, cache_mla_ckv, cache_mla_kpe, state_ssd, state_s5, c, c_ctx, norm1_g, norm2_g, w_mod, b_mod, w_in, gqa_qn_g, gqa_kn_g, ssd_conv_w, ssd_conv_b, ssd_a_log, ssd_dt_bias, ssd_d, ssd_norm_g, mla_qn_g, mla_w_uq, mla_kvn_g, mla_w_ukv, s5_lam_re, s5_lam_im, s5_log_step, s5_b_re, s5_b_im, s5_c_re, s5_c_im, s5_d, s5_w_glu, w_branch, w_out, w_ffn_in, w_ffn_out, final_g):
    nb_ctx, len_ctx, d_model = x_prompt.shape
    nb_lat, len_lat, _ = x_sample.shape
    depth = w_in.shape[0]
    past = cache_gqa_k.shape[2]
    branch_w = w_branch.shape[2]
    n_heads = ssd_d.shape[1]
    ctx_rows, lat_rows = nb_ctx * len_ctx, nb_lat * len_lat
    m = ctx_rows + lat_rows
    dims = {"ctx_rows": ctx_rows, "lat_len": len_lat}
    cols, _ = _columns(d_model, branch_w)
    kvw = GQA_KV_HEADS * GQA_HEAD_DIM
    n_s5 = branch_w // S5_GROUP_CH
    gw = branch_w // SSD_GROUPS

    row = lambda a: a.reshape(depth, 1, -1)
    col = lambda a: a.reshape(depth, -1, 1)
    w_main, w_small = _prep_w_in(w_in, d_model, branch_w, 2 * n_heads)
    w_q, w_kv = _prep_mla(mla_w_uq, mla_w_ukv)
    s5_toep, s5_state, s5_carry, s5_apow = jax.vmap(_s5_weights)(
        s5_lam_re, s5_lam_im, s5_log_step, s5_b_re, s5_b_im, s5_c_re, s5_c_im)
    tabs_a = _rope_tables(len_lat, GQA_HEAD_DIM)
    tabs_c = _rope_tables(len_lat, MLA_ROPE)
    n_mod = -(-(nb_lat + 1) // 8) * 8
    cvec = jnp.concatenate([c_ctx[None], c, jnp.zeros((n_mod - nb_lat - 1, d_model), F32)], axis=0)
    mod = _modulation(cvec, w_mod, b_mod).reshape(depth, n_mod, 6, 1, d_model)
    norm1_g, norm2_g = row(norm1_g), row(norm2_g)
    gqa_qn_g, gqa_kn_g, mla_qn_g, mla_kvn_g = row(gqa_qn_g), row(gqa_kn_g), row(mla_qn_g), row(mla_kvn_g)
    conv_b, ssd_norm_g, s5_d = row(ssd_conv_b), row(ssd_norm_g), row(s5_d)
    d_skip = row(jnp.repeat(ssd_d, SSD_HEAD_DIM, axis=-1))
    dt_bias, a_log = (row(ssd_dt_bias), col(ssd_dt_bias)), (row(ssd_a_log), col(ssd_a_log))
    w_glu, w_branch, w_out = s5_w_glu.astype(BF16), w_branch.astype(BF16), w_out.astype(BF16)
    w_ffn_in, w_ffn_out = w_ffn_in.astype(BF16), w_ffn_out.astype(BF16)
    cache_k = cache_gqa_k.reshape(nb_lat, depth, past, kvw).astype(BF16)
    cache_v = cache_gqa_v.reshape(nb_lat, depth, past, kvw).astype(BF16)
    zpad = jnp.zeros(cache_mla_kpe.shape[:-1] + (LANE - MLA_ROPE,), F32)
    cache_kv = jnp.concatenate([cache_mla_ckv, cache_mla_kpe, zpad], axis=-1).astype(BF16)
    h0_ssd = jnp.moveaxis(state_ssd, (1, 2), (0, 1)).reshape(depth, 2, nb_lat, SSD_GROUPS, gw, SSD_STATE)
    h0_s5 = jnp.moveaxis(state_s5, 1, 0).reshape(depth, nb_lat, 4, 1, n_s5 * S5_STATE)
    ssd_new = jnp.zeros((nb_ctx, depth, 2, SSD_GROUPS, gw, SSD_STATE), F32)
    new_buf = lambda: jnp.zeros((m, branch_w), BF16)
    gqa = dict(n_kv=GQA_KV_HEADS, n_rep=cols["gq"][1] // kvw, dk=GQA_HEAD_DIM, dv=GQA_HEAD_DIM)
    mla = dict(n_kv=MLA_HEADS, n_rep=1, dk=MLA_QK, dv=LANE, k_off=0, v_off=MLA_HEADS * MLA_QK)
    nchunk = m // S5_CHUNK

    x = jnp.concatenate([x_prompt.reshape(ctx_rows, d_model), x_sample.reshape(lat_rows, d_model)], axis=0)
    new = []
    for l in range(depth):
        proj, small = _in_proj(x, norm1_g, mod, w_main, w_small, l, dims)

        qn, kn, k_own = _gqa_prep(proj, gqa_qn_g, gqa_kn_g, None, l=l, row0=0, nrows=ctx_rows, seq_len=len_ctx,
                                  cols=cols)
        o_a = _attention(qn, kn.reshape(nb_ctx, len_ctx, kvw), proj.reshape(-1, len_ctx, proj.shape[1]), new_buf(),
                         nseq=nb_ctx, seq_len=len_ctx, tq=len_ctx, kv_per_step=GQA_KV_HEADS, k_off=0,
                         v_off=cols["gv"][0], row0=0, **gqa)
        v_own = proj[:ctx_rows, cols["gv"][0]:cols["gv"][0] + kvw]
        qn, kn = _gqa_prep(proj, gqa_qn_g, gqa_kn_g, tabs_a, l=l, row0=ctx_rows, nrows=lat_rows, seq_len=len_lat,
                           cols=cols)
        k_lat = jnp.concatenate([kn.reshape(nb_lat, len_lat, kvw), cache_k[:, l]], axis=1)
        v_lat = jnp.concatenate([proj[ctx_rows:, cols["gv"][0]:cols["gv"][0] + kvw].reshape(nb_lat, len_lat, kvw),
                                 cache_v[:, l]], axis=1)
        o_a = _attention(qn, k_lat, v_lat, o_a, nseq=nb_lat, seq_len=len_lat, tq=min(TQ_LAT, len_lat),
                         kv_per_step=1, k_off=0, v_off=0, row0=ctx_rows, **gqa)

        qdn, kv_in, ckv_own = _mla_prep(proj, small, mla_qn_g, mla_kvn_g, None, l=l, row0=0, nrows=ctx_rows,
                                        seq_len=len_ctx, cols=cols)
        kpe_own = small[:ctx_rows, :MLA_ROPE]
        kv = _mla_kv(kv_in, w_kv, l).reshape(nb_ctx, len_ctx, -1)
        o_c = _attention(_mla_q(qdn, w_q, None, l=l, seq_len=len_ctx), kv, kv, new_buf(),
                         nseq=nb_ctx, seq_len=len_ctx, tq=len_ctx, kv_per_step=MLA_HEADS, row0=0, **mla)
        qdn, kv_in = _mla_prep(proj, small, mla_qn_g, mla_kvn_g, tabs_c, l=l, row0=ctx_rows, nrows=lat_rows,
                               seq_len=len_lat, cols=cols)
        kv_in = jnp.concatenate([kv_in.reshape(nb_lat, len_lat, MLA_KV_IN), cache_kv[:, l]], axis=1)
        kv = _mla_kv(kv_in.reshape(-1, MLA_KV_IN), w_kv, l).reshape(nb_lat, len_lat + past, -1)
        o_c = _attention(_mla_q(qdn, w_q, tabs_c, l=l, seq_len=len_lat), kv, kv, o_c,
                         nseq=nb_lat, seq_len=len_lat, tq=min(TQ_LAT, len_lat), kv_per_step=MLA_LAT_HEADS_PER_STEP,
                         row0=ctx_rows, **mla)

        dt_col = small[:, MLA_ROPE:MLA_ROPE + 2 * n_heads]
        o_b = new_buf()
        for nseq, seq_len, row0, h0 in ((nb_ctx, len_ctx, 0, None), (nb_lat, len_lat, ctx_rows, h0_ssd)):
            xbc = _ssd_conv(proj, ssd_conv_w, conv_b, l=l, seq0=row0 // seq_len, nseq=nseq, seq_len=seq_len,
                            cols=cols)
            dt_row = dt_col[row0:row0 + nseq * seq_len].reshape(nseq, seq_len, -1).transpose(0, 2, 1)
            common = dict(l=l, nseq=nseq, seq_len=seq_len, row0=row0, cols=cols)
            keep = h0 is None
            y_f, st = _ssd_pass(xbc, dt_col, dt_row, dt_bias, a_log, h0, None, None, ssd_new if keep else None,
                                d=0, **common)
            ssd_new = st if keep else ssd_new
            o_b, st = _ssd_pass(xbc, dt_col, dt_row, dt_bias, a_log, h0, (y_f, proj, d_skip, ssd_norm_g), o_b,
                                ssd_new if keep else None, d=1, **common)
            ssd_new = st if keep else ssd_new

        u = proj[:, cols["s5u"][0]:cols["s5u"][0] + branch_w]
        u_t = u.reshape(nchunk, S5_CHUNK * branch_w).T.reshape(S5_CHUNK, n_s5, S5_GROUP_CH, nchunk)
        s5_args = (u_t, s5_toep, s5_state, s5_carry, s5_apow)
        y_t, s5_new = _s5_scan(*s5_args, None, jnp.zeros(u_t.shape, BF16), l=l, nseq=nb_ctx,
                               seq_len=len_ctx, row0=0)
        y_t, _ = _s5_scan(*s5_args, h0_s5, y_t, l=l, nseq=nb_lat, seq_len=len_lat, row0=ctx_rows // S5_CHUNK)
        y = y_t.reshape(S5_CHUNK * branch_w, nchunk).T.reshape(m, branch_w)
        o_d = _s5_glu(y, proj, s5_d, w_glu, l, dims, cols)

        mixed = _branch_mix([o_a, o_b, o_c, o_d], proj, w_branch, cols["gate"][0], l, dims)
        x = _resid_proj(mixed, w_out, x, mod, 2, 2 * TM, l, dims)
        hidden = _ffn_in(x, norm2_g, mod, w_ffn_in, l, dims)
        x = _resid_proj(hidden, w_ffn_out, x, mod, 5, TM, l, dims)
        new.append((k_own.reshape(nb_ctx, len_ctx, GQA_KV_HEADS, GQA_HEAD_DIM),
                    v_own.astype(F32).reshape(nb_ctx, len_ctx, GQA_KV_HEADS, GQA_HEAD_DIM),
                    ckv_own.reshape(nb_ctx, len_ctx, MLA_KV_LORA),
                    kpe_own.reshape(nb_ctx, len_ctx, MLA_ROPE),
                    s5_new.reshape(nb_ctx, 2, 2, n_s5, S5_STATE)))

    stacked = [jnp.stack([layer_out[i] for layer_out in new], axis=1) for i in range(5)]
    return (_final_norm(x, final_g, 0, ctx_rows).reshape(x_prompt.shape),
            _final_norm(x, final_g, ctx_rows, lat_rows).reshape(x_sample.shape),
            *stacked[:4],
            ssd_new.reshape(nb_ctx, depth, 2, n_heads, SSD_HEAD_DIM, SSD_STATE),
            stacked[4])
```
